```python
import math
import jax, jax.numpy as jnp
from jax import lax
import numpy as np

D_MODEL = 2048
BATCH = 2
SEQ = 8192
DEPTH = 2

N_MIXERS = 2
N_HEADS = 16
HEAD_DIM = D_MODEL // N_HEADS
NSA_KV_GROUPS = 4
NSA_HPG = N_HEADS // NSA_KV_GROUPS
CMP_LEN = 32
CMP_STRIDE = 16
SEL_LEN = 64
SEL_TOP = 16
WINDOW = 512
N_BRANCH = 3
NSA_Q_BLOCK = 64
SEL_FORCE = 1e4
MLA_KV_RANK = 256
IDX_HEADS = 8
IDX_DIM = 64
IDX_TOPK = 256
DSA_Q_BLOCK = 128
REL_BUCKETS = 32
REL_MAX_DIST = 128
D_FF = 5632
N_EXPERTS = 8
MOE_TOP_K = 2
D_FF_EXPERT = 7168
MOE_ROW_BLOCK = 512
ALPHA = (2 * DEPTH) ** 0.25
BETA = (8 * DEPTH) ** -0.25
LN_EPS = 1e-5
RMS_EPS = 1e-6
NEG_INF = -1e30
NSA_SIZES = (N_HEADS * HEAD_DIM,) + (NSA_KV_GROUPS * HEAD_DIM,) * 6 + (N_BRANCH * N_HEADS,)
DSA_SIZES = (N_HEADS * HEAD_DIM, MLA_KV_RANK, IDX_HEADS * IDX_DIM, IDX_DIM, IDX_HEADS)

kernel_name = "nsa_dsa_interleaved_deepnorm_moe"


def split_cols(y, sizes):
    return jnp.split(y, np.cumsum(sizes)[:-1].tolist(), axis=-1)


def layer_norm(x, g, b):
    xf = x.astype(jnp.float32)
    mu = xf.mean(-1, keepdims=True)
    var = jnp.square(xf - mu).mean(-1, keepdims=True)
    return ((xf - mu) * lax.rsqrt(var + LN_EPS) * g + b).astype(x.dtype)


def rms_norm(x, g):
    xf = x.astype(jnp.float32)
    return (xf * lax.rsqrt(jnp.mean(xf * xf, -1, keepdims=True) + RMS_EPS) * g).astype(x.dtype)


def masked_softmax(s, mask):
    s = jnp.where(mask, s.astype(jnp.float32), NEG_INF)
    return jax.nn.softmax(s, axis=-1) * mask


def rel_bucket(dist):
    n = jnp.maximum(dist, 0)
    exact = REL_BUCKETS // 2
    nf = jnp.maximum(n, exact).astype(jnp.float32)
    large = exact + (jnp.log(nf / exact) / math.log(REL_MAX_DIST / exact)
                     * (REL_BUCKETS - exact)).astype(jnp.int32)
    return jnp.where(n < exact, n, jnp.minimum(large, REL_BUCKETS - 1))


def head_bias(rel_bias, dist):
    tq, n = dist.shape
    bias = rel_bias[rel_bucket(dist)].transpose(2, 0, 1)
    return bias.reshape(NSA_KV_GROUPS, NSA_HPG, tq, n).astype(jnp.float32)


def compress_blocks(kv, cidx, pe, w1, w2):
    blk = kv[:, cidx] + pe[:, None, :]
    hid = jax.nn.gelu(jnp.einsum('bnlgd,lde->bnge', blk, w1))
    return jnp.einsum('bnge,ef->bngf', hid, w2)


def cmp_sel_overlap(n_cmp, n_sel):
    i = np.arange(n_cmp)[:, None]
    j = np.arange(n_sel)[None, :]
    lo = np.maximum(i * CMP_STRIDE, j * SEL_LEN)
    hi = np.minimum(i * CMP_STRIDE + CMP_LEN, (j + 1) * SEL_LEN)
    return (np.maximum(hi - lo, 0) / CMP_LEN).astype(np.float32)


def nsa_mixer(x, w_in, pe_k, w1_k, w2_k, pe_v, w1_v, w2_v, w_out, rel_bias):
    b, s_len, _ = x.shape
    G, J, Dh = NSA_KV_GROUPS, NSA_HPG, HEAD_DIM
    q, kc, vc, ksl, vsl, kw, vw, gl = split_cols(x @ w_in, NSA_SIZES)
    q = q.reshape(b, s_len, N_HEADS, Dh) * (Dh ** -0.5)
    gates = jax.nn.sigmoid(gl.astype(jnp.float32)).astype(x.dtype).reshape(b, s_len, G, J, N_BRANCH)
    kv = lambda t: t.reshape(b, s_len, G, Dh)
    n_cmp = (s_len - CMP_LEN) // CMP_STRIDE + 1
    cidx = np.arange(n_cmp)[:, None] * CMP_STRIDE + np.arange(CMP_LEN)[None, :]
    k_cmp = compress_blocks(kv(kc), cidx, pe_k, w1_k, w2_k)
    v_cmp = compress_blocks(kv(vc), cidx, pe_v, w1_v, w2_v)
    cmp_end = jnp.asarray(np.arange(n_cmp) * CMP_STRIDE + CMP_LEN - 1, jnp.int32)
    n_sel = s_len // SEL_LEN
    n_top = min(SEL_TOP, n_sel)
    overlap = jnp.asarray(cmp_sel_overlap(n_cmp, n_sel))
    k_blk = kv(ksl).reshape(b, n_sel, SEL_LEN, G, Dh).transpose(0, 3, 1, 2, 4)
    v_blk = kv(vsl).reshape(b, n_sel, SEL_LEN, G, Dh).transpose(0, 3, 1, 2, 4)
    pad = ((0, 0), (WINDOW, 0), (0, 0), (0, 0))
    k_win = jnp.pad(kv(kw), pad)
    v_win = jnp.pad(kv(vw), pad)
    bI = jnp.arange(b)[:, None, None, None]
    gI = jnp.arange(G)[None, :, None, None]
    table_g = rel_bias.reshape(REL_BUCKETS, G, J)
    sel_off = jnp.arange(SEL_LEN)
    j_sel = jnp.arange(n_sel)

    def one_block(args):
        qb, q_b, g_b = args
        t = qb * NSA_Q_BLOCK + jnp.arange(NSA_Q_BLOCK)
        qg = q_b.reshape(b, NSA_Q_BLOCK, G, J, Dh)
        d_c = t[:, None] - cmp_end[None, :]
        s_c = jnp.einsum('bqgjd,bngd->bgjqn', qg, k_cmp).astype(jnp.float32) + head_bias(rel_bias, d_c)
        p_c = masked_softmax(s_c, d_c >= 0)
        o_c = jnp.einsum('bgjqn,bngd->bqgjd', p_c.astype(x.dtype), v_cmp)
        score = jnp.einsum('bgjqn,nm->bgqm', p_c, overlap)
        cur = (t // SEL_LEN)[:, None]
        forced = (j_sel == 0) | (j_sel == cur) | (j_sel == cur - 1)
        visible = j_sel * SEL_LEN <= t[:, None]
        score = jnp.where(forced, SEL_FORCE, jnp.where(visible, score, -1.0))
        _, sel = lax.top_k(score, n_top)
        k_s = k_blk[bI, gI, sel].reshape(b, G, NSA_Q_BLOCK, n_top * SEL_LEN, Dh)
        v_s = v_blk[bI, gI, sel].reshape(b, G, NSA_Q_BLOCK, n_top * SEL_LEN, Dh)
        pos = (sel[..., None] * SEL_LEN + sel_off).reshape(b, G, NSA_Q_BLOCK, n_top * SEL_LEN)
        d_s = t[None, None, :, None] - pos
        bias_s = table_g[rel_bucket(d_s), gI].transpose(0, 1, 4, 2, 3).astype(jnp.float32)
        s_s = jnp.einsum('bqgjd,bgqkd->bgjqk', qg, k_s).astype(jnp.float32) + bias_s
        p_s = masked_softmax(s_s, (d_s >= 0)[:, :, None])
        o_s = jnp.einsum('bgjqk,bgqkd->bqgjd', p_s.astype(x.dtype), v_s)
        start = qb * NSA_Q_BLOCK
        k_w = lax.dynamic_slice_in_dim(k_win, start, WINDOW + NSA_Q_BLOCK, axis=1)
        v_w = lax.dynamic_slice_in_dim(v_win, start, WINDOW + NSA_Q_BLOCK, axis=1)
        spos = start - WINDOW + jnp.arange(WINDOW + NSA_Q_BLOCK)
        d_w = t[:, None] - spos[None, :]
        m_w = (d_w >= 0) & (d_w < WINDOW) & (spos >= 0)[None, :]
        s_w = jnp.einsum('bqgjd,bkgd->bgjqk', qg, k_w).astype(jnp.float32) + head_bias(rel_bias, d_w)
        p_w = masked_softmax(s_w, m_w)
        o_w = jnp.einsum('bgjqk,bkgd->bqgjd', p_w.astype(x.dtype), v_w)
        return g_b[..., 0:1] * o_c + g_b[..., 1:2] * o_s + g_b[..., 2:3] * o_w

    nqb = s_len // NSA_Q_BLOCK
    q_blocks = q.reshape(b, nqb, NSA_Q_BLOCK, N_HEADS, Dh).swapaxes(0, 1)
    g_blocks = gates.reshape(b, nqb, NSA_Q_BLOCK, G, J, N_BRANCH).swapaxes(0, 1)
    o = lax.map(one_block, (jnp.arange(nqb), q_blocks, g_blocks))
    o = o.swapaxes(0, 1).reshape(b, s_len, N_HEADS * Dh)
    return o @ w_out


def dsa_mixer(x, w_in, kv_norm_g, w_uk, w_uv, w_out, rel_bias):
    b, s_len, _ = x.shape
    q, c, qi, ki, wi = split_cols(x @ w_in, DSA_SIZES)
    q = q.reshape(b, s_len, N_HEADS, HEAD_DIM)
    c = rms_norm(c, kv_norm_g)
    qi = qi.reshape(b, s_len, IDX_HEADS, IDX_DIM) * (IDX_DIM ** -0.5)
    wi = wi * (IDX_HEADS ** -0.5)
    q_lat = jnp.einsum('bshd,hdc->bshc', q, w_uk) * (HEAD_DIM ** -0.5)
    k_sel = min(IDX_TOPK, s_len // 4)
    nqb = s_len // DSA_Q_BLOCK
    keys = jnp.arange(s_len)
    bI = jnp.arange(b)[:, None, None]

    def one_block(args):
        qb, qi_b, wi_b, ql_b = args
        t = qb * DSA_Q_BLOCK + jnp.arange(DSA_Q_BLOCK)
        dots = jnp.einsum('bqhe,bke->bqhk', qi_b, ki).astype(jnp.float32)
        score = jnp.einsum('bqh,bqhk->bqk', wi_b.astype(jnp.float32), jax.nn.relu(dots))
        score = jnp.where(keys[None, :] <= t[:, None], score, NEG_INF)
        _, idx = lax.top_k(score, k_sel)
        c_s = c[bI, idx]
        d = t[None, :, None] - idx
        bias = rel_bias[rel_bucket(d)].transpose(0, 3, 1, 2).astype(jnp.float32)
        s = jnp.einsum('bqhc,bqkc->bhqk', ql_b, c_s).astype(jnp.float32) + bias
        p = masked_softmax(s, (d >= 0)[:, None])
        o_lat = jnp.einsum('bhqk,bqkc->bqhc', p.astype(x.dtype), c_s)
        return jnp.einsum('bqhc,hcd->bqhd', o_lat, w_uv)

    blk = lambda t: t.reshape((b, nqb, DSA_Q_BLOCK) + t.shape[2:]).swapaxes(0, 1)
    o = lax.map(one_block, (jnp.arange(nqb), blk(qi), blk(wi), blk(q_lat)))
    o = o.swapaxes(0, 1).reshape(b, s_len, N_HEADS * HEAD_DIM)
    return o @ w_out


def swiglu(x, w_gate, w_up, w_down):
    return (jax.nn.silu(x @ w_gate) * (x @ w_up)) @ w_down


def moe_swiglu(x, w_router, b_router, w_gate, w_up, w_down):
    b, s_len, d = x.shape
    xt = x.reshape(-1, d)
    n_tok = xt.shape[0]
    logits = xt.astype(jnp.float32) @ w_router.astype(jnp.float32) + b_router.astype(jnp.float32)
    top_logit, top_e = lax.top_k(logits, MOE_TOP_K)
    top_w = jax.nn.softmax(top_logit, axis=-1)
    n_assign = n_tok * MOE_TOP_K
    e_flat = top_e.reshape(-1)
    order = jnp.argsort(e_flat)
    e_sorted = e_flat[order]
    tok_sorted = (order // MOE_TOP_K).astype(jnp.int32)
    w_sorted = top_w.reshape(-1)[order].astype(x.dtype)
    counts = jnp.bincount(e_flat, length=N_EXPERTS)
    starts = jnp.cumsum(counts) - counts
    padded = (counts + MOE_ROW_BLOCK - 1) // MOE_ROW_BLOCK * MOE_ROW_BLOCK
    pad_ends = jnp.cumsum(padded)
    pad_starts = pad_ends - padded
    dest = pad_starts[e_sorted] + jnp.arange(n_assign) - starts[e_sorted]
    n_blocks = -(-n_assign // MOE_ROW_BLOCK) + N_EXPERTS
    n_rows = n_blocks * MOE_ROW_BLOCK
    row_tok = jnp.zeros((n_rows,), jnp.int32).at[dest].set(tok_sorted)
    row_w = jnp.zeros((n_rows,), x.dtype).at[dest].set(w_sorted)
    block_e = jnp.minimum(jnp.searchsorted(pad_ends, jnp.arange(n_blocks) * MOE_ROW_BLOCK, side='right'),
                          N_EXPERTS - 1)

    def expert_block(args):
        tok, w, e = args
        xb = xt[tok]
        h = jax.nn.silu(xb @ w_gate[e]) * (xb @ w_up[e])
        return (h @ w_down[e]) * w[:, None]

    y_rows = lax.map(expert_block, (row_tok.reshape(n_blocks, MOE_ROW_BLOCK),
                                    row_w.reshape(n_blocks, MOE_ROW_BLOCK), block_e))
    y = jnp.zeros_like(xt).at[row_tok].add(y_rows.reshape(n_rows, d))
    return y.reshape(b, s_len, d)


def setup_inputs(seed: int = 0) -> dict:
    key = jax.random.key(seed)
    ks = iter(jax.random.split(key, 32))
    nrm = lambda shape, scale: jax.random.normal(next(ks), shape, jnp.float32) * scale
    D, Dh, E = D_MODEL, HEAD_DIM, N_EXPERTS
    return {
        "x": nrm((BATCH, SEQ, D), 1.0),
        "rel_bias": nrm((REL_BUCKETS, N_HEADS), 0.3),
        "nsa_w_in": nrm((D, sum(NSA_SIZES)), D ** -0.5),
        "nsa_cmp_pe_k": nrm((CMP_LEN, Dh), 0.1),
        "nsa_cmp_w1_k": nrm((CMP_LEN, Dh, Dh), (CMP_LEN * Dh) ** -0.5),
        "nsa_cmp_w2_k": nrm((Dh, Dh), Dh ** -0.5),
        "nsa_cmp_pe_v": nrm((CMP_LEN, Dh), 0.1),
        "nsa_cmp_w1_v": nrm((CMP_LEN, Dh, Dh), (CMP_LEN * Dh) ** -0.5),
        "nsa_cmp_w2_v": nrm((Dh, Dh), Dh ** -0.5),
        "nsa_w_out": nrm((N_HEADS * Dh, D), BETA * (N_HEADS * Dh) ** -0.5),
        "dsa_w_in": nrm((D, sum(DSA_SIZES)), D ** -0.5),
        "dsa_kv_norm_g": 1.0 + nrm((MLA_KV_RANK,), 0.02),
        "dsa_w_uk": nrm((N_HEADS, Dh, MLA_KV_RANK), Dh ** -0.5),
        "dsa_w_uv": nrm((N_HEADS, MLA_KV_RANK, Dh), MLA_KV_RANK ** -0.5),
        "dsa_w_out": nrm((N_HEADS * Dh, D), BETA * (N_HEADS * Dh) ** -0.5),
        "ffn_w_gate": nrm((D, D_FF), D ** -0.5),
        "ffn_w_up": nrm((D, D_FF), D ** -0.5),
        "ffn_w_down": nrm((D_FF, D), BETA * D_FF ** -0.5),
        "moe_w_router": nrm((D, E), D ** -0.5),
        "moe_b_router": nrm((E,), 0.01),
        "moe_w_gate": nrm((E, D, D_FF_EXPERT), D ** -0.5),
        "moe_w_up": nrm((E, D, D_FF_EXPERT), D ** -0.5),
        "moe_w_down": nrm((E, D_FF_EXPERT, D), BETA * D_FF_EXPERT ** -0.5),
        "ln_mix_g": 1.0 + nrm((DEPTH, D), 0.02),
        "ln_mix_b": nrm((DEPTH, D), 0.02),
        "ln_ffn_g": 1.0 + nrm((DEPTH, D), 0.02),
        "ln_ffn_b": nrm((DEPTH, D), 0.02),
    }


def reference(x, rel_bias,
              nsa_w_in, nsa_cmp_pe_k, nsa_cmp_w1_k, nsa_cmp_w2_k,
              nsa_cmp_pe_v, nsa_cmp_w1_v, nsa_cmp_w2_v, nsa_w_out,
              dsa_w_in, dsa_kv_norm_g, dsa_w_uk, dsa_w_uv, dsa_w_out,
              ffn_w_gate, ffn_w_up, ffn_w_down,
              moe_w_router, moe_b_router, moe_w_gate, moe_w_up, moe_w_down,
              ln_mix_g, ln_mix_b, ln_ffn_g, ln_ffn_b):
    for i in range(DEPTH):
        if i % N_MIXERS == 0:
            h = nsa_mixer(x, nsa_w_in, nsa_cmp_pe_k, nsa_cmp_w1_k, nsa_cmp_w2_k,
                          nsa_cmp_pe_v, nsa_cmp_w1_v, nsa_cmp_w2_v, nsa_w_out, rel_bias)
        else:
            h = dsa_mixer(x, dsa_w_in, dsa_kv_norm_g, dsa_w_uk, dsa_w_uv, dsa_w_out, rel_bias)
        x = layer_norm(ALPHA * x + h, ln_mix_g[i], ln_mix_b[i])
        if i % 2 == 0:
            f = swiglu(x, ffn_w_gate, ffn_w_up, ffn_w_down)
        else:
            f = moe_swiglu(x, moe_w_router, moe_b_router, moe_w_gate, moe_w_up, moe_w_down)
        x = layer_norm(ALPHA * x + f, ln_ffn_g[i], ln_ffn_b[i])
    return x
```

```python
import functools
import math

import numpy as np
import jax
import jax.numpy as jnp
from jax import lax
from jax.experimental import pallas as pl
from jax.experimental.pallas import tpu as pltpu

F32 = jnp.float32
BF = jnp.bfloat16
I32 = jnp.int32

N_HEADS = 16
HEAD_DIM = 128
KV_GROUPS = 4
HPG = N_HEADS // KV_GROUPS
CMP_LEN = 32
CMP_STRIDE = 16
SEL_LEN = 64
SEL_TOP = 16
WINDOW = 512
SEL_FORCE = 1e4
KV_RANK = 256
IDX_HEADS = 8
IDX_DIM = 64
IDX_TOPK = 256
REL_BUCKETS = 32
REL_MAX_DIST = 128
N_EXPERTS = 8
DEPTH = 2
ALPHA = (2 * DEPTH) ** 0.25
LN_EPS = 1e-5
RMS_EPS = 1e-6
NEG_INF = -1e30

BIG = float(2.0 ** 100)
M_INIT = -3.0e38
TQ = 256
LANES = 128
VMEM_LIMIT_BYTES = 60000 * 1024


def _cp(sem, vmem=None):
    return pltpu.CompilerParams(dimension_semantics=sem, vmem_limit_bytes=vmem or VMEM_LIMIT_BYTES)


def _dot(a, b):
    return jnp.dot(a.astype(BF), b.astype(BF), preferred_element_type=F32)


def _dot_nt(a, b):
    return lax.dot_general(a.astype(BF), b.astype(BF), (((1,), (1,)), ((), ())),
                           preferred_element_type=F32)


def _layer_norm_rows(z, g, b):
    mu = jnp.mean(z, axis=-1, keepdims=True)
    zc = z - mu
    var = jnp.mean(zc * zc, axis=-1, keepdims=True)
    return zc * lax.rsqrt(var + LN_EPS) * g + b


def _mm_scale_kernel(a_ref, b_ref, s_ref, o_ref):
    o_ref[...] = (_dot(a_ref[...], b_ref[...]) * s_ref[...]).astype(o_ref.dtype)


def matmul_scaled(a, b, scale, out_dtype, tm, tn):
    m, k = a.shape
    n = b.shape[1]
    return pl.pallas_call(
        _mm_scale_kernel,
        grid=(m // tm, n // tn),
        in_specs=[pl.BlockSpec((tm, k), lambda i, j: (i, 0)),
                  pl.BlockSpec((k, tn), lambda i, j: (0, j)),
                  pl.BlockSpec((1, tn), lambda i, j: (0, j))],
        out_specs=pl.BlockSpec((tm, tn), lambda i, j: (i, j)),
        out_shape=jax.ShapeDtypeStruct((m, n), out_dtype),
        compiler_params=_cp(("parallel", "parallel")),
        name="matmul_scaled",
    )(a, b, scale.reshape(1, n).astype(F32))


def _mm_rms_kernel(a_ref, b_ref, g_ref, o_ref):
    c = _dot(a_ref[...], b_ref[...])
    r = lax.rsqrt(jnp.mean(c * c, axis=-1, keepdims=True) + RMS_EPS)
    o_ref[...] = (c * r * g_ref[...]).astype(o_ref.dtype)


def matmul_rmsnorm(a, b, g, tm):
    m, k = a.shape
    n = b.shape[1]
    return pl.pallas_call(
        _mm_rms_kernel,
        grid=(m // tm,),
        in_specs=[pl.BlockSpec((tm, k), lambda i: (i, 0)),
                  pl.BlockSpec((k, n), lambda i: (0, 0)),
                  pl.BlockSpec((1, n), lambda i: (0, 0))],
        out_specs=pl.BlockSpec((tm, n), lambda i: (i, 0)),
        out_shape=jax.ShapeDtypeStruct((m, n), BF),
        compiler_params=_cp(("parallel",)),
        name="matmul_rmsnorm",
    )(a, b, g.reshape(1, n).astype(F32))


def _mm_ln_kernel(a_ref, b_ref, r_ref, g_ref, be_ref, of_ref, ob_ref, acc_ref, *, nk):
    kk = pl.program_id(1)

    @pl.when(kk == 0)
    def _():
        acc_ref[...] = jnp.zeros_like(acc_ref)

    acc_ref[...] += _dot(a_ref[...], b_ref[...])

    @pl.when(kk == nk - 1)
    def _():
        y = _layer_norm_rows(ALPHA * r_ref[...] + acc_ref[...], g_ref[...], be_ref[...])
        of_ref[...] = y
        ob_ref[...] = y.astype(BF)


def matmul_residual_ln(a, b, res, g, beta, tm, tk):
    m, k = a.shape
    n = b.shape[1]
    nk = k // tk
    return pl.pallas_call(
        functools.partial(_mm_ln_kernel, nk=nk),
        grid=(m // tm, nk),
        in_specs=[pl.BlockSpec((tm, tk), lambda i, kk: (i, kk)),
                  pl.BlockSpec((tk, n), lambda i, kk: (kk, 0)),
                  pl.BlockSpec((tm, n), lambda i, kk: (i, 0)),
                  pl.BlockSpec((1, n), lambda i, kk: (0, 0)),
                  pl.BlockSpec((1, n), lambda i, kk: (0, 0))],
        out_specs=[pl.BlockSpec((tm, n), lambda i, kk: (i, 0)),
                   pl.BlockSpec((tm, n), lambda i, kk: (i, 0))],
        out_shape=[jax.ShapeDtypeStruct((m, n), F32), jax.ShapeDtypeStruct((m, n), BF)],
        scratch_shapes=[pltpu.VMEM((tm, n), F32)],
        compiler_params=_cp(("parallel", "arbitrary")),
        name="matmul_residual_ln",
    )(a, b, res, g.reshape(1, n).astype(F32), beta.reshape(1, n).astype(F32))


def _silu(x):
    return x * (1.0 / (1.0 + jnp.exp(-x)))


def _swiglu_up_kernel(a_ref, wg_ref, wu_ref, o_ref):
    a = a_ref[...]
    o_ref[...] = (_silu(_dot(a, wg_ref[...])) * _dot(a, wu_ref[...])).astype(o_ref.dtype)


def swiglu_up(a, wg, wu, tm, tn):
    m, k = a.shape
    n = wg.shape[1]
    return pl.pallas_call(
        _swiglu_up_kernel,
        grid=(m // tm, n // tn),
        in_specs=[pl.BlockSpec((tm, k), lambda i, j: (i, 0)),
                  pl.BlockSpec((k, tn), lambda i, j: (0, j)),
                  pl.BlockSpec((k, tn), lambda i, j: (0, j))],
        out_specs=pl.BlockSpec((tm, tn), lambda i, j: (i, j)),
        out_shape=jax.ShapeDtypeStruct((m, n), BF),
        compiler_params=_cp(("parallel", "parallel")),
        name="swiglu_up",
    )(a, wg, wu)


def _bucket_of_distance():
    n = np.arange(REL_MAX_DIST + 1)
    exact = REL_BUCKETS // 2
    nf = np.maximum(n, exact).astype(np.float64)
    large = exact + (np.log(nf / exact) / math.log(REL_MAX_DIST / exact) * (REL_BUCKETS - exact)).astype(np.int64)
    return np.where(n < exact, n, np.minimum(large, REL_BUCKETS - 1)).astype(np.int32)


def _bias_tables(rel_bias, n_cmp_pad):
    tab = rel_bias[jnp.asarray(_bucket_of_distance())].astype(F32).T
    far = tab[:, REL_MAX_DIST][:, None]
    tabc = tab - far
    i = np.arange(TQ)[:, None]
    j = np.arange(TQ)[None, :]
    d = np.stack([TQ * r + i - j for r in range(3)])
    vals = tabc[:, np.clip(d, 0, REL_MAX_DIST)]
    vals = jnp.where(jnp.asarray((d < 0) | (d >= WINDOW)), -BIG, vals)
    band = vals.reshape(4, 4, 3, TQ, TQ).transpose(0, 2, 1, 3, 4).reshape(4, 3, 4 * TQ, TQ)
    c = np.arange(32)[None, :]
    dc = i - CMP_STRIDE * (c - 16) - (CMP_LEN - 1)
    cv = jnp.where(jnp.asarray(dc < 0), -BIG, tabc[:, np.clip(dc, 0, REL_MAX_DIST)])
    cv = jnp.pad(cv, ((0, 0), (0, 0), (0, n_cmp_pad - 32)))
    cmpb = cv.reshape(4, 4 * TQ, n_cmp_pad)
    return band, cmpb


def _heads_to_rows(q):
    return jnp.concatenate([q[:, j * HEAD_DIM:(j + 1) * HEAD_DIM] for j in range(HPG)], axis=0)


def _osm_update(carry, s, v):
    m, l, acc = carry
    m_new = jnp.maximum(m, jnp.max(s, axis=-1, keepdims=True))
    alpha = jnp.exp(m - m_new)
    p = jnp.exp(s - m_new)
    l = alpha * l + jnp.sum(p, axis=-1, keepdims=True)
    acc = alpha * acc + jnp.dot(p.astype(BF), v, preferred_element_type=F32)
    return m_new, l, acc


def _osm_init(rows, dv):
    return (jnp.full((rows, 1), M_INIT, F32), jnp.zeros((rows, 1), F32), jnp.zeros((rows, dv), F32))


def _gelu_tanh(x):
    return 0.5 * x * (1.0 + jnp.tanh(math.sqrt(2.0 / math.pi) * (x + 0.044715 * (x * x * x))))


def _compress_kernel(*refs, nch):
    x_refs = refs[:CMP_STRIDE]
    pe_ref, w1_ref, w2_ref, o_ref = refs[CMP_STRIDE:]
    acc_a = [jnp.zeros((nch, HEAD_DIM), F32) for _ in range(KV_GROUPS)]
    acc_b = [jnp.zeros((nch, HEAD_DIM), F32) for _ in range(KV_GROUPS)]
    for l in range(CMP_STRIDE):
        x = x_refs[l][...].astype(F32)
        xa = (x + pe_ref[l:l + 1, :]).astype(BF)
        xb = (x + pe_ref[CMP_STRIDE + l:CMP_STRIDE + l + 1, :]).astype(BF)
        for g in range(KV_GROUPS):
            sl = slice(g * HEAD_DIM, (g + 1) * HEAD_DIM)
            acc_a[g] = acc_a[g] + jnp.dot(xa[:, sl], w1_ref[l], preferred_element_type=F32)
            acc_b[g] = acc_b[g] + jnp.dot(xb[:, sl], w1_ref[CMP_STRIDE + l], preferred_element_type=F32)
    for g in range(KV_GROUPS):
        pre = acc_a[g] + pltpu.roll(acc_b[g], nch - 1, axis=0)
        hid = _gelu_tanh(pre).astype(BF)
        o_ref[:, g * HEAD_DIM:(g + 1) * HEAD_DIM] = jnp.dot(hid, w2_ref[...], preferred_element_type=F32).astype(BF)


def nsa_compress(ybf, b, s, col_block, pe, w1, w2):
    ncols = ybf.shape[1]
    nch = s // CMP_STRIDE
    blk_w = KV_GROUPS * HEAD_DIM
    per_tok = ncols // blk_w
    y3 = ybf.reshape(b, nch, CMP_STRIDE * ncols)
    pe_t = jnp.tile(pe.astype(F32), (1, KV_GROUPS))
    in_specs = [pl.BlockSpec((None, nch, blk_w), (lambda bi, l=l: (bi, 0, l * per_tok + col_block)))
                for l in range(CMP_STRIDE)]
    in_specs += [pl.BlockSpec((CMP_LEN, blk_w), lambda bi: (0, 0)),
                 pl.BlockSpec((CMP_LEN, HEAD_DIM, HEAD_DIM), lambda bi: (0, 0, 0)),
                 pl.BlockSpec((HEAD_DIM, HEAD_DIM), lambda bi: (0, 0))]
    return pl.pallas_call(
        functools.partial(_compress_kernel, nch=nch),
        grid=(b,),
        in_specs=in_specs,
        out_specs=pl.BlockSpec((None, nch, blk_w), lambda bi: (bi, 0, 0)),
        out_shape=jax.ShapeDtypeStruct((b, nch, blk_w), BF),
        compiler_params=_cp(("parallel",)),
        name="nsa_compress",
    )(*([y3] * CMP_STRIDE), pe_t, w1.astype(BF), w2.astype(BF))


def _nsa_cmp_kernel(q_ref, kc_ref, vc_ref, cb_ref, ov_ref, oc_ref, ns_ref, *, n_sel, n_top):
    qt = pl.program_id(2)
    t0 = qt * TQ
    ncp = kc_ref.shape[0]
    qs = _heads_to_rows(q_ref[...])
    s = _dot_nt(qs, kc_ref[...])
    shift = lax.rem(qt * (TQ // CMP_STRIDE) + (ncp - 16), ncp)
    s = s + pltpu.roll(cb_ref[...], shift, axis=1)
    row = lax.broadcasted_iota(I32, (HPG * TQ, ncp), 0) & (TQ - 1)
    col = lax.broadcasted_iota(I32, (HPG * TQ, ncp), 1)
    vis = (col * CMP_STRIDE + (CMP_LEN - 1)) <= (t0 + row)
    s = jnp.where(vis, s, -BIG)
    m = jnp.max(s, axis=-1, keepdims=True)
    p = jnp.where(vis, jnp.exp(s - m), 0.0)
    l = jnp.sum(p, axis=-1, keepdims=True)
    p = p * jnp.where(l > 0.0, 1.0 / l, 0.0)
    pb = p.astype(BF)
    oc = jnp.dot(pb, vc_ref[...], preferred_element_type=F32)
    score = jnp.zeros((TQ, LANES), F32)
    for j in range(HPG):
        oc_ref[:, j * HEAD_DIM:(j + 1) * HEAD_DIM] = oc[j * TQ:(j + 1) * TQ].astype(oc_ref.dtype)
        score = score + jnp.dot(pb[j * TQ:(j + 1) * TQ], ov_ref[...], preferred_element_type=F32)
    blk = lax.broadcasted_iota(I32, (TQ, LANES), 1)
    t = t0 + lax.broadcasted_iota(I32, (TQ, LANES), 0)
    cur = lax.shift_right_logical(t, int(math.log2(SEL_LEN)))
    forced = (blk == 0) | (blk == cur) | (blk == cur - 1)
    visible = blk * SEL_LEN <= t
    sc = jnp.where(forced, SEL_FORCE, jnp.where(visible, score, -1.0))
    sc = jnp.where(blk < n_sel, sc, -2.0)
    blkf = blk.astype(F32)

    def pick_one(_, carry):
        sc, sel = carry
        mx = jnp.max(sc, axis=-1, keepdims=True)
        first = jnp.min(jnp.where(sc == mx, blkf, float(LANES)), axis=-1, keepdims=True)
        hit = blkf == first
        return jnp.where(hit, -3.0, sc), jnp.where(hit, 1.0, sel)

    _, sel = lax.fori_loop(0, n_top, pick_one, (sc, jnp.zeros((TQ, LANES), F32)))
    ns_ref[...] = jnp.where(sel > 0.5, 0.0, BIG).astype(BF)


def nsa_cmp_select(ybf, kcmp, vcmp, cmpb, overlap, b, s):
    ncp = kcmp.shape[1]
    n_sel = s // SEL_LEN
    n_top = min(SEL_TOP, n_sel)
    y3 = ybf.reshape(b, s, ybf.shape[1])
    gw = HPG * HEAD_DIM
    return pl.pallas_call(
        functools.partial(_nsa_cmp_kernel, n_sel=n_sel, n_top=n_top),
        grid=(b, KV_GROUPS, s // TQ),
        in_specs=[pl.BlockSpec((None, TQ, gw), lambda bi, g, qt: (bi, qt, g)),
                  pl.BlockSpec((None, ncp, HEAD_DIM), lambda bi, g, qt: (bi, 0, g)),
                  pl.BlockSpec((None, ncp, HEAD_DIM), lambda bi, g, qt: (bi, 0, g)),
                  pl.BlockSpec((None, HPG * TQ, ncp), lambda bi, g, qt: (g, 0, 0)),
                  pl.BlockSpec((ncp, LANES), lambda bi, g, qt: (0, 0))],
        out_specs=[pl.BlockSpec((None, TQ, gw), lambda bi, g, qt: (bi, qt, g)),
                   pl.BlockSpec((None, None, TQ, LANES), lambda bi, g, qt: (bi, g, qt, 0))],
        out_shape=[jax.ShapeDtypeStruct((b, s, KV_GROUPS * gw), BF),
                   jax.ShapeDtypeStruct((b, KV_GROUPS, s, LANES), BF)],
        compiler_params=_cp(("parallel", "parallel", "parallel")),
        name="nsa_cmp_select",
    )(y3, kcmp, vcmp, cmpb, overlap)


def _nsa_main_kernel(q_ref, ks_ref, vs_ref, kw_ref, vw_ref, ns_ref, band_ref, gl_ref, oc_ref, o_ref):
    qt = pl.program_id(2)
    rows = HPG * TQ
    qs = _heads_to_rows(q_ref[...])
    ns4 = jnp.concatenate([ns_ref[...]] * HPG, axis=0)
    qp = jnp.concatenate([qs, ns4], axis=1)
    krow = lax.broadcasted_iota(I32, (TQ, LANES), 0)
    klane = lax.broadcasted_iota(I32, (TQ, LANES), 1)
    kblk = lax.shift_right_logical(krow, int(math.log2(SEL_LEN)))

    def sel_chunk(kc, carry, bias):
        start = pl.multiple_of(kc * TQ, TQ)
        k = ks_ref[pl.ds(start, TQ), :]
        v = vs_ref[pl.ds(start, TQ), :]
        oh = jnp.where(klane == kblk + kc * (TQ // SEL_LEN), -1.0, 0.0).astype(BF)
        s = _dot_nt(qp, jnp.concatenate([k, oh], axis=1))
        if bias is not None:
            s = s + bias
        return _osm_update(carry, s, v)

    carry = lax.fori_loop(0, jnp.maximum(qt - 1, 0), lambda kc, c: sel_chunk(kc, c, None),
                          _osm_init(rows, HEAD_DIM))
    carry = sel_chunk(jnp.maximum(qt - 1, 0), carry, jnp.where(qt >= 1, band_ref[1], -BIG))
    m_s, l_s, acc_s = sel_chunk(qt, carry, band_ref[0])

    def win_chunk(r, carry):
        kc = jnp.maximum(qt - r, 0)
        start = pl.multiple_of(kc * TQ, TQ)
        s = _dot_nt(qs, kw_ref[pl.ds(start, TQ), :]) + jnp.where(qt >= r, band_ref[r], -BIG)
        return _osm_update(carry, s, vw_ref[pl.ds(start, TQ), :])

    carry = _osm_init(rows, HEAD_DIM)
    for r in (2, 1, 0):
        carry = win_chunk(r, carry)
    m_w, l_w, acc_w = carry

    o_s = acc_s * (1.0 / l_s)
    o_w = acc_w * (1.0 / l_w)
    gates = 1.0 / (1.0 + jnp.exp(-gl_ref[...]))
    oc = oc_ref[...].astype(F32)
    for j in range(HPG):
        sl = slice(j * HEAD_DIM, (j + 1) * HEAD_DIM)
        rs = slice(j * TQ, (j + 1) * TQ)
        o = (gates[:, j:j + 1] * oc[:, sl] + gates[:, HPG + j:HPG + j + 1] * o_s[rs]
             + gates[:, 2 * HPG + j:2 * HPG + j + 1] * o_w[rs])
        o_ref[:, sl] = o.astype(o_ref.dtype)


def nsa_main(ybf, nsel, band, gl, oc, b, s, col_ksl, col_vsl, col_kw, col_vw):
    y3 = ybf.reshape(b, s, ybf.shape[1])
    gw = HPG * HEAD_DIM
    kv_spec = lambda cb: pl.BlockSpec((None, s, HEAD_DIM), lambda bi, g, qt: (bi, 0, cb + g))
    return pl.pallas_call(
        _nsa_main_kernel,
        grid=(b, KV_GROUPS, s // TQ),
        in_specs=[pl.BlockSpec((None, TQ, gw), lambda bi, g, qt: (bi, qt, g)),
                  kv_spec(col_ksl), kv_spec(col_vsl), kv_spec(col_kw), kv_spec(col_vw),
                  pl.BlockSpec((None, None, TQ, LANES), lambda bi, g, qt: (bi, g, qt, 0)),
                  pl.BlockSpec((None, 3, HPG * TQ, TQ), lambda bi, g, qt: (g, 0, 0, 0)),
                  pl.BlockSpec((None, TQ, LANES), lambda bi, g, qt: (bi, qt, g)),
                  pl.BlockSpec((None, TQ, gw), lambda bi, g, qt: (bi, qt, g))],
        out_specs=pl.BlockSpec((None, TQ, gw), lambda bi, g, qt: (bi, qt, g)),
        out_shape=jax.ShapeDtypeStruct((b, s, KV_GROUPS * gw), BF),
        compiler_params=_cp(("parallel", "parallel", "parallel")),
        name="nsa_main",
    )(y3, y3, y3, y3, y3, nsel, band, gl.reshape(b, s, gl.shape[1]), oc)


def _dsa_index_kernel(qi_ref, kw_ref, wq_ref, nm_ref, ka_ref, kb_ref, key_ref, *, k_sel, nchunk, idx_bits):
    qt = pl.program_id(1)
    t0 = qt * TQ
    half = LANES // 2

    @pl.when(qt == 0)
    def _():
        lane = lax.broadcasted_iota(I32, kw_ref.shape, 1)
        ka = jnp.where(lane < IDX_DIM, kw_ref[...], 0.0)
        ka_ref[...] = ka.astype(BF)
        kb_ref[...] = pltpu.roll(ka, half, axis=1).astype(BF)

    qi = qi_ref[...]
    npair = IDX_HEADS // 2
    lq = jnp.concatenate([qi[:, p * LANES:(p + 1) * LANES] for p in range(npair)], axis=0)
    wq = wq_ref[...]
    w_b = [jnp.broadcast_to(wq[:, IDX_DIM + h:IDX_DIM + h + 1], (TQ, TQ)) for h in range(IDX_HEADS)]
    row = lax.broadcasted_iota(I32, (TQ, TQ), 0)
    lane = lax.broadcasted_iota(I32, (TQ, TQ), 1)

    def score_chunk(kc, _):
        start = pl.multiple_of(kc * TQ, TQ)
        da = _dot_nt(lq, ka_ref[pl.ds(start, TQ), :])
        db = _dot_nt(lq, kb_ref[pl.ds(start, TQ), :])
        sc = jnp.zeros((TQ, TQ), F32)
        for p in range(npair):
            rs = slice(p * TQ, (p + 1) * TQ)
            sc = sc + w_b[2 * p] * jnp.maximum(da[rs], 0.0) + w_b[2 * p + 1] * jnp.maximum(db[rs], 0.0)
        sc = jnp.where(sc == 0.0, 0.0, sc)
        sc = jnp.where(kc * TQ + lane <= t0 + row, sc, NEG_INF)
        bits = lax.bitcast_convert_type(sc, I32)
        key_ref[kc] = jnp.where(bits < 0, bits ^ 0x7FFFFFFF, bits)
        return 0

    nproc = qt + 1
    lax.fori_loop(0, nproc, score_chunk, 0)

    def count(pred):
        def body(kc, acc):
            return acc + jnp.where(pred(key_ref[kc], kc), 1.0, 0.0)
        return jnp.sum(lax.fori_loop(0, nproc, body, jnp.zeros((TQ, TQ), F32)), axis=-1, keepdims=True)

    def count_ge(cand):
        cb = jnp.broadcast_to(cand, (TQ, TQ))
        return count(lambda k, kc: k >= cb)

    int_min = jnp.int32(-2 ** 31)
    kf = float(k_sel)
    thr = jnp.where(count_ge(jnp.zeros((TQ, 1), I32)) >= kf, 0, int_min).astype(I32)

    def thr_bit(i, thr):
        cand = thr | lax.shift_left(jnp.int32(1), jnp.int32(30) - i)
        return jnp.where(count_ge(cand) >= kf, cand, thr)

    thr = lax.fori_loop(0, 31, thr_bit, thr)
    need = kf - count_ge(thr + 1)
    thr_b = jnp.broadcast_to(thr, (TQ, TQ))

    def ties_below(bound):
        bb = jnp.broadcast_to(bound, (TQ, TQ))
        return count(lambda k, kc: (k == thr_b) & (kc * TQ + lane < bb))

    def idx_bit(i, jm):
        cand = jm | lax.shift_left(jnp.int32(1), jnp.int32(idx_bits - 1) - i)
        return jnp.where(ties_below(cand) < need, cand, jm)

    jm = lax.fori_loop(0, idx_bits, idx_bit, jnp.zeros((TQ, 1), I32))
    jm_b = jnp.broadcast_to(jm, (TQ, TQ))

    def write_chunk(kc, _):
        k = key_ref[kc]
        kidx = kc * TQ + lane
        sel = (kidx <= t0 + row) & ((k > thr_b) | ((k == thr_b) & (kidx <= jm_b)))
        nm_ref[kc] = jnp.where(sel, 0.0, -BIG).astype(BF)
        return 0

    lax.fori_loop(0, nproc, write_chunk, 0)

    def fill_chunk(kc, _):
        nm_ref[kc] = jnp.full((TQ, TQ), -BIG, BF)
        return 0

    lax.fori_loop(nproc, nchunk, fill_chunk, 0)


def dsa_index(qbf, kw, b, s, col_qi):
    nchunk = s // TQ
    k_sel = min(IDX_TOPK, s // 4)
    q3 = qbf.reshape(b, s, qbf.shape[1])
    kw3 = kw.reshape(b, s, LANES)
    return pl.pallas_call(
        functools.partial(_dsa_index_kernel, k_sel=k_sel, nchunk=nchunk, idx_bits=int(math.log2(s))),
        grid=(b, nchunk),
        in_specs=[pl.BlockSpec((None, TQ, IDX_HEADS * IDX_DIM), lambda bi, qt: (bi, qt, col_qi)),
                  pl.BlockSpec((None, s, LANES), lambda bi, qt: (bi, 0, 0)),
                  pl.BlockSpec((None, TQ, LANES), lambda bi, qt: (bi, qt, 0))],
        out_specs=pl.BlockSpec((None, None, nchunk, TQ, TQ), lambda bi, qt: (bi, qt, 0, 0, 0)),
        out_shape=jax.ShapeDtypeStruct((b, nchunk, nchunk, TQ, TQ), BF),
        scratch_shapes=[pltpu.VMEM((s, LANES), BF), pltpu.VMEM((s, LANES), BF),
                        pltpu.VMEM((nchunk, TQ, TQ), I32)],
        compiler_params=_cp(("parallel", "arbitrary")),
        name="dsa_index",
    )(q3, kw3, kw3)


def _dsa_attn_kernel(q_ref, c_ref, nm_ref, band_ref, wuk_ref, wuv_ref, o_ref):
    qt = pl.program_id(1)
    rows = HPG * TQ
    q = q_ref[...]
    ql = jnp.concatenate(
        [(jnp.dot(q[:, j * HEAD_DIM:(j + 1) * HEAD_DIM], wuk_ref[j], preferred_element_type=F32)
          * (HEAD_DIM ** -0.5)).astype(BF) for j in range(HPG)], axis=0)

    def chunk(kc, carry, bias):
        start = pl.multiple_of(kc * TQ, TQ)
        c = c_ref[pl.ds(start, TQ), :]
        nm = nm_ref[kc].astype(F32)
        s = _dot_nt(ql, c) + jnp.concatenate([nm] * HPG, axis=0)
        if bias is not None:
            s = s + bias
        return _osm_update(carry, s, c)

    carry = lax.fori_loop(0, jnp.maximum(qt - 1, 0), lambda kc, cr: chunk(kc, cr, None),
                          _osm_init(rows, KV_RANK))
    carry = chunk(jnp.maximum(qt - 1, 0), carry, jnp.where(qt >= 1, band_ref[1], -BIG))
    m, l, acc = chunk(qt, carry, band_ref[0])
    o_lat = (acc * (1.0 / l)).astype(BF)
    for j in range(HPG):
        o_ref[:, j * HEAD_DIM:(j + 1) * HEAD_DIM] = jnp.dot(
            o_lat[j * TQ:(j + 1) * TQ], wuv_ref[j], preferred_element_type=F32).astype(o_ref.dtype)


def dsa_attention(qbf, cn, nmask, band, w_uk, w_uv, b, s):
    nchunk = s // TQ
    q3 = qbf.reshape(b, s, qbf.shape[1])
    c3 = cn.reshape(b, s, KV_RANK)
    gw = HPG * HEAD_DIM
    return pl.pallas_call(
        _dsa_attn_kernel,
        grid=(b, nchunk, N_HEADS // HPG),
        in_specs=[pl.BlockSpec((None, TQ, gw), lambda bi, qt, hg: (bi, qt, hg)),
                  pl.BlockSpec((None, s, KV_RANK), lambda bi, qt, hg: (bi, 0, 0)),
                  pl.BlockSpec((None, None, nchunk, TQ, TQ), lambda bi, qt, hg: (bi, qt, 0, 0, 0)),
                  pl.BlockSpec((None, 2, HPG * TQ, TQ), lambda bi, qt, hg: (hg, 0, 0, 0)),
                  pl.BlockSpec((HPG, HEAD_DIM, KV_RANK), lambda bi, qt, hg: (hg, 0, 0)),
                  pl.BlockSpec((HPG, KV_RANK, HEAD_DIM), lambda bi, qt, hg: (hg, 0, 0))],
        out_specs=pl.BlockSpec((None, TQ, gw), lambda bi, qt, hg: (bi, qt, hg)),
        out_shape=jax.ShapeDtypeStruct((b, s, N_HEADS * HEAD_DIM), BF),
        compiler_params=_cp(("parallel", "parallel", "arbitrary")),
        name="dsa_attention",
    )(q3, c3, nmask, band, w_uk.astype(BF), w_uv.astype(BF))


ROUTER_TM = 256
MOE_TM = 256


def _router_kernel(x_ref, w_ref, b_ref, o_ref, cnt_ref, carry_ref):
    i = pl.program_id(0)

    @pl.when(i == 0)
    def _():
        carry_ref[...] = jnp.zeros_like(carry_ref)

    tm = x_ref.shape[0]
    logits = jnp.dot(x_ref[...], w_ref[...], preferred_element_type=F32,
                     precision=lax.Precision.HIGHEST) + b_ref[...]
    lane = lax.broadcasted_iota(I32, (tm, LANES), 1)
    lanef = lane.astype(F32)
    lg = jnp.where(lane < N_EXPERTS, logits, -BIG)
    m1 = jnp.max(lg, axis=-1, keepdims=True)
    i1 = jnp.min(jnp.where(lg == m1, lanef, float(LANES)), axis=-1, keepdims=True)
    lg2 = jnp.where(lanef == i1, -BIG, lg)
    m2 = jnp.max(lg2, axis=-1, keepdims=True)
    i2 = jnp.min(jnp.where(lg2 == m2, lanef, float(LANES)), axis=-1, keepdims=True)
    e2 = jnp.exp(m2 - m1)
    den = 1.0 + e2
    w1 = 1.0 / den
    w2 = e2 / den
    hit1 = lanef == i1
    hit2 = lanef == i2
    onehot = jnp.where(hit1 | hit2, 1.0, 0.0)
    r = lax.broadcasted_iota(I32, (tm, tm), 0)
    c = lax.broadcasted_iota(I32, (tm, tm), 1)
    tri = jnp.where(c < r, 1.0, 0.0).astype(BF)
    before = jnp.dot(tri, onehot.astype(BF), preferred_element_type=F32) + carry_ref[...]
    rank1 = jnp.sum(jnp.where(hit1, before, 0.0), axis=-1, keepdims=True)
    rank2 = jnp.sum(jnp.where(hit2, before, 0.0), axis=-1, keepdims=True)
    carry_ref[...] = carry_ref[...] + jnp.sum(onehot, axis=0, keepdims=True)
    vals = (i1, i2, w1, w2, rank1, rank2)
    out = jnp.zeros((tm, LANES), F32)
    for k, v in enumerate(vals):
        out = jnp.where(lane == k, v, out)
    o_ref[...] = out
    cnt_ref[...] = jnp.broadcast_to(carry_ref[...], cnt_ref.shape)


def moe_router(x, w_router, b_router):
    n, d = x.shape
    wp = jnp.pad(w_router.astype(F32), ((0, 0), (0, LANES - N_EXPERTS)))
    bp = jnp.pad(b_router.astype(F32), (0, LANES - N_EXPERTS)).reshape(1, LANES)
    return pl.pallas_call(
        _router_kernel,
        grid=(n // ROUTER_TM,),
        in_specs=[pl.BlockSpec((ROUTER_TM, d), lambda i: (i, 0)),
                  pl.BlockSpec((d, LANES), lambda i: (0, 0)),
                  pl.BlockSpec((1, LANES), lambda i: (0, 0))],
        out_specs=[pl.BlockSpec((ROUTER_TM, LANES), lambda i: (i, 0)),
                   pl.BlockSpec((8, LANES), lambda i: (0, 0))],
        out_shape=[jax.ShapeDtypeStruct((n, LANES), F32), jax.ShapeDtypeStruct((8, LANES), F32)],
        scratch_shapes=[pltpu.VMEM((1, LANES), F32)],
        compiler_params=_cp(("arbitrary",)),
        name="moe_router",
    )(x, wp, bp)


GATHER_ROWS = 256


def _row_slabs(x):
    return x.reshape(x.shape[0], x.shape[1] // LANES, LANES)


def _row_gather_kernel(idx_ref, x_hbm, o_hbm, sem):
    i = pl.program_id(0)

    def start(r, _):
        pltpu.make_async_copy(x_hbm.at[idx_ref[0, 0, r]], o_hbm.at[i * GATHER_ROWS + r], sem).start()
        return 0

    lax.fori_loop(0, GATHER_ROWS, start, 0)
    pltpu.make_async_copy(x_hbm.at[pl.ds(0, GATHER_ROWS)],
                          o_hbm.at[pl.ds(i * GATHER_ROWS, GATHER_ROWS)], sem).wait()


def row_gather(x, row_idx):
    n_rows = row_idx.shape[0]
    nt = n_rows // GATHER_ROWS
    x3 = _row_slabs(x)
    out = pl.pallas_call(
        _row_gather_kernel,
        grid=(nt,),
        in_specs=[pl.BlockSpec((1, 1, GATHER_ROWS), lambda i: (i, 0, 0), memory_space=pltpu.SMEM),
                  pl.BlockSpec(memory_space=pl.ANY)],
        out_specs=pl.BlockSpec(memory_space=pl.ANY),
        out_shape=jax.ShapeDtypeStruct((n_rows,) + x3.shape[1:], x.dtype),
        scratch_shapes=[pltpu.SemaphoreType.DMA],
        compiler_params=_cp(("arbitrary",)),
        name="moe_row_gather",
    )(row_idx.reshape(nt, 1, GATHER_ROWS), x3)
    return out.reshape(n_rows, x.shape[1])


def _moe_up_kernel(te_ref, tv_ref, x_ref, wg_ref, wu_ref, o_ref):
    i = pl.program_id(1)

    @pl.when(tv_ref[i] > 0)
    def _():
        x = x_ref[...]
        o_ref[...] = (_silu(_dot(x, wg_ref[...])) * _dot(x, wu_ref[...])).astype(o_ref.dtype)

    @pl.when(tv_ref[i] == 0)
    def _():
        o_ref[...] = jnp.zeros_like(o_ref)


def moe_up(xs, w_gate, w_up, tile_e, tile_v, tn):
    n_rows, d = xs.shape
    f = w_gate.shape[2]
    nt = n_rows // MOE_TM
    grid_spec = pltpu.PrefetchScalarGridSpec(
        num_scalar_prefetch=2,
        grid=(f // tn, nt),
        in_specs=[pl.BlockSpec((MOE_TM, d), lambda j, i, te, tv: (i, 0)),
                  pl.BlockSpec((None, d, tn), lambda j, i, te, tv: (te[i], 0, j)),
                  pl.BlockSpec((None, d, tn), lambda j, i, te, tv: (te[i], 0, j))],
        out_specs=pl.BlockSpec((MOE_TM, tn), lambda j, i, te, tv: (i, j)),
    )
    return pl.pallas_call(
        _moe_up_kernel,
        grid_spec=grid_spec,
        out_shape=jax.ShapeDtypeStruct((n_rows, f), BF),
        compiler_params=_cp(("parallel", "arbitrary")),
        name="moe_up",
    )(tile_e, tile_v, xs, w_gate, w_up)


def _moe_down_kernel(te_ref, tv_ref, h_ref, wd_ref, o_ref):
    i = pl.program_id(1)

    @pl.when(tv_ref[i] > 0)
    def _():
        o_ref[...] = _dot(h_ref[...], wd_ref[...])

    @pl.when(tv_ref[i] == 0)
    def _():
        o_ref[...] = jnp.zeros_like(o_ref)


def moe_down(h, w_down, tile_e, tile_v, tn):
    n_rows, f = h.shape
    d = w_down.shape[2]
    nt = n_rows // MOE_TM
    grid_spec = pltpu.PrefetchScalarGridSpec(
        num_scalar_prefetch=2,
        grid=(d // tn, nt),
        in_specs=[pl.BlockSpec((MOE_TM, f), lambda j, i, te, tv: (i, 0)),
                  pl.BlockSpec((None, f, tn), lambda j, i, te, tv: (te[i], 0, j))],
        out_specs=pl.BlockSpec((MOE_TM, tn), lambda j, i, te, tv: (i, j)),
    )
    return pl.pallas_call(
        _moe_down_kernel,
        grid_spec=grid_spec,
        out_shape=jax.ShapeDtypeStruct((n_rows, d), F32),
        compiler_params=_cp(("parallel", "arbitrary")),
        name="moe_down",
    )(tile_e, tile_v, h, w_down)


COMBINE_TM = 128


def _moe_combine_kernel(d1_ref, d2_ref, y_hbm, x_ref, rw_ref, g_ref, b_ref, o_ref, buf, sem):
    nslab = buf.shape[2]

    def start(r, _):
        pltpu.make_async_copy(y_hbm.at[d1_ref[0, 0, r]], buf.at[0, r], sem).start()
        pltpu.make_async_copy(y_hbm.at[d2_ref[0, 0, r]], buf.at[1, r], sem).start()
        return 0

    lax.fori_loop(0, COMBINE_TM, start, 0)
    for k in range(2):
        pltpu.make_async_copy(y_hbm.at[pl.ds(0, COMBINE_TM)], buf.at[k], sem).wait()
    rw = rw_ref[...]
    w1 = rw[:, 2:3]
    w2 = rw[:, 3:4]
    z = []
    tot = jnp.zeros((COMBINE_TM, 1), F32)
    for c in range(nslab):
        sl = slice(c * LANES, (c + 1) * LANES)
        zc = ALPHA * x_ref[:, sl] + (w1 * buf[0, :, c, :] + w2 * buf[1, :, c, :])
        z.append(zc)
        tot = tot + jnp.sum(zc, axis=-1, keepdims=True)
    d = nslab * LANES
    mu = tot * (1.0 / d)
    ss = jnp.zeros((COMBINE_TM, 1), F32)
    for c in range(nslab):
        z[c] = z[c] - mu
        ss = ss + jnp.sum(z[c] * z[c], axis=-1, keepdims=True)
    r = lax.rsqrt(ss * (1.0 / d) + LN_EPS)
    for c in range(nslab):
        sl = slice(c * LANES, (c + 1) * LANES)
        o_ref[:, sl] = z[c] * r * g_ref[:, sl] + b_ref[:, sl]


def moe_combine(y_rows, x, route, dest1, dest2, g, beta):
    n, d = x.shape
    nt = n // COMBINE_TM
    y3 = _row_slabs(y_rows)
    idx_spec = pl.BlockSpec((1, 1, COMBINE_TM), lambda i: (i, 0, 0), memory_space=pltpu.SMEM)
    return pl.pallas_call(
        _moe_combine_kernel,
        grid=(nt,),
        in_specs=[idx_spec, idx_spec,
                  pl.BlockSpec(memory_space=pl.ANY),
                  pl.BlockSpec((COMBINE_TM, d), lambda i: (i, 0)),
                  pl.BlockSpec((COMBINE_TM, LANES), lambda i: (i, 0)),
                  pl.BlockSpec((1, d), lambda i: (0, 0)),
                  pl.BlockSpec((1, d), lambda i: (0, 0))],
        out_specs=pl.BlockSpec((COMBINE_TM, d), lambda i: (i, 0)),
        out_shape=jax.ShapeDtypeStruct((n, d), F32),
        scratch_shapes=[pltpu.VMEM((2, COMBINE_TM) + y3.shape[1:], F32), pltpu.SemaphoreType.DMA],
        compiler_params=_cp(("arbitrary",)),
        name="moe_combine",
    )(dest1.reshape(nt, 1, COMBINE_TM), dest2.reshape(nt, 1, COMBINE_TM), y3, x, route,
      g.reshape(1, d).astype(F32), beta.reshape(1, d).astype(F32))


def moe_layer(x, xb, w_router, b_router, w_gate, w_up, w_down, g, beta):
    n, d = x.shape
    route, cnt = moe_router(x, w_router, b_router)
    e1 = route[:, 0].astype(I32)
    e2 = route[:, 1].astype(I32)
    counts = cnt[0, :N_EXPERTS].astype(I32)
    padded = (counts + MOE_TM - 1) // MOE_TM * MOE_TM
    pad_end = jnp.cumsum(padded)
    pad_start = pad_end - padded
    dest1 = pad_start[e1] + route[:, 4].astype(I32)
    dest2 = pad_start[e2] + route[:, 5].astype(I32)
    n_rows = 2 * n + N_EXPERTS * MOE_TM
    nt = n_rows // MOE_TM
    tok = jnp.arange(n, dtype=I32)
    row_tok = jnp.zeros((n_rows,), I32).at[dest1].set(tok).at[dest2].set(tok)
    tile_start = jnp.arange(nt, dtype=I32) * MOE_TM
    tile_e = jnp.minimum(jnp.searchsorted(pad_end, tile_start, side='right'), N_EXPERTS - 1).astype(I32)
    tile_v = (tile_start < pad_end[-1]).astype(I32)
    xs = row_gather(xb, row_tok)
    h = moe_up(xs, w_gate, w_up, tile_e, tile_v, tn=min(512, w_gate.shape[2]))
    y_rows = moe_down(h, w_down, tile_e, tile_v, tn=min(512, d))
    return moe_combine(y_rows, x, route, dest1, dest2, g, beta)


def _cmp_sel_overlap(n_cmp_pad, n_sel):
    i = np.arange(n_cmp_pad)[:, None]
    j = np.arange(LANES)[None, :]
    lo = np.maximum(i * CMP_STRIDE, j * SEL_LEN)
    hi = np.minimum(i * CMP_STRIDE + CMP_LEN, (j + 1) * SEL_LEN)
    ov = np.maximum(hi - lo, 0) / CMP_LEN
    ov[:, n_sel:] = 0.0
    return ov.astype(np.float32)


def _gate_columns():
    src = -np.ones((KV_GROUPS * LANES,), np.int64)
    for g in range(KV_GROUPS):
        for j in range(HPG):
            for br in range(3):
                src[g * LANES + br * HPG + j] = g * HPG * 3 + j * 3 + br
    return src


def nsa_layer(xb, b, s, w_in, pe_k, w1_k, w2_k, pe_v, w1_v, w2_v, band, cmpb):
    n = xb.shape[0]
    hd = N_HEADS * HEAD_DIM
    gw = KV_GROUPS * HEAD_DIM
    w_main = w_in[:, :hd + 6 * gw].astype(BF)
    scale = jnp.concatenate([jnp.full((hd,), HEAD_DIM ** -0.5, F32), jnp.ones((6 * gw,), F32)])
    ybf = matmul_scaled(xb, w_main, scale, BF, tm=512, tn=512)
    src = _gate_columns()
    w_gl = jnp.where(jnp.asarray(src >= 0), w_in[:, hd + 6 * gw:][:, np.maximum(src, 0)], 0.0).astype(BF)
    gl = matmul_scaled(xb, w_gl, jnp.ones((w_gl.shape[1],), F32), F32, tm=512, tn=512)
    cb = hd // gw
    kcmp = nsa_compress(ybf, b, s, cb + 0, pe_k, w1_k, w2_k)
    vcmp = nsa_compress(ybf, b, s, cb + 1, pe_v, w1_v, w2_v)
    overlap = jnp.asarray(_cmp_sel_overlap(s // CMP_STRIDE, s // SEL_LEN)).astype(BF)
    oc, nsel = nsa_cmp_select(ybf, kcmp, vcmp, cmpb, overlap, b, s)
    c128 = hd // HEAD_DIM
    o = nsa_main(ybf, nsel, band, gl, oc, b, s,
                 col_ksl=c128 + 2 * KV_GROUPS, col_vsl=c128 + 3 * KV_GROUPS,
                 col_kw=c128 + 4 * KV_GROUPS, col_vw=c128 + 5 * KV_GROUPS)
    return o.reshape(n, hd)


def dsa_layer(xb, b, s, w_in, kv_norm_g, w_uk, w_uv, band):
    n = xb.shape[0]
    hd = N_HEADS * HEAD_DIM
    o_c = hd
    o_qi = hd + KV_RANK
    o_ki = o_qi + IDX_HEADS * IDX_DIM
    o_wi = o_ki + IDX_DIM
    w_q = jnp.concatenate([w_in[:, :hd], w_in[:, o_qi:o_ki]], axis=1).astype(BF)
    scale_q = jnp.concatenate([jnp.ones((hd,), F32), jnp.full((IDX_HEADS * IDX_DIM,), IDX_DIM ** -0.5, F32)])
    qbf = matmul_scaled(xb, w_q, scale_q, BF, tm=512, tn=512)
    cn = matmul_rmsnorm(xb, w_in[:, o_c:o_qi].astype(BF), kv_norm_g, tm=512)
    w_kw = jnp.pad(w_in[:, o_ki:], ((0, 0), (0, LANES - IDX_DIM - IDX_HEADS))).astype(BF)
    scale_kw = jnp.concatenate([jnp.ones((IDX_DIM,), F32), jnp.full((IDX_HEADS,), IDX_HEADS ** -0.5, F32),
                                jnp.zeros((LANES - IDX_DIM - IDX_HEADS,), F32)])
    kw = matmul_scaled(xb, w_kw, scale_kw, F32, tm=512, tn=LANES)
    nmask = dsa_index(qbf, kw, b, s, col_qi=hd // (IDX_HEADS * IDX_DIM))
    o = dsa_attention(qbf, cn, nmask, band[:, :2], w_uk, w_uv, b, s)
    return o.reshape(n, hd)


def kernel(x, rel_bias, nsa_w_in, nsa_cmp_pe_k, nsa_cmp_w1_k, nsa_cmp_w2_k, nsa_cmp_pe_v, nsa_cmp_w1_v, nsa_cmp_w2_v, nsa_w_out, dsa_w_in, dsa_kv_norm_g, dsa_w_uk, dsa_w_uv, dsa_w_out, ffn_w_gate, ffn_w_up, ffn_w_down, moe_w_router, moe_b_router, moe_w_gate, moe_w_up, moe_w_down, ln_mix_g, ln_mix_b, ln_ffn_g, ln_ffn_b):
    b, s, d = x.shape
    n = b * s
    assert s % TQ == 0 and s // SEL_LEN <= LANES
    x0 = x.reshape(n, d)
    band, cmpb = _bias_tables(rel_bias, s // CMP_STRIDE)
    o = nsa_layer(x0.astype(BF), b, s, nsa_w_in, nsa_cmp_pe_k, nsa_cmp_w1_k, nsa_cmp_w2_k,
                  nsa_cmp_pe_v, nsa_cmp_w1_v, nsa_cmp_w2_v, band, cmpb)
    x1, x1b = matmul_residual_ln(o, nsa_w_out.astype(BF), x0, ln_mix_g[0], ln_mix_b[0], tm=256, tk=d)
    hff = swiglu_up(x1b, ffn_w_gate.astype(BF), ffn_w_up.astype(BF), tm=512, tn=512)
    x2, x2b = matmul_residual_ln(hff, ffn_w_down.astype(BF), x1, ln_ffn_g[0], ln_ffn_b[0], tm=256, tk=512)
    o = dsa_layer(x2b, b, s, dsa_w_in, dsa_kv_norm_g, dsa_w_uk, dsa_w_uv, band)
    x3, x3b = matmul_residual_ln(o, dsa_w_out.astype(BF), x2, ln_mix_g[1], ln_mix_b[1], tm=256, tk=d)
    out = moe_layer(x3, x3b, moe_w_router, moe_b_router, moe_w_gate, moe_w_up, moe_w_down,
                    ln_ffn_g[1], ln_ffn_b[1])
    return out.reshape(b, s, d)
```

```python
import functools
import math

import numpy as np
import jax
import jax.numpy as jnp
from jax import lax
from jax.experimental import pallas as pl
from jax.experimental.pallas import tpu as pltpu

F32 = jnp.float32
BF = jnp.bfloat16
I32 = jnp.int32

N_HEADS = 16
HEAD_DIM = 128
KV_GROUPS = 4
HPG = N_HEADS // KV_GROUPS
CMP_LEN = 32
CMP_STRIDE = 16
SEL_LEN = 64
SEL_TOP = 16
WINDOW = 512
SEL_FORCE = 1e4
KV_RANK = 256
IDX_HEADS = 8
IDX_DIM = 64
IDX_TOPK = 256
REL_BUCKETS = 32
REL_MAX_DIST = 128
N_EXPERTS = 8
DEPTH = 2
ALPHA = (2 * DEPTH) ** 0.25
LN_EPS = 1e-5
RMS_EPS = 1e-6
NEG_INF = -1e30

BIG = float(2.0 ** 100)
M_INIT = -3.0e38
TQ = 256
LANES = 128
VMEM_LIMIT_BYTES = 60000 * 1024


def _cp(sem, vmem=None):
    return pltpu.CompilerParams(dimension_semantics=sem, vmem_limit_bytes=vmem or VMEM_LIMIT_BYTES)


def _dot(a, b):
    return jnp.dot(a.astype(BF), b.astype(BF), preferred_element_type=F32)


def _dot_nt(a, b):
    return lax.dot_general(a.astype(BF), b.astype(BF), (((1,), (1,)), ((), ())),
                           preferred_element_type=F32)


def _layer_norm_rows(z, g, b):
    mu = jnp.mean(z, axis=-1, keepdims=True)
    zc = z - mu
    var = jnp.mean(zc * zc, axis=-1, keepdims=True)
    return zc * lax.rsqrt(var + LN_EPS) * g + b


def _mm_scale_kernel(a_ref, b_ref, s_ref, o_ref):
    o_ref[...] = (_dot(a_ref[...], b_ref[...]) * s_ref[...]).astype(o_ref.dtype)


def matmul_scaled(a, b, scale, out_dtype, tm, tn):
    m, k = a.shape
    n = b.shape[1]
    return pl.pallas_call(
        _mm_scale_kernel,
        grid=(m // tm, n // tn),
        in_specs=[pl.BlockSpec((tm, k), lambda i, j: (i, 0)),
                  pl.BlockSpec((k, tn), lambda i, j: (0, j)),
                  pl.BlockSpec((1, tn), lambda i, j: (0, j))],
        out_specs=pl.BlockSpec((tm, tn), lambda i, j: (i, j)),
        out_shape=jax.ShapeDtypeStruct((m, n), out_dtype),
        compiler_params=_cp(("parallel", "parallel")),
        name="matmul_scaled",
    )(a, b, scale.reshape(1, n).astype(F32))


def _mm_rms_kernel(a_ref, b_ref, g_ref, o_ref):
    c = _dot(a_ref[...], b_ref[...])
    r = lax.rsqrt(jnp.mean(c * c, axis=-1, keepdims=True) + RMS_EPS)
    o_ref[...] = (c * r * g_ref[...]).astype(o_ref.dtype)


def matmul_rmsnorm(a, b, g, tm):
    m, k = a.shape
    n = b.shape[1]
    return pl.pallas_call(
        _mm_rms_kernel,
        grid=(m // tm,),
        in_specs=[pl.BlockSpec((tm, k), lambda i: (i, 0)),
                  pl.BlockSpec((k, n), lambda i: (0, 0)),
                  pl.BlockSpec((1, n), lambda i: (0, 0))],
        out_specs=pl.BlockSpec((tm, n), lambda i: (i, 0)),
        out_shape=jax.ShapeDtypeStruct((m, n), BF),
        compiler_params=_cp(("parallel",)),
        name="matmul_rmsnorm",
    )(a, b, g.reshape(1, n).astype(F32))


def _mm_ln_kernel(a_ref, b_ref, r_ref, g_ref, be_ref, of_ref, ob_ref, acc_ref, *, nk):
    kk = pl.program_id(1)

    @pl.when(kk == 0)
    def _():
        acc_ref[...] = jnp.zeros_like(acc_ref)

    acc_ref[...] += _dot(a_ref[...], b_ref[...])

    @pl.when(kk == nk - 1)
    def _():
        y = _layer_norm_rows(ALPHA * r_ref[...] + acc_ref[...], g_ref[...], be_ref[...])
        of_ref[...] = y
        ob_ref[...] = y.astype(BF)


def matmul_residual_ln(a, b, res, g, beta, tm, tk):
    m, k = a.shape
    n = b.shape[1]
    nk = k // tk
    return pl.pallas_call(
        functools.partial(_mm_ln_kernel, nk=nk),
        grid=(m // tm, nk),
        in_specs=[pl.BlockSpec((tm, tk), lambda i, kk: (i, kk)),
                  pl.BlockSpec((tk, n), lambda i, kk: (kk, 0)),
                  pl.BlockSpec((tm, n), lambda i, kk: (i, 0)),
                  pl.BlockSpec((1, n), lambda i, kk: (0, 0)),
                  pl.BlockSpec((1, n), lambda i, kk: (0, 0))],
        out_specs=[pl.BlockSpec((tm, n), lambda i, kk: (i, 0)),
                   pl.BlockSpec((tm, n), lambda i, kk: (i, 0))],
        out_shape=[jax.ShapeDtypeStruct((m, n), F32), jax.ShapeDtypeStruct((m, n), BF)],
        scratch_shapes=[pltpu.VMEM((tm, n), F32)],
        compiler_params=_cp(("parallel", "arbitrary")),
        name="matmul_residual_ln",
    )(a, b, res, g.reshape(1, n).astype(F32), beta.reshape(1, n).astype(F32))


def _silu(x):
    return x * (1.0 / (1.0 + jnp.exp(-x)))


def _swiglu_up_kernel(a_ref, wg_ref, wu_ref, o_ref):
    a = a_ref[...]
    o_ref[...] = (_silu(_dot(a, wg_ref[...])) * _dot(a, wu_ref[...])).astype(o_ref.dtype)


def swiglu_up(a, wg, wu, tm, tn):
    m, k = a.shape
    n = wg.shape[1]
    return pl.pallas_call(
        _swiglu_up_kernel,
        grid=(m // tm, n // tn),
        in_specs=[pl.BlockSpec((tm, k), lambda i, j: (i, 0)),
                  pl.BlockSpec((k, tn), lambda i, j: (0, j)),
                  pl.BlockSpec((k, tn), lambda i, j: (0, j))],
        out_specs=pl.BlockSpec((tm, tn), lambda i, j: (i, j)),
        out_shape=jax.ShapeDtypeStruct((m, n), BF),
        compiler_params=_cp(("parallel", "parallel")),
        name="swiglu_up",
    )(a, wg, wu)


def _bucket_of_distance():
    n = np.arange(REL_MAX_DIST + 1)
    exact = REL_BUCKETS // 2
    nf = np.maximum(n, exact).astype(np.float64)
    large = exact + (np.log(nf / exact) / math.log(REL_MAX_DIST / exact) * (REL_BUCKETS - exact)).astype(np.int64)
    return np.where(n < exact, n, np.minimum(large, REL_BUCKETS - 1)).astype(np.int32)


def _bucket_starts():
    bk = _bucket_of_distance()
    return [int(np.argmax(bk >= b)) for b in range(REL_BUCKETS)]


def _bias_kernel(tab_ref, band_ref, cmp_ref, *, starts):
    h = pl.program_id(0)
    far = tab_ref[REL_BUCKETS - 1, h]

    def lookup(d):
        val = jnp.zeros(d.shape, F32)
        for b in range(REL_BUCKETS - 2, -1, -1):
            val = jnp.where(d < starts[b + 1], tab_ref[b, h] - far, val)
        return val

    j = lax.broadcasted_iota(I32, (TQ, TQ), 0)
    i = lax.broadcasted_iota(I32, (TQ, TQ), 1)
    for r in range(3):
        d = TQ * r + i - j
        band_ref[r] = jnp.where((d < 0) | (d >= WINDOW), -BIG, lookup(d))
    ncp = cmp_ref.shape[1]
    i = lax.broadcasted_iota(I32, (TQ, ncp), 0)
    c = lax.broadcasted_iota(I32, (TQ, ncp), 1)
    d = i - CMP_STRIDE * (c - 16) - (CMP_LEN - 1)
    cmp_ref[...] = jnp.where(c >= 32, 0.0, jnp.where(d < 0, -BIG, lookup(d)))


def _bias_tables(rel_bias, n_cmp_pad):
    return pl.pallas_call(
        functools.partial(_bias_kernel, starts=_bucket_starts()),
        grid=(N_HEADS,),
        in_specs=[pl.BlockSpec(memory_space=pltpu.SMEM)],
        out_specs=[pl.BlockSpec((None, 3, TQ, TQ), lambda h: (h // HPG, 0, 0, h % HPG)),
                   pl.BlockSpec((None, TQ, n_cmp_pad), lambda h: (h // HPG, h % HPG, 0))],
        out_shape=[jax.ShapeDtypeStruct((N_HEADS // HPG, 3, TQ, HPG * TQ), F32),
                   jax.ShapeDtypeStruct((N_HEADS // HPG, HPG * TQ, n_cmp_pad), F32)],
        compiler_params=_cp(("parallel",)),
        name="bias_tables",
    )(rel_bias.astype(F32))


def _heads_to_rows(q):
    return jnp.concatenate([q[:, j * HEAD_DIM:(j + 1) * HEAD_DIM] for j in range(HPG)], axis=0)


def _osm_update(carry, st, vt, acc_ref):
    m, l = carry
    m_new = jnp.maximum(m, jnp.max(st, axis=0, keepdims=True))
    alpha = jnp.exp(m - m_new)
    p = jnp.exp(st - m_new)
    l = alpha * l + jnp.sum(p, axis=0, keepdims=True)
    acc_ref[...] = alpha * acc_ref[...] + jnp.dot(vt, p.astype(BF), preferred_element_type=F32)
    return m_new, l


def _osm_init(rows, acc_ref):
    acc_ref[...] = jnp.zeros_like(acc_ref)
    return jnp.full((1, rows), M_INIT, F32), jnp.zeros((1, rows), F32)


def _store_transposed(src_ref, dst_ref, nchunk):
    def body(kc, _):
        start = pl.multiple_of(kc * TQ, TQ)
        dst_ref[kc] = src_ref[pl.ds(start, TQ), :].astype(F32).T.astype(dst_ref.dtype)
        return 0

    lax.fori_loop(0, nchunk, body, 0)


def _gelu_tanh(x):
    return 0.5 * x * (1.0 + jnp.tanh(math.sqrt(2.0 / math.pi) * (x + 0.044715 * (x * x * x))))


def _compress_kernel(*refs, nch):
    x_refs = refs[:CMP_STRIDE]
    pe_ref, w1_ref, w2_ref, o_ref = refs[CMP_STRIDE:]
    acc_a = [jnp.zeros((nch, HEAD_DIM), F32) for _ in range(KV_GROUPS)]
    acc_b = [jnp.zeros((nch, HEAD_DIM), F32) for _ in range(KV_GROUPS)]
    for l in range(CMP_STRIDE):
        x = x_refs[l][...].astype(F32)
        xa = (x + pe_ref[l:l + 1, :]).astype(BF)
        xb = (x + pe_ref[CMP_STRIDE + l:CMP_STRIDE + l + 1, :]).astype(BF)
        for g in range(KV_GROUPS):
            sl = slice(g * HEAD_DIM, (g + 1) * HEAD_DIM)
            acc_a[g] = acc_a[g] + jnp.dot(xa[:, sl], w1_ref[l], preferred_element_type=F32)
            acc_b[g] = acc_b[g] + jnp.dot(xb[:, sl], w1_ref[CMP_STRIDE + l], preferred_element_type=F32)
    for g in range(KV_GROUPS):
        pre = acc_a[g] + pltpu.roll(acc_b[g], nch - 1, axis=0)
        hid = _gelu_tanh(pre).astype(BF)
        o_ref[:, g * HEAD_DIM:(g + 1) * HEAD_DIM] = jnp.dot(hid, w2_ref[...], preferred_element_type=F32).astype(BF)


def nsa_compress(ybf, b, s, col_block, pe, w1, w2):
    ncols = ybf.shape[1]
    nch = s // CMP_STRIDE
    blk_w = KV_GROUPS * HEAD_DIM
    per_tok = ncols // blk_w
    y3 = ybf.reshape(b, nch, CMP_STRIDE * ncols)
    pe_t = jnp.tile(pe.astype(F32), (1, KV_GROUPS))
    in_specs = [pl.BlockSpec((None, nch, blk_w), (lambda bi, l=l: (bi, 0, l * per_tok + col_block)))
                for l in range(CMP_STRIDE)]
    in_specs += [pl.BlockSpec((CMP_LEN, blk_w), lambda bi: (0, 0)),
                 pl.BlockSpec((CMP_LEN, HEAD_DIM, HEAD_DIM), lambda bi: (0, 0, 0)),
                 pl.BlockSpec((HEAD_DIM, HEAD_DIM), lambda bi: (0, 0))]
    return pl.pallas_call(
        functools.partial(_compress_kernel, nch=nch),
        grid=(b,),
        in_specs=in_specs,
        out_specs=pl.BlockSpec((None, nch, blk_w), lambda bi: (bi, 0, 0)),
        out_shape=jax.ShapeDtypeStruct((b, nch, blk_w), BF),
        compiler_params=_cp(("parallel",)),
        name="nsa_compress",
    )(*([y3] * CMP_STRIDE), pe_t, w1.astype(BF), w2.astype(BF))


def _nsa_cmp_kernel(q_ref, kc_ref, vc_ref, cb_ref, ov_ref, oc_ref, ns_ref, *, n_sel, n_top):
    qt = pl.program_id(2)
    t0 = qt * TQ
    ncp = kc_ref.shape[0]
    qs = _heads_to_rows(q_ref[...])
    s = _dot_nt(qs, kc_ref[...])
    shift = lax.rem(qt * (TQ // CMP_STRIDE) + (ncp - 16), ncp)
    s = s + pltpu.roll(cb_ref[...], shift, axis=1)
    row = lax.broadcasted_iota(I32, (HPG * TQ, ncp), 0) & (TQ - 1)
    col = lax.broadcasted_iota(I32, (HPG * TQ, ncp), 1)
    vis = (col * CMP_STRIDE + (CMP_LEN - 1)) <= (t0 + row)
    s = jnp.where(vis, s, -BIG)
    m = jnp.max(s, axis=-1, keepdims=True)
    p = jnp.where(vis, jnp.exp(s - m), 0.0)
    l = jnp.sum(p, axis=-1, keepdims=True)
    p = p * jnp.where(l > 0.0, 1.0 / l, 0.0)
    pb = p.astype(BF)
    oc = jnp.dot(pb, vc_ref[...], preferred_element_type=F32)
    score = jnp.zeros((TQ, LANES), F32)
    for j in range(HPG):
        oc_ref[:, j * HEAD_DIM:(j + 1) * HEAD_DIM] = oc[j * TQ:(j + 1) * TQ].astype(oc_ref.dtype)
        score = score + jnp.dot(pb[j * TQ:(j + 1) * TQ], ov_ref[...], preferred_element_type=F32)
    blk = lax.broadcasted_iota(I32, (TQ, LANES), 1)
    t = t0 + lax.broadcasted_iota(I32, (TQ, LANES), 0)
    cur = lax.shift_right_logical(t, int(math.log2(SEL_LEN)))
    forced = (blk == 0) | (blk == cur) | (blk == cur - 1)
    visible = blk * SEL_LEN <= t
    sc = jnp.where(forced, SEL_FORCE, jnp.where(visible, score, -1.0))
    sc = jnp.where(blk < n_sel, sc, -2.0)
    blkf = blk.astype(F32)

    def pick_one(_, carry):
        sc, sel = carry
        mx = jnp.max(sc, axis=-1, keepdims=True)
        first = jnp.min(jnp.where(sc == mx, blkf, float(LANES)), axis=-1, keepdims=True)
        hit = blkf == first
        return jnp.where(hit, -3.0, sc), jnp.where(hit, 1.0, sel)

    _, sel = lax.fori_loop(0, n_top, pick_one, (sc, jnp.zeros((TQ, LANES), F32)))
    ns_ref[...] = jnp.where(sel > 0.5, 0.0, BIG).astype(BF)


def nsa_cmp_select(ybf, kcmp, vcmp, cmpb, overlap, b, s):
    ncp = kcmp.shape[1]
    n_sel = s // SEL_LEN
    n_top = min(SEL_TOP, n_sel)
    y3 = ybf.reshape(b, s, ybf.shape[1])
    gw = HPG * HEAD_DIM
    return pl.pallas_call(
        functools.partial(_nsa_cmp_kernel, n_sel=n_sel, n_top=n_top),
        grid=(b, KV_GROUPS, s // TQ),
        in_specs=[pl.BlockSpec((None, TQ, gw), lambda bi, g, qt: (bi, qt, g)),
                  pl.BlockSpec((None, ncp, HEAD_DIM), lambda bi, g, qt: (bi, 0, g)),
                  pl.BlockSpec((None, ncp, HEAD_DIM), lambda bi, g, qt: (bi, 0, g)),
                  pl.BlockSpec((None, HPG * TQ, ncp), lambda bi, g, qt: (g, 0, 0)),
                  pl.BlockSpec((ncp, LANES), lambda bi, g, qt: (0, 0))],
        out_specs=[pl.BlockSpec((None, TQ, gw), lambda bi, g, qt: (bi, qt, g)),
                   pl.BlockSpec((None, None, TQ, LANES), lambda bi, g, qt: (bi, g, qt, 0))],
        out_shape=[jax.ShapeDtypeStruct((b, s, KV_GROUPS * gw), BF),
                   jax.ShapeDtypeStruct((b, KV_GROUPS, s, LANES), BF)],
        compiler_params=_cp(("parallel", "parallel", "parallel")),
        name="nsa_cmp_select",
    )(y3, kcmp, vcmp, cmpb, overlap)


def _nsa_main_kernel(q_ref, ks_ref, vs_ref, kw_ref, vw_ref, ns_ref, band_ref, gl_ref, oc_ref, o_ref,
                     vst_ref, vwt_ref, accs_ref, accw_ref, *, nchunk):
    qt = pl.program_id(2)
    rows = HPG * TQ

    @pl.when(qt == 0)
    def _():
        _store_transposed(vs_ref, vst_ref, nchunk)
        _store_transposed(vw_ref, vwt_ref, nchunk)

    qs = _heads_to_rows(q_ref[...])
    ns4 = jnp.concatenate([ns_ref[...]] * HPG, axis=0)
    qp = jnp.concatenate([qs, ns4], axis=1)
    krow = lax.broadcasted_iota(I32, (TQ, LANES), 0)
    klane = lax.broadcasted_iota(I32, (TQ, LANES), 1)
    kblk = lax.shift_right_logical(krow, int(math.log2(SEL_LEN)))

    def sel_chunk(kc, carry, bias):
        start = pl.multiple_of(kc * TQ, TQ)
        k = ks_ref[pl.ds(start, TQ), :]
        oh = jnp.where(klane == kblk + kc * (TQ // SEL_LEN), -1.0, 0.0).astype(BF)
        st = _dot_nt(jnp.concatenate([k, oh], axis=1), qp)
        if bias is not None:
            st = st + bias
        return _osm_update(carry, st, vst_ref[kc], accs_ref)

    carry = lax.fori_loop(0, jnp.maximum(qt - 1, 0), lambda kc, c: sel_chunk(kc, c, None),
                          _osm_init(rows, accs_ref))
    carry = sel_chunk(jnp.maximum(qt - 1, 0), carry, jnp.where(qt >= 1, band_ref[1], -BIG))
    m_s, l_s = sel_chunk(qt, carry, band_ref[0])

    def win_chunk(r, carry):
        kc = jnp.maximum(qt - r, 0)
        start = pl.multiple_of(kc * TQ, TQ)
        st = _dot_nt(kw_ref[pl.ds(start, TQ), :], qs) + jnp.where(qt >= r, band_ref[r], -BIG)
        return _osm_update(carry, st, vwt_ref[kc], accw_ref)

    carry = _osm_init(rows, accw_ref)
    for r in (2, 1, 0):
        carry = win_chunk(r, carry)
    m_w, l_w = carry

    o_s = accs_ref[...] * (1.0 / l_s)
    o_w = accw_ref[...] * (1.0 / l_w)
    gates = 1.0 / (1.0 + jnp.exp(-gl_ref[...]))
    oc = oc_ref[...].astype(F32)
    for j in range(HPG):
        sl = slice(j * HEAD_DIM, (j + 1) * HEAD_DIM)
        rs = slice(j * TQ, (j + 1) * TQ)
        o = (gates[:, j:j + 1] * oc[:, sl] + gates[:, HPG + j:HPG + j + 1] * o_s[:, rs].T
             + gates[:, 2 * HPG + j:2 * HPG + j + 1] * o_w[:, rs].T)
        o_ref[:, sl] = o.astype(o_ref.dtype)


def nsa_main(ybf, nsel, band, gl, oc, b, s, col_ksl, col_vsl, col_kw, col_vw):
    y3 = ybf.reshape(b, s, ybf.shape[1])
    gw = HPG * HEAD_DIM
    kv_spec = lambda cb: pl.BlockSpec((None, s, HEAD_DIM), lambda bi, g, qt: (bi, 0, cb + g))
    nchunk = s // TQ
    return pl.pallas_call(
        functools.partial(_nsa_main_kernel, nchunk=nchunk),
        grid=(b, KV_GROUPS, nchunk),
        in_specs=[pl.BlockSpec((None, TQ, gw), lambda bi, g, qt: (bi, qt, g)),
                  kv_spec(col_ksl), kv_spec(col_vsl), kv_spec(col_kw), kv_spec(col_vw),
                  pl.BlockSpec((None, None, TQ, LANES), lambda bi, g, qt: (bi, g, qt, 0)),
                  pl.BlockSpec((None, 3, TQ, HPG * TQ), lambda bi, g, qt: (g, 0, 0, 0)),
                  pl.BlockSpec((None, TQ, LANES), lambda bi, g, qt: (bi, qt, g)),
                  pl.BlockSpec((None, TQ, gw), lambda bi, g, qt: (bi, qt, g))],
        out_specs=pl.BlockSpec((None, TQ, gw), lambda bi, g, qt: (bi, qt, g)),
        out_shape=jax.ShapeDtypeStruct((b, s, KV_GROUPS * gw), BF),
        scratch_shapes=[pltpu.VMEM((nchunk, HEAD_DIM, TQ), BF), pltpu.VMEM((nchunk, HEAD_DIM, TQ), BF),
                        pltpu.VMEM((HEAD_DIM, HPG * TQ), F32), pltpu.VMEM((HEAD_DIM, HPG * TQ), F32)],
        compiler_params=_cp(("parallel", "parallel", "arbitrary")),
        name="nsa_main",
    )(y3, y3, y3, y3, y3, nsel, band, gl.reshape(b, s, gl.shape[1]), oc)


def _dsa_index_kernel(qi_ref, kw_ref, wq_ref, nm_ref, ka_ref, kb_ref, key_ref, *, k_sel, nchunk, idx_bits):
    qt = pl.program_id(1)
    t0 = qt * TQ
    half = LANES // 2

    @pl.when(qt == 0)
    def _():
        lane = lax.broadcasted_iota(I32, kw_ref.shape, 1)
        ka = jnp.where(lane < IDX_DIM, kw_ref[...], 0.0)
        ka_ref[...] = ka.astype(BF)
        kb_ref[...] = pltpu.roll(ka, half, axis=1).astype(BF)

    qi = qi_ref[...]
    npair = IDX_HEADS // 2
    lq = jnp.concatenate([qi[:, p * LANES:(p + 1) * LANES] for p in range(npair)], axis=0)
    wt = wq_ref[...].T
    w_r = [wt[IDX_DIM + h:IDX_DIM + h + 1, :] for h in range(IDX_HEADS)]
    kpos = lax.broadcasted_iota(I32, (TQ, TQ), 0)
    qpos = t0 + lax.broadcasted_iota(I32, (TQ, TQ), 1)

    def score_chunk(kc, _):
        start = pl.multiple_of(kc * TQ, TQ)
        da = _dot_nt(ka_ref[pl.ds(start, TQ), :], lq)
        db = _dot_nt(kb_ref[pl.ds(start, TQ), :], lq)
        sc = jnp.zeros((TQ, TQ), F32)
        for p in range(npair):
            cs = slice(p * TQ, (p + 1) * TQ)
            sc = sc + w_r[2 * p] * jnp.maximum(da[:, cs], 0.0) + w_r[2 * p + 1] * jnp.maximum(db[:, cs], 0.0)
        sc = jnp.where(sc == 0.0, 0.0, sc)
        sc = jnp.where(kc * TQ + kpos <= qpos, sc, NEG_INF)
        bits = lax.bitcast_convert_type(sc, I32)
        key_ref[kc] = jnp.where(bits < 0, bits ^ 0x7FFFFFFF, bits)
        return 0

    nproc = qt + 1
    lax.fori_loop(0, nproc, score_chunk, 0)

    def count(pred):
        def body(kc, acc):
            return acc + jnp.sum(jnp.where(pred(key_ref[kc], kc), 1.0, 0.0), axis=0, keepdims=True)
        return lax.fori_loop(0, nproc, body, jnp.zeros((1, TQ), F32))

    def count_ge(cand):
        return count(lambda k, kc: k >= cand)

    kf = float(k_sel)
    thr0 = jnp.full((1, TQ), -2 ** 31, I32)
    done0 = jnp.where(count_ge(thr0) == kf, 1.0, 0.0)

    def search_cond(state):
        i, _, done = state
        return (i < 32) & (jnp.min(done) < 0.5)

    def search_body(state):
        i, thr, done = state
        cand = thr ^ lax.shift_left(jnp.int32(1), jnp.int32(31) - i)
        cnt = count_ge(cand)
        take = (cnt >= kf) & (done < 0.5)
        thr = jnp.where(take, cand, thr)
        done = jnp.where(take & (cnt == kf), 1.0, done)
        return i + 1, thr, done

    _, thr, done = lax.while_loop(search_cond, search_body, (jnp.int32(0), thr0, done0))

    def tie_break(_):
        need = kf - count_ge(thr + 1)

        def ties_below(bound):
            return count(lambda k, kc: (k == thr) & (kc * TQ + kpos < bound))

        def idx_bit(i, jm):
            cand = jm | lax.shift_left(jnp.int32(1), jnp.int32(idx_bits - 1) - i)
            return jnp.where(ties_below(cand) < need, cand, jm)

        jm = lax.fori_loop(0, idx_bits, idx_bit, jnp.zeros((1, TQ), I32))
        return jnp.where(done > 0.5, jnp.int32(2 ** 30), jm)

    jm = lax.cond(jnp.min(done) < 0.5, tie_break, lambda _: jnp.full((1, TQ), 2 ** 30, I32), 0)

    def write_chunk(kc, _):
        k = key_ref[kc]
        kidx = kc * TQ + kpos
        sel = (kidx <= qpos) & ((k > thr) | ((k == thr) & (kidx <= jm)))
        nm_ref[kc] = jnp.where(sel, 0.0, -BIG).astype(BF)
        return 0

    lax.fori_loop(0, nproc, write_chunk, 0)

    def fill_chunk(kc, _):
        nm_ref[kc] = jnp.full((TQ, TQ), -BIG, BF)
        return 0

    lax.fori_loop(nproc, nchunk, fill_chunk, 0)


def dsa_index(qbf, kw, b, s, col_qi):
    nchunk = s // TQ
    k_sel = min(IDX_TOPK, s // 4)
    q3 = qbf.reshape(b, s, qbf.shape[1])
    kw3 = kw.reshape(b, s, LANES)
    return pl.pallas_call(
        functools.partial(_dsa_index_kernel, k_sel=k_sel, nchunk=nchunk, idx_bits=int(math.log2(s))),
        grid=(b, nchunk),
        in_specs=[pl.BlockSpec((None, TQ, IDX_HEADS * IDX_DIM), lambda bi, qt: (bi, qt, col_qi)),
                  pl.BlockSpec((None, s, LANES), lambda bi, qt: (bi, 0, 0)),
                  pl.BlockSpec((None, TQ, LANES), lambda bi, qt: (bi, qt, 0))],
        out_specs=pl.BlockSpec((None, None, nchunk, TQ, TQ), lambda bi, qt: (bi, qt, 0, 0, 0)),
        out_shape=jax.ShapeDtypeStruct((b, nchunk, nchunk, TQ, TQ), BF),
        scratch_shapes=[pltpu.VMEM((s, LANES), BF), pltpu.VMEM((s, LANES), BF),
                        pltpu.VMEM((nchunk, TQ, TQ), I32)],
        compiler_params=_cp(("parallel", "arbitrary")),
        name="dsa_index",
    )(q3, kw3, kw3)


def _dsa_attn_kernel(q_ref, c_ref, nm_ref, band_ref, wuk_ref, wuvt_ref, o_ref, ct_ref, acc_ref, *, nchunk):
    qt = pl.program_id(1)
    hg = pl.program_id(2)
    rows = HPG * TQ

    @pl.when((qt == 0) & (hg == 0))
    def _():
        _store_transposed(c_ref, ct_ref, nchunk)

    q = q_ref[...]
    ql = jnp.concatenate(
        [(jnp.dot(q[:, j * HEAD_DIM:(j + 1) * HEAD_DIM], wuk_ref[j], preferred_element_type=F32)
          * (HEAD_DIM ** -0.5)).astype(BF) for j in range(HPG)], axis=0)

    def chunk(kc, carry, bias):
        start = pl.multiple_of(kc * TQ, TQ)
        nm = nm_ref[kc].astype(F32)
        st = _dot_nt(c_ref[pl.ds(start, TQ), :], ql) + jnp.concatenate([nm] * HPG, axis=1)
        if bias is not None:
            st = st + bias
        return _osm_update(carry, st, ct_ref[kc], acc_ref)

    carry = lax.fori_loop(0, jnp.maximum(qt - 1, 0), lambda kc, cr: chunk(kc, cr, None),
                          _osm_init(rows, acc_ref))
    carry = chunk(jnp.maximum(qt - 1, 0), carry, jnp.where(qt >= 1, band_ref[1], -BIG))
    m, l = chunk(qt, carry, band_ref[0])
    o_lat = (acc_ref[...] * (1.0 / l)).astype(BF)
    for j in range(HPG):
        ot = jnp.dot(wuvt_ref[j], o_lat[:, j * TQ:(j + 1) * TQ], preferred_element_type=F32)
        o_ref[:, j * HEAD_DIM:(j + 1) * HEAD_DIM] = ot.T.astype(o_ref.dtype)


def dsa_attention(qbf, cn, nmask, band, w_uk, w_uv, b, s):
    nchunk = s // TQ
    q3 = qbf.reshape(b, s, qbf.shape[1])
    c3 = cn.reshape(b, s, KV_RANK)
    gw = HPG * HEAD_DIM
    return pl.pallas_call(
        functools.partial(_dsa_attn_kernel, nchunk=nchunk),
        grid=(b, nchunk, N_HEADS // HPG),
        in_specs=[pl.BlockSpec((None, TQ, gw), lambda bi, qt, hg: (bi, qt, hg)),
                  pl.BlockSpec((None, s, KV_RANK), lambda bi, qt, hg: (bi, 0, 0)),
                  pl.BlockSpec((None, None, nchunk, TQ, TQ), lambda bi, qt, hg: (bi, qt, 0, 0, 0)),
                  pl.BlockSpec((None, 2, TQ, HPG * TQ), lambda bi, qt, hg: (hg, 0, 0, 0)),
                  pl.BlockSpec((HPG, HEAD_DIM, KV_RANK), lambda bi, qt, hg: (hg, 0, 0)),
                  pl.BlockSpec((HPG, HEAD_DIM, KV_RANK), lambda bi, qt, hg: (hg, 0, 0))],
        out_specs=pl.BlockSpec((None, TQ, gw), lambda bi, qt, hg: (bi, qt, hg)),
        out_shape=jax.ShapeDtypeStruct((b, s, N_HEADS * HEAD_DIM), BF),
        scratch_shapes=[pltpu.VMEM((nchunk, KV_RANK, TQ), BF), pltpu.VMEM((KV_RANK, HPG * TQ), F32)],
        compiler_params=_cp(("parallel", "arbitrary", "arbitrary")),
        name="dsa_attention",
    )(q3, c3, nmask, band, w_uk.astype(BF), jnp.swapaxes(w_uv, 1, 2).astype(BF))


ROUTER_TM = 256
MOE_TM = 256


def _router_kernel(x_ref, w_ref, b_ref, o_ref, cnt_ref, carry_ref):
    i = pl.program_id(0)

    @pl.when(i == 0)
    def _():
        carry_ref[...] = jnp.zeros_like(carry_ref)

    tm = x_ref.shape[0]
    logits = jnp.dot(x_ref[...], w_ref[...], preferred_element_type=F32,
                     precision=lax.Precision.HIGHEST) + b_ref[...]
    lane = lax.broadcasted_iota(I32, (tm, LANES), 1)
    lanef = lane.astype(F32)
    lg = jnp.where(lane < N_EXPERTS, logits, -BIG)
    m1 = jnp.max(lg, axis=-1, keepdims=True)
    i1 = jnp.min(jnp.where(lg == m1, lanef, float(LANES)), axis=-1, keepdims=True)
    lg2 = jnp.where(lanef == i1, -BIG, lg)
    m2 = jnp.max(lg2, axis=-1, keepdims=True)
    i2 = jnp.min(jnp.where(lg2 == m2, lanef, float(LANES)), axis=-1, keepdims=True)
    e2 = jnp.exp(m2 - m1)
    den = 1.0 + e2
    w1 = 1.0 / den
    w2 = e2 / den
    hit1 = lanef == i1
    hit2 = lanef == i2
    onehot = jnp.where(hit1 | hit2, 1.0, 0.0)
    r = lax.broadcasted_iota(I32, (tm, tm), 0)
    c = lax.broadcasted_iota(I32, (tm, tm), 1)
    tri = jnp.where(c < r, 1.0, 0.0).astype(BF)
    before = jnp.dot(tri, onehot.astype(BF), preferred_element_type=F32) + carry_ref[...]
    rank1 = jnp.sum(jnp.where(hit1, before, 0.0), axis=-1, keepdims=True)
    rank2 = jnp.sum(jnp.where(hit2, before, 0.0), axis=-1, keepdims=True)
    carry_ref[...] = carry_ref[...] + jnp.sum(onehot, axis=0, keepdims=True)
    vals = (i1, i2, w1, w2, rank1, rank2)
    out = jnp.zeros((tm, LANES), F32)
    for k, v in enumerate(vals):
        out = jnp.where(lane == k, v, out)
    o_ref[...] = out
    cnt_ref[...] = jnp.broadcast_to(carry_ref[...], cnt_ref.shape)


def moe_router(x, w_router, b_router):
    n, d = x.shape
    wp = jnp.pad(w_router.astype(F32), ((0, 0), (0, LANES - N_EXPERTS)))
    bp = jnp.pad(b_router.astype(F32), (0, LANES - N_EXPERTS)).reshape(1, LANES)
    return pl.pallas_call(
        _router_kernel,
        grid=(n // ROUTER_TM,),
        in_specs=[pl.BlockSpec((ROUTER_TM, d), lambda i: (i, 0)),
                  pl.BlockSpec((d, LANES), lambda i: (0, 0)),
                  pl.BlockSpec((1, LANES), lambda i: (0, 0))],
        out_specs=[pl.BlockSpec((ROUTER_TM, LANES), lambda i: (i, 0)),
                   pl.BlockSpec((8, LANES), lambda i: (0, 0))],
        out_shape=[jax.ShapeDtypeStruct((n, LANES), F32), jax.ShapeDtypeStruct((8, LANES), F32)],
        scratch_shapes=[pltpu.VMEM((1, LANES), F32)],
        compiler_params=_cp(("arbitrary",)),
        name="moe_router",
    )(x, wp, bp)


GATHER_ROWS = 512


def _row_slabs(x):
    return x.reshape(x.shape[0], x.shape[1] // LANES, LANES)


def _row_gather_kernel(idx_ref, x_hbm, o_ref, sem):
    def start(r, _):
        pltpu.make_async_copy(x_hbm.at[idx_ref[0, 0, r]], o_ref.at[r], sem).start()
        return 0

    lax.fori_loop(0, GATHER_ROWS, start, 0)
    pltpu.make_async_copy(x_hbm.at[pl.ds(0, GATHER_ROWS)], o_ref, sem).wait()


def row_gather(x, row_idx):
    n_rows = row_idx.shape[0]
    nt = n_rows // GATHER_ROWS
    x3 = _row_slabs(x)
    out = pl.pallas_call(
        _row_gather_kernel,
        grid=(nt,),
        in_specs=[pl.BlockSpec((1, 1, GATHER_ROWS), lambda i: (i, 0, 0), memory_space=pltpu.SMEM),
                  pl.BlockSpec(memory_space=pl.ANY)],
        out_specs=pl.BlockSpec((GATHER_ROWS,) + x3.shape[1:], lambda i: (i, 0, 0)),
        out_shape=jax.ShapeDtypeStruct((n_rows,) + x3.shape[1:], x.dtype),
        scratch_shapes=[pltpu.SemaphoreType.DMA],
        compiler_params=_cp(("arbitrary",)),
        name="moe_row_gather",
    )(row_idx.reshape(nt, 1, GATHER_ROWS), x3)
    return out.reshape(n_rows, x.shape[1])


def _moe_up_kernel(te_ref, tv_ref, x_ref, wg_ref, wu_ref, o_ref):
    i = pl.program_id(1)

    @pl.when(tv_ref[i] > 0)
    def _():
        x = x_ref[...]
        o_ref[...] = (_silu(_dot(x, wg_ref[...])) * _dot(x, wu_ref[...])).astype(o_ref.dtype)

    @pl.when(tv_ref[i] == 0)
    def _():
        o_ref[...] = jnp.zeros_like(o_ref)


def moe_up(xs, w_gate, w_up, tile_e, tile_v, tn):
    n_rows, d = xs.shape
    f = w_gate.shape[2]
    nt = n_rows // MOE_TM
    grid_spec = pltpu.PrefetchScalarGridSpec(
        num_scalar_prefetch=2,
        grid=(f // tn, nt),
        in_specs=[pl.BlockSpec((MOE_TM, d), lambda j, i, te, tv: (i, 0)),
                  pl.BlockSpec((None, d, tn), lambda j, i, te, tv: (te[i], 0, j)),
                  pl.BlockSpec((None, d, tn), lambda j, i, te, tv: (te[i], 0, j))],
        out_specs=pl.BlockSpec((MOE_TM, tn), lambda j, i, te, tv: (i, j)),
    )
    return pl.pallas_call(
        _moe_up_kernel,
        grid_spec=grid_spec,
        out_shape=jax.ShapeDtypeStruct((n_rows, f), BF),
        compiler_params=_cp(("parallel", "arbitrary")),
        name="moe_up",
    )(tile_e, tile_v, xs, w_gate, w_up)


def _moe_down_kernel(te_ref, tv_ref, h_ref, wd_ref, o_ref):
    i = pl.program_id(1)

    @pl.when(tv_ref[i] > 0)
    def _():
        o_ref[...] = _dot(h_ref[...], wd_ref[...])

    @pl.when(tv_ref[i] == 0)
    def _():
        o_ref[...] = jnp.zeros_like(o_ref)


def moe_down(h, w_down, tile_e, tile_v, tn):
    n_rows, f = h.shape
    d = w_down.shape[2]
    nt = n_rows // MOE_TM
    grid_spec = pltpu.PrefetchScalarGridSpec(
        num_scalar_prefetch=2,
        grid=(d // tn, nt),
        in_specs=[pl.BlockSpec((MOE_TM, f), lambda j, i, te, tv: (i, 0)),
                  pl.BlockSpec((None, f, tn), lambda j, i, te, tv: (te[i], 0, j))],
        out_specs=pl.BlockSpec((MOE_TM, tn), lambda j, i, te, tv: (i, j)),
    )
    return pl.pallas_call(
        _moe_down_kernel,
        grid_spec=grid_spec,
        out_shape=jax.ShapeDtypeStruct((n_rows, d), F32),
        compiler_params=_cp(("parallel", "arbitrary")),
        name="moe_down",
    )(tile_e, tile_v, h, w_down)


COMBINE_TM = 128


def _moe_combine_kernel(d1_ref, d2_ref, y_hbm, x_ref, rw_ref, g_ref, b_ref, o_ref, buf, sem):
    nslab = buf.shape[2]

    def start(r, _):
        pltpu.make_async_copy(y_hbm.at[d1_ref[0, 0, r]], buf.at[0, r], sem).start()
        pltpu.make_async_copy(y_hbm.at[d2_ref[0, 0, r]], buf.at[1, r], sem).start()
        return 0

    lax.fori_loop(0, COMBINE_TM, start, 0)
    for k in range(2):
        pltpu.make_async_copy(y_hbm.at[pl.ds(0, COMBINE_TM)], buf.at[k], sem).wait()
    rw = rw_ref[...]
    w1 = rw[:, 2:3]
    w2 = rw[:, 3:4]
    z = []
    tot = jnp.zeros((COMBINE_TM, 1), F32)
    for c in range(nslab):
        sl = slice(c * LANES, (c + 1) * LANES)
        zc = ALPHA * x_ref[:, sl] + (w1 * buf[0, :, c, :] + w2 * buf[1, :, c, :])
        z.append(zc)
        tot = tot + jnp.sum(zc, axis=-1, keepdims=True)
    d = nslab * LANES
    mu = tot * (1.0 / d)
    ss = jnp.zeros((COMBINE_TM, 1), F32)
    for c in range(nslab):
        z[c] = z[c] - mu
        ss = ss + jnp.sum(z[c] * z[c], axis=-1, keepdims=True)
    r = lax.rsqrt(ss * (1.0 / d) + LN_EPS)
    for c in range(nslab):
        sl = slice(c * LANES, (c + 1) * LANES)
        o_ref[:, sl] = z[c] * r * g_ref[:, sl] + b_ref[:, sl]


def moe_combine(y_rows, x, route, dest1, dest2, g, beta):
    n, d = x.shape
    nt = n // COMBINE_TM
    y3 = _row_slabs(y_rows)
    idx_spec = pl.BlockSpec((1, 1, COMBINE_TM), lambda i: (i, 0, 0), memory_space=pltpu.SMEM)
    return pl.pallas_call(
        _moe_combine_kernel,
        grid=(nt,),
        in_specs=[idx_spec, idx_spec,
                  pl.BlockSpec(memory_space=pl.ANY),
                  pl.BlockSpec((COMBINE_TM, d), lambda i: (i, 0)),
                  pl.BlockSpec((COMBINE_TM, LANES), lambda i: (i, 0)),
                  pl.BlockSpec((1, d), lambda i: (0, 0)),
                  pl.BlockSpec((1, d), lambda i: (0, 0))],
        out_specs=pl.BlockSpec((COMBINE_TM, d), lambda i: (i, 0)),
        out_shape=jax.ShapeDtypeStruct((n, d), F32),
        scratch_shapes=[pltpu.VMEM((2, COMBINE_TM) + y3.shape[1:], F32), pltpu.SemaphoreType.DMA],
        compiler_params=_cp(("arbitrary",)),
        name="moe_combine",
    )(dest1.reshape(nt, 1, COMBINE_TM), dest2.reshape(nt, 1, COMBINE_TM), y3, x, route,
      g.reshape(1, d).astype(F32), beta.reshape(1, d).astype(F32))


def moe_layer(x, xb, w_router, b_router, w_gate, w_up, w_down, g, beta):
    n, d = x.shape
    route, cnt = moe_router(x, w_router, b_router)
    e1 = route[:, 0].astype(I32)
    e2 = route[:, 1].astype(I32)
    counts = cnt[0, :N_EXPERTS].astype(I32)
    padded = (counts + MOE_TM - 1) // MOE_TM * MOE_TM
    pad_end = jnp.cumsum(padded)
    pad_start = pad_end - padded
    dest1 = pad_start[e1] + route[:, 4].astype(I32)
    dest2 = pad_start[e2] + route[:, 5].astype(I32)
    n_rows = 2 * n + N_EXPERTS * MOE_TM
    nt = n_rows // MOE_TM
    tok = jnp.arange(n, dtype=I32)
    row_tok = jnp.zeros((n_rows,), I32).at[dest1].set(tok).at[dest2].set(tok)
    tile_start = jnp.arange(nt, dtype=I32) * MOE_TM
    tile_e = jnp.minimum(jnp.searchsorted(pad_end, tile_start, side='right'), N_EXPERTS - 1).astype(I32)
    tile_v = (tile_start < pad_end[-1]).astype(I32)
    xs = row_gather(xb, row_tok)
    h = moe_up(xs, w_gate, w_up, tile_e, tile_v, tn=min(512, w_gate.shape[2]))
    y_rows = moe_down(h, w_down, tile_e, tile_v, tn=min(512, d))
    return moe_combine(y_rows, x, route, dest1, dest2, g, beta)


def _cmp_sel_overlap(n_cmp_pad, n_sel):
    i = np.arange(n_cmp_pad)[:, None]
    j = np.arange(LANES)[None, :]
    lo = np.maximum(i * CMP_STRIDE, j * SEL_LEN)
    hi = np.minimum(i * CMP_STRIDE + CMP_LEN, (j + 1) * SEL_LEN)
    ov = np.maximum(hi - lo, 0) / CMP_LEN
    ov[:, n_sel:] = 0.0
    return ov.astype(np.float32)


def _gate_columns():
    src = -np.ones((KV_GROUPS * LANES,), np.int64)
    for g in range(KV_GROUPS):
        for j in range(HPG):
            for br in range(3):
                src[g * LANES + br * HPG + j] = g * HPG * 3 + j * 3 + br
    return src


def nsa_layer(xb, b, s, w_in, pe_k, w1_k, w2_k, pe_v, w1_v, w2_v, band, cmpb):
    n = xb.shape[0]
    hd = N_HEADS * HEAD_DIM
    gw = KV_GROUPS * HEAD_DIM
    w_main = w_in[:, :hd + 6 * gw].astype(BF)
    scale = jnp.concatenate([jnp.full((hd,), HEAD_DIM ** -0.5, F32), jnp.ones((6 * gw,), F32)])
    ybf = matmul_scaled(xb, w_main, scale, BF, tm=512, tn=512)
    src = _gate_columns()
    w_gl = jnp.where(jnp.asarray(src >= 0), w_in[:, hd + 6 * gw:][:, np.maximum(src, 0)], 0.0).astype(BF)
    gl = matmul_scaled(xb, w_gl, jnp.ones((w_gl.shape[1],), F32), F32, tm=512, tn=512)
    cb = hd // gw
    kcmp = nsa_compress(ybf, b, s, cb + 0, pe_k, w1_k, w2_k)
    vcmp = nsa_compress(ybf, b, s, cb + 1, pe_v, w1_v, w2_v)
    overlap = jnp.asarray(_cmp_sel_overlap(s // CMP_STRIDE, s // SEL_LEN)).astype(BF)
    oc, nsel = nsa_cmp_select(ybf, kcmp, vcmp, cmpb, overlap, b, s)
    c128 = hd // HEAD_DIM
    o = nsa_main(ybf, nsel, band, gl, oc, b, s,
                 col_ksl=c128 + 2 * KV_GROUPS, col_vsl=c128 + 3 * KV_GROUPS,
                 col_kw=c128 + 4 * KV_GROUPS, col_vw=c128 + 5 * KV_GROUPS)
    return o.reshape(n, hd)


def dsa_layer(xb, b, s, w_in, kv_norm_g, w_uk, w_uv, band):
    n = xb.shape[0]
    hd = N_HEADS * HEAD_DIM
    o_c = hd
    o_qi = hd + KV_RANK
    o_ki = o_qi + IDX_HEADS * IDX_DIM
    o_wi = o_ki + IDX_DIM
    w_q = jnp.concatenate([w_in[:, :hd], w_in[:, o_qi:o_ki]], axis=1).astype(BF)
    scale_q = jnp.concatenate([jnp.ones((hd,), F32), jnp.full((IDX_HEADS * IDX_DIM,), IDX_DIM ** -0.5, F32)])
    qbf = matmul_scaled(xb, w_q, scale_q, BF, tm=512, tn=512)
    cn = matmul_rmsnorm(xb, w_in[:, o_c:o_qi].astype(BF), kv_norm_g, tm=512)
    w_kw = jnp.pad(w_in[:, o_ki:], ((0, 0), (0, LANES - IDX_DIM - IDX_HEADS))).astype(BF)
    scale_kw = jnp.concatenate([jnp.ones((IDX_DIM,), F32), jnp.full((IDX_HEADS,), IDX_HEADS ** -0.5, F32),
                                jnp.zeros((LANES - IDX_DIM - IDX_HEADS,), F32)])
    kw = matmul_scaled(xb, w_kw, scale_kw, F32, tm=512, tn=LANES)
    nmask = dsa_index(qbf, kw, b, s, col_qi=hd // (IDX_HEADS * IDX_DIM))
    o = dsa_attention(qbf, cn, nmask, band[:, :2], w_uk, w_uv, b, s)
    return o.reshape(n, hd)


def kernel(x, rel_bias, nsa_w_in, nsa_cmp_pe_k, nsa_cmp_w1_k, nsa_cmp_w2_k, nsa_cmp_pe_v, nsa_cmp_w1_v, nsa_cmp_w2_v, nsa_w_out, dsa_w_in, dsa_kv_norm_g, dsa_w_uk, dsa_w_uv, dsa_w_out, ffn_w_gate, ffn_w_up, ffn_w_down, moe_w_router, moe_b_router, moe_w_gate, moe_w_up, moe_w_down, ln_mix_g, ln_mix_b, ln_ffn_g, ln_ffn_b):
    b, s, d = x.shape
    n = b * s
    assert s % TQ == 0 and s // SEL_LEN <= LANES
    x0 = x.reshape(n, d)
    band, cmpb = _bias_tables(rel_bias, s // CMP_STRIDE)
    o = nsa_layer(x0.astype(BF), b, s, nsa_w_in, nsa_cmp_pe_k, nsa_cmp_w1_k, nsa_cmp_w2_k,
                  nsa_cmp_pe_v, nsa_cmp_w1_v, nsa_cmp_w2_v, band, cmpb)
    x1, x1b = matmul_residual_ln(o, nsa_w_out.astype(BF), x0, ln_mix_g[0], ln_mix_b[0], tm=256, tk=d)
    hff = swiglu_up(x1b, ffn_w_gate.astype(BF), ffn_w_up.astype(BF), tm=512, tn=512)
    x2, x2b = matmul_residual_ln(hff, ffn_w_down.astype(BF), x1, ln_ffn_g[0], ln_ffn_b[0], tm=256, tk=512)
    o = dsa_layer(x2b, b, s, dsa_w_in, dsa_kv_norm_g, dsa_w_uk, dsa_w_uv, band)
    x3, x3b = matmul_residual_ln(o, dsa_w_out.astype(BF), x2, ln_mix_g[1], ln_mix_b[1], tm=256, tk=d)
    out = moe_layer(x3, x3b, moe_w_router, moe_b_router, moe_w_gate, moe_w_up, moe_w_down,
                    ln_ffn_g[1], ln_ffn_b[1])
    return out.reshape(b, s, d)
```

```python
import functools
import math

import numpy as np
import jax
import jax.numpy as jnp
from jax import lax
from jax.experimental import pallas as pl
from jax.experimental.pallas import tpu as pltpu

F32 = jnp.float32
BF = jnp.bfloat16
I32 = jnp.int32

N_HEADS = 16
HEAD_DIM = 128
KV_GROUPS = 4
HPG = N_HEADS // KV_GROUPS
CMP_LEN = 32
CMP_STRIDE = 16
SEL_LEN = 64
SEL_TOP = 16
WINDOW = 512
SEL_FORCE = 1e4
KV_RANK = 256
IDX_HEADS = 8
IDX_DIM = 64
IDX_TOPK = 256
REL_BUCKETS = 32
REL_MAX_DIST = 128
N_EXPERTS = 8
DEPTH = 2
ALPHA = (2 * DEPTH) ** 0.25
LN_EPS = 1e-5
RMS_EPS = 1e-6
NEG_INF = -1e30

BIG = float(2.0 ** 100)
M_INIT = -3.0e38
TQ = 256
LANES = 128
VMEM_LIMIT_BYTES = 60000 * 1024


def _cp(sem, vmem=None):
    return pltpu.CompilerParams(dimension_semantics=sem, vmem_limit_bytes=vmem or VMEM_LIMIT_BYTES)


def _dot(a, b):
    return jnp.dot(a.astype(BF), b.astype(BF), preferred_element_type=F32)


def _dot_nt(a, b):
    return lax.dot_general(a.astype(BF), b.astype(BF), (((1,), (1,)), ((), ())),
                           preferred_element_type=F32)


def _layer_norm_rows(z, g, b):
    mu = jnp.mean(z, axis=-1, keepdims=True)
    zc = z - mu
    var = jnp.mean(zc * zc, axis=-1, keepdims=True)
    return zc * lax.rsqrt(var + LN_EPS) * g + b


def _mm_scale_kernel(a_ref, b_ref, s_ref, o_ref):
    o_ref[...] = (_dot(a_ref[...], b_ref[...]) * s_ref[...]).astype(o_ref.dtype)


def matmul_scaled(a, b, scale, out_dtype, tm, tn):
    m, k = a.shape
    n = b.shape[1]
    return pl.pallas_call(
        _mm_scale_kernel,
        grid=(m // tm, n // tn),
        in_specs=[pl.BlockSpec((tm, k), lambda i, j: (i, 0)),
                  pl.BlockSpec((k, tn), lambda i, j: (0, j)),
                  pl.BlockSpec((1, tn), lambda i, j: (0, j))],
        out_specs=pl.BlockSpec((tm, tn), lambda i, j: (i, j)),
        out_shape=jax.ShapeDtypeStruct((m, n), out_dtype),
        compiler_params=_cp(("parallel", "parallel")),
        name="matmul_scaled",
    )(a, b, scale.reshape(1, n).astype(F32))


def _mm_rms_kernel(a_ref, b_ref, g_ref, o_ref):
    c = _dot(a_ref[...], b_ref[...])
    r = lax.rsqrt(jnp.mean(c * c, axis=-1, keepdims=True) + RMS_EPS)
    o_ref[...] = (c * r * g_ref[...]).astype(o_ref.dtype)


def matmul_rmsnorm(a, b, g, tm):
    m, k = a.shape
    n = b.shape[1]
    return pl.pallas_call(
        _mm_rms_kernel,
        grid=(m // tm,),
        in_specs=[pl.BlockSpec((tm, k), lambda i: (i, 0)),
                  pl.BlockSpec((k, n), lambda i: (0, 0)),
                  pl.BlockSpec((1, n), lambda i: (0, 0))],
        out_specs=pl.BlockSpec((tm, n), lambda i: (i, 0)),
        out_shape=jax.ShapeDtypeStruct((m, n), BF),
        compiler_params=_cp(("parallel",)),
        name="matmul_rmsnorm",
    )(a, b, g.reshape(1, n).astype(F32))


def _mm_ln_kernel(a_ref, b_ref, r_ref, g_ref, be_ref, of_ref, ob_ref, acc_ref, *, nk):
    kk = pl.program_id(1)

    @pl.when(kk == 0)
    def _():
        acc_ref[...] = jnp.zeros_like(acc_ref)

    acc_ref[...] += _dot(a_ref[...], b_ref[...])

    @pl.when(kk == nk - 1)
    def _():
        y = _layer_norm_rows(ALPHA * r_ref[...] + acc_ref[...], g_ref[...], be_ref[...])
        of_ref[...] = y
        ob_ref[...] = y.astype(BF)


def matmul_residual_ln(a, b, res, g, beta, tm, tk):
    m, k = a.shape
    n = b.shape[1]
    nk = k // tk
    return pl.pallas_call(
        functools.partial(_mm_ln_kernel, nk=nk),
        grid=(m // tm, nk),
        in_specs=[pl.BlockSpec((tm, tk), lambda i, kk: (i, kk)),
                  pl.BlockSpec((tk, n), lambda i, kk: (kk, 0)),
                  pl.BlockSpec((tm, n), lambda i, kk: (i, 0)),
                  pl.BlockSpec((1, n), lambda i, kk: (0, 0)),
                  pl.BlockSpec((1, n), lambda i, kk: (0, 0))],
        out_specs=[pl.BlockSpec((tm, n), lambda i, kk: (i, 0)),
                   pl.BlockSpec((tm, n), lambda i, kk: (i, 0))],
        out_shape=[jax.ShapeDtypeStruct((m, n), F32), jax.ShapeDtypeStruct((m, n), BF)],
        scratch_shapes=[pltpu.VMEM((tm, n), F32)],
        compiler_params=_cp(("parallel", "arbitrary")),
        name="matmul_residual_ln",
    )(a, b, res, g.reshape(1, n).astype(F32), beta.reshape(1, n).astype(F32))


def _silu(x):
    return x * (1.0 / (1.0 + jnp.exp(-x)))


def _swiglu_up_kernel(a_ref, wg_ref, wu_ref, o_ref):
    a = a_ref[...]
    o_ref[...] = (_silu(_dot(a, wg_ref[...])) * _dot(a, wu_ref[...])).astype(o_ref.dtype)


def swiglu_up(a, wg, wu, tm, tn):
    m, k = a.shape
    n = wg.shape[1]
    return pl.pallas_call(
        _swiglu_up_kernel,
        grid=(m // tm, n // tn),
        in_specs=[pl.BlockSpec((tm, k), lambda i, j: (i, 0)),
                  pl.BlockSpec((k, tn), lambda i, j: (0, j)),
                  pl.BlockSpec((k, tn), lambda i, j: (0, j))],
        out_specs=pl.BlockSpec((tm, tn), lambda i, j: (i, j)),
        out_shape=jax.ShapeDtypeStruct((m, n), BF),
        compiler_params=_cp(("parallel", "parallel")),
        name="swiglu_up",
    )(a, wg, wu)


def _bucket_of_distance():
    n = np.arange(REL_MAX_DIST + 1)
    exact = REL_BUCKETS // 2
    nf = np.maximum(n, exact).astype(np.float64)
    large = exact + (np.log(nf / exact) / math.log(REL_MAX_DIST / exact) * (REL_BUCKETS - exact)).astype(np.int64)
    return np.where(n < exact, n, np.minimum(large, REL_BUCKETS - 1)).astype(np.int32)


def _bucket_starts():
    bk = _bucket_of_distance()
    return [int(np.argmax(bk >= b)) for b in range(REL_BUCKETS)]


def _bias_kernel(tab_ref, band_ref, cmp_ref, *, starts):
    h = pl.program_id(0)
    far = tab_ref[REL_BUCKETS - 1, h]

    def lookup(d):
        val = jnp.zeros(d.shape, F32)
        for b in range(REL_BUCKETS - 2, -1, -1):
            val = jnp.where(d < starts[b + 1], tab_ref[b, h] - far, val)
        return val

    j = lax.broadcasted_iota(I32, (TQ, TQ), 0)
    i = lax.broadcasted_iota(I32, (TQ, TQ), 1)
    for r in range(3):
        d = TQ * r + i - j
        band_ref[r] = jnp.where((d < 0) | (d >= WINDOW), -BIG, lookup(d))
    c = lax.broadcasted_iota(I32, (CMP_NEAR, TQ), 0)
    i = lax.broadcasted_iota(I32, (CMP_NEAR, TQ), 1)
    d = i - CMP_STRIDE * (c - CMP_NEAR // 2) - (CMP_LEN - 1)
    cmp_ref[...] = jnp.where(d < 0, -BIG, lookup(d))


CMP_NEAR = 2 * (TQ // CMP_STRIDE)


def _bias_tables(rel_bias):
    return pl.pallas_call(
        functools.partial(_bias_kernel, starts=_bucket_starts()),
        grid=(N_HEADS,),
        in_specs=[pl.BlockSpec(memory_space=pltpu.SMEM)],
        out_specs=[pl.BlockSpec((None, 3, TQ, TQ), lambda h: (h // HPG, 0, 0, h % HPG)),
                   pl.BlockSpec((None, CMP_NEAR, TQ), lambda h: (h // HPG, 0, h % HPG))],
        out_shape=[jax.ShapeDtypeStruct((N_HEADS // HPG, 3, TQ, HPG * TQ), F32),
                   jax.ShapeDtypeStruct((N_HEADS // HPG, CMP_NEAR, HPG * TQ), F32)],
        compiler_params=_cp(("parallel",)),
        name="bias_tables",
    )(rel_bias.astype(F32))


def _heads_to_rows(q):
    return jnp.concatenate([q[:, j * HEAD_DIM:(j + 1) * HEAD_DIM] for j in range(HPG)], axis=0)


def _osm_update(carry, score_fn, vt, acc_ref):
    ms, ls = carry
    new_m, new_l = [], []
    for j in range(HPG):
        st = score_fn(j)
        m_new = jnp.maximum(ms[j], jnp.max(st, axis=0, keepdims=True))
        alpha = jnp.exp(ms[j] - m_new)
        p = jnp.exp(st - m_new)
        new_m.append(m_new)
        new_l.append(alpha * ls[j] + jnp.sum(p, axis=0, keepdims=True))
        acc_ref[j] = alpha * acc_ref[j] + jnp.dot(vt, p.astype(BF), preferred_element_type=F32)
    return tuple(new_m), tuple(new_l)


def _osm_init(acc_ref):
    acc_ref[...] = jnp.zeros_like(acc_ref)
    return ((jnp.full((1, TQ), M_INIT, F32),) * HPG, (jnp.zeros((1, TQ), F32),) * HPG)


def _store_transposed(src_ref, dst_ref, nchunk):
    def body(kc, _):
        start = pl.multiple_of(kc * TQ, TQ)
        dst_ref[kc] = src_ref[pl.ds(start, TQ), :].astype(F32).T.astype(dst_ref.dtype)
        return 0

    lax.fori_loop(0, nchunk, body, 0)


def _gelu_tanh(x):
    return 0.5 * x * (1.0 + jnp.tanh(math.sqrt(2.0 / math.pi) * (x + 0.044715 * (x * x * x))))


def _compress_kernel(*refs, nch):
    x_refs = refs[:CMP_STRIDE]
    pe_ref, w1_ref, w2_ref, o_ref = refs[CMP_STRIDE:]
    acc_a = [jnp.zeros((nch, HEAD_DIM), F32) for _ in range(KV_GROUPS)]
    acc_b = [jnp.zeros((nch, HEAD_DIM), F32) for _ in range(KV_GROUPS)]
    for l in range(CMP_STRIDE):
        x = x_refs[l][...].astype(F32)
        xa = (x + pe_ref[l:l + 1, :]).astype(BF)
        xb = (x + pe_ref[CMP_STRIDE + l:CMP_STRIDE + l + 1, :]).astype(BF)
        for g in range(KV_GROUPS):
            sl = slice(g * HEAD_DIM, (g + 1) * HEAD_DIM)
            acc_a[g] = acc_a[g] + jnp.dot(xa[:, sl], w1_ref[l], preferred_element_type=F32)
            acc_b[g] = acc_b[g] + jnp.dot(xb[:, sl], w1_ref[CMP_STRIDE + l], preferred_element_type=F32)
    for g in range(KV_GROUPS):
        pre = acc_a[g] + pltpu.roll(acc_b[g], nch - 1, axis=0)
        hid = _gelu_tanh(pre).astype(BF)
        o_ref[:, g * HEAD_DIM:(g + 1) * HEAD_DIM] = jnp.dot(hid, w2_ref[...], preferred_element_type=F32).astype(BF)


def nsa_compress(ybf, b, s, col_block, pe, w1, w2):
    ncols = ybf.shape[1]
    nch = s // CMP_STRIDE
    blk_w = KV_GROUPS * HEAD_DIM
    per_tok = ncols // blk_w
    y3 = ybf.reshape(b, nch, CMP_STRIDE * ncols)
    pe_t = jnp.tile(pe.astype(F32), (1, KV_GROUPS))
    in_specs = [pl.BlockSpec((None, nch, blk_w), (lambda bi, l=l: (bi, 0, l * per_tok + col_block)))
                for l in range(CMP_STRIDE)]
    in_specs += [pl.BlockSpec((CMP_LEN, blk_w), lambda bi: (0, 0)),
                 pl.BlockSpec((CMP_LEN, HEAD_DIM, HEAD_DIM), lambda bi: (0, 0, 0)),
                 pl.BlockSpec((HEAD_DIM, HEAD_DIM), lambda bi: (0, 0))]
    return pl.pallas_call(
        functools.partial(_compress_kernel, nch=nch),
        grid=(b,),
        in_specs=in_specs,
        out_specs=pl.BlockSpec((None, nch, blk_w), lambda bi: (bi, 0, 0)),
        out_shape=jax.ShapeDtypeStruct((b, nch, blk_w), BF),
        compiler_params=_cp(("parallel",)),
        name="nsa_compress",
    )(*([y3] * CMP_STRIDE), pe_t, w1.astype(BF), w2.astype(BF))


def _nsa_cmp_kernel(q_ref, kc_ref, vc_ref, cb_ref, ovt_ref, oc_ref, ns_ref, vct_ref, s_ref, *, n_sel, n_top):
    qt = pl.program_id(2)
    t0 = qt * TQ
    ncp = kc_ref.shape[0]
    pad = CMP_NEAR // 2

    @pl.when(qt == 0)
    def _():
        vct_ref[...] = vc_ref[...].astype(F32).T.astype(BF)
        s_ref[0:pad, :] = jnp.zeros((pad, HPG * TQ), F32)

    qs = _heads_to_rows(q_ref[...])
    s_ref[pad:pad + ncp, :] = _dot_nt(kc_ref[...], qs)
    near = pl.multiple_of(qt * (TQ // CMP_STRIDE), TQ // CMP_STRIDE)
    s_ref[pl.ds(near, CMP_NEAR), :] = s_ref[pl.ds(near, CMP_NEAR), :] + cb_ref[...]
    s = s_ref[pad:pad + ncp, :]
    key = lax.broadcasted_iota(I32, (ncp, HPG * TQ), 0)
    tq = t0 + (lax.broadcasted_iota(I32, (ncp, HPG * TQ), 1) & (TQ - 1))
    vis = (key * CMP_STRIDE + (CMP_LEN - 1)) <= tq
    s = jnp.where(vis, s, -BIG)
    m = jnp.max(s, axis=0, keepdims=True)
    p = jnp.where(vis, jnp.exp(s - m), 0.0)
    l = jnp.sum(p, axis=0, keepdims=True)
    p = p * jnp.where(l > 0.0, 1.0 / l, 0.0)
    pb = p.astype(BF)
    oct = jnp.dot(vct_ref[...], pb, preferred_element_type=F32)
    score = jnp.zeros((LANES, TQ), F32)
    for j in range(HPG):
        cs = slice(j * TQ, (j + 1) * TQ)
        oc_ref[:, j * HEAD_DIM:(j + 1) * HEAD_DIM] = oct[:, cs].T.astype(oc_ref.dtype)
        score = score + jnp.dot(ovt_ref[...], pb[:, cs], preferred_element_type=F32)
    blk = lax.broadcasted_iota(I32, (LANES, TQ), 0)
    t = t0 + lax.broadcasted_iota(I32, (LANES, TQ), 1)
    cur = lax.shift_right_logical(t, int(math.log2(SEL_LEN)))
    forced = (blk == 0) | (blk == cur) | (blk == cur - 1)
    visible = blk * SEL_LEN <= t
    sc = jnp.where(forced, SEL_FORCE, jnp.where(visible, score, -1.0))
    sc = jnp.where(blk < n_sel, sc, -2.0)
    blkf = blk.astype(F32)

    def pick_one(_, carry):
        sc, sel = carry
        mx = jnp.max(sc, axis=0, keepdims=True)
        first = jnp.min(jnp.where(sc == mx, blkf, float(LANES)), axis=0, keepdims=True)
        hit = blkf == first
        return jnp.where(hit, -3.0, sc), jnp.where(hit, 1.0, sel)

    _, sel = lax.fori_loop(0, n_top, pick_one, (sc, jnp.zeros((LANES, TQ), F32)))
    ns_ref[...] = jnp.where(sel.T > 0.5, 0.0, BIG).astype(BF)


def nsa_cmp_select(ybf, kcmp, vcmp, cmpb, overlap_t, b, s):
    ncp = kcmp.shape[1]
    n_sel = s // SEL_LEN
    n_top = min(SEL_TOP, n_sel)
    y3 = ybf.reshape(b, s, ybf.shape[1])
    gw = HPG * HEAD_DIM
    return pl.pallas_call(
        functools.partial(_nsa_cmp_kernel, n_sel=n_sel, n_top=n_top),
        grid=(b, KV_GROUPS, s // TQ),
        in_specs=[pl.BlockSpec((None, TQ, gw), lambda bi, g, qt: (bi, qt, g)),
                  pl.BlockSpec((None, ncp, HEAD_DIM), lambda bi, g, qt: (bi, 0, g)),
                  pl.BlockSpec((None, ncp, HEAD_DIM), lambda bi, g, qt: (bi, 0, g)),
                  pl.BlockSpec((None, CMP_NEAR, HPG * TQ), lambda bi, g, qt: (g, 0, 0)),
                  pl.BlockSpec((LANES, ncp), lambda bi, g, qt: (0, 0))],
        out_specs=[pl.BlockSpec((None, TQ, gw), lambda bi, g, qt: (bi, qt, g)),
                   pl.BlockSpec((None, None, TQ, LANES), lambda bi, g, qt: (bi, g, qt, 0))],
        out_shape=[jax.ShapeDtypeStruct((b, s, KV_GROUPS * gw), BF),
                   jax.ShapeDtypeStruct((b, KV_GROUPS, s, LANES), BF)],
        scratch_shapes=[pltpu.VMEM((HEAD_DIM, ncp), BF),
                        pltpu.VMEM((ncp + CMP_NEAR, HPG * TQ), F32)],
        compiler_params=_cp(("parallel", "parallel", "arbitrary")),
        name="nsa_cmp_select",
    )(y3, kcmp, vcmp, cmpb, overlap_t)


def _nsa_main_kernel(q_ref, ks_ref, vs_ref, kw_ref, vw_ref, ns_ref, band_ref, gl_ref, oc_ref, o_ref,
                     vst_ref, vwt_ref, accs_ref, accw_ref, *, nchunk):
    qt = pl.program_id(2)

    @pl.when(qt == 0)
    def _():
        _store_transposed(vs_ref, vst_ref, nchunk)
        _store_transposed(vw_ref, vwt_ref, nchunk)

    q = q_ref[...]
    ns = ns_ref[...]
    qs = _heads_to_rows(q)
    qp = jnp.concatenate([qs, jnp.concatenate([ns] * HPG, axis=0)], axis=1)
    krow = lax.broadcasted_iota(I32, (TQ, LANES), 0)
    klane = lax.broadcasted_iota(I32, (TQ, LANES), 1)
    kblk = lax.shift_right_logical(krow, int(math.log2(SEL_LEN)))

    def head_cols(st, j):
        return st[:, j * TQ:(j + 1) * TQ]

    def sel_scores(kc):
        start = pl.multiple_of(kc * TQ, TQ)
        k = ks_ref[pl.ds(start, TQ), :]
        oh = jnp.where(klane == kblk + kc * (TQ // SEL_LEN), -1.0, 0.0).astype(BF)
        return _dot_nt(jnp.concatenate([k, oh], axis=1), qp)

    def far_chunk(kc, carry):
        ms, ls, st = carry
        st_next = sel_scores(kc + 1)
        ms, ls = _osm_update((ms, ls), lambda j: head_cols(st, j), vst_ref[kc], accs_ref)
        return ms, ls, st_next

    nfar = jnp.maximum(qt - 1, 0)
    ms, ls, st = lax.fori_loop(0, nfar, far_chunk, _osm_init(accs_ref) + (sel_scores(0),))
    st_last = sel_scores(qt) + band_ref[0]
    st = st + jnp.where(qt >= 1, band_ref[1], -BIG)
    carry = _osm_update((ms, ls), lambda j: head_cols(st, j), vst_ref[nfar], accs_ref)
    m_s, l_s = _osm_update(carry, lambda j: head_cols(st_last, j), vst_ref[qt], accs_ref)

    def win_chunk(r, carry):
        kc = jnp.maximum(qt - r, 0)
        start = pl.multiple_of(kc * TQ, TQ)
        st = _dot_nt(kw_ref[pl.ds(start, TQ), :], qs) + jnp.where(qt >= r, band_ref[r], -BIG)
        return _osm_update(carry, lambda j: head_cols(st, j), vwt_ref[kc], accw_ref)

    carry = _osm_init(accw_ref)
    for r in (2, 1, 0):
        carry = win_chunk(r, carry)
    m_w, l_w = carry

    gates = 1.0 / (1.0 + jnp.exp(-gl_ref[...]))
    oc = oc_ref[...].astype(F32)
    for j in range(HPG):
        sl = slice(j * HEAD_DIM, (j + 1) * HEAD_DIM)
        o_s = (accs_ref[j] * (1.0 / l_s[j])).T
        o_w = (accw_ref[j] * (1.0 / l_w[j])).T
        o = (gates[:, j:j + 1] * oc[:, sl] + gates[:, HPG + j:HPG + j + 1] * o_s
             + gates[:, 2 * HPG + j:2 * HPG + j + 1] * o_w)
        o_ref[:, sl] = o.astype(o_ref.dtype)


def nsa_main(ybf, nsel, band, gl, oc, b, s, col_ksl, col_vsl, col_kw, col_vw):
    y3 = ybf.reshape(b, s, ybf.shape[1])
    gw = HPG * HEAD_DIM
    kv_spec = lambda cb: pl.BlockSpec((None, s, HEAD_DIM), lambda bi, g, qt: (bi, 0, cb + g))
    nchunk = s // TQ
    return pl.pallas_call(
        functools.partial(_nsa_main_kernel, nchunk=nchunk),
        grid=(b, KV_GROUPS, nchunk),
        in_specs=[pl.BlockSpec((None, TQ, gw), lambda bi, g, qt: (bi, qt, g)),
                  kv_spec(col_ksl), kv_spec(col_vsl), kv_spec(col_kw), kv_spec(col_vw),
                  pl.BlockSpec((None, None, TQ, LANES), lambda bi, g, qt: (bi, g, qt, 0)),
                  pl.BlockSpec((None, 3, TQ, HPG * TQ), lambda bi, g, qt: (g, 0, 0, 0)),
                  pl.BlockSpec((None, TQ, LANES), lambda bi, g, qt: (bi, qt, g)),
                  pl.BlockSpec((None, TQ, gw), lambda bi, g, qt: (bi, qt, g))],
        out_specs=pl.BlockSpec((None, TQ, gw), lambda bi, g, qt: (bi, qt, g)),
        out_shape=jax.ShapeDtypeStruct((b, s, KV_GROUPS * gw), BF),
        scratch_shapes=[pltpu.VMEM((nchunk, HEAD_DIM, TQ), BF), pltpu.VMEM((nchunk, HEAD_DIM, TQ), BF),
                        pltpu.VMEM((HPG, HEAD_DIM, TQ), F32), pltpu.VMEM((HPG, HEAD_DIM, TQ), F32)],
        compiler_params=_cp(("parallel", "parallel", "arbitrary")),
        name="nsa_main",
    )(y3, y3, y3, y3, y3, nsel, band, gl.reshape(b, s, gl.shape[1]), oc)


def _dsa_index_kernel(qi_ref, kw_ref, wq_ref, nm_ref, ka_ref, kb_ref, key_ref, *, k_sel, nchunk, idx_bits):
    qt = pl.program_id(1)
    t0 = qt * TQ
    half = LANES // 2

    @pl.when(qt == 0)
    def _():
        lane = lax.broadcasted_iota(I32, kw_ref.shape, 1)
        ka = jnp.where(lane < IDX_DIM, kw_ref[...], 0.0)
        ka_ref[...] = ka.astype(BF)
        kb_ref[...] = pltpu.roll(ka, half, axis=1).astype(BF)

    qi = qi_ref[...]
    npair = IDX_HEADS // 2
    lq = jnp.concatenate([qi[:, p * LANES:(p + 1) * LANES] for p in range(npair)], axis=0)
    wt = wq_ref[...].T
    w_r = [wt[IDX_DIM + h:IDX_DIM + h + 1, :] for h in range(IDX_HEADS)]
    kpos = lax.broadcasted_iota(I32, (TQ, TQ), 0)
    qpos = t0 + lax.broadcasted_iota(I32, (TQ, TQ), 1)

    def score_chunk(kc, _):
        start = pl.multiple_of(kc * TQ, TQ)
        da = _dot_nt(ka_ref[pl.ds(start, TQ), :], lq)
        db = _dot_nt(kb_ref[pl.ds(start, TQ), :], lq)
        sc = jnp.zeros((TQ, TQ), F32)
        for p in range(npair):
            cs = slice(p * TQ, (p + 1) * TQ)
            sc = sc + w_r[2 * p] * jnp.maximum(da[:, cs], 0.0) + w_r[2 * p + 1] * jnp.maximum(db[:, cs], 0.0)
        sc = jnp.where(sc == 0.0, 0.0, sc)
        sc = jnp.where(kc * TQ + kpos <= qpos, sc, NEG_INF)
        bits = lax.bitcast_convert_type(sc, I32)
        key_ref[kc] = jnp.where(bits < 0, bits ^ 0x7FFFFFFF, bits)
        return 0

    nproc = qt + 1
    lax.fori_loop(0, nproc, score_chunk, 0)

    def count(pred):
        def body(kc, acc):
            return acc + jnp.sum(jnp.where(pred(key_ref[kc], kc), 1.0, 0.0), axis=0, keepdims=True)
        return lax.fori_loop(0, nproc, body, jnp.zeros((1, TQ), F32))

    def count_ge(cand):
        return count(lambda k, kc: k >= cand)

    kf = float(k_sel)
    thr0 = jnp.full((1, TQ), -2 ** 31, I32)
    done0 = jnp.where(count_ge(thr0) == kf, 1.0, 0.0)

    def search_cond(state):
        i, _, done = state
        return (i < 32) & (jnp.min(done) < 0.5)

    def search_body(state):
        i, thr, done = state
        cand = thr ^ lax.shift_left(jnp.int32(1), jnp.int32(31) - i)
        cnt = count_ge(cand)
        take = (cnt >= kf) & (done < 0.5)
        thr = jnp.where(take, cand, thr)
        done = jnp.where(take & (cnt == kf), 1.0, done)
        return i + 1, thr, done

    _, thr, done = lax.while_loop(search_cond, search_body, (jnp.int32(0), thr0, done0))

    def tie_break(_):
        need = kf - count_ge(thr + 1)

        def ties_below(bound):
            return count(lambda k, kc: (k == thr) & (kc * TQ + kpos < bound))

        def idx_bit(i, jm):
            cand = jm | lax.shift_left(jnp.int32(1), jnp.int32(idx_bits - 1) - i)
            return jnp.where(ties_below(cand) < need, cand, jm)

        jm = lax.fori_loop(0, idx_bits, idx_bit, jnp.zeros((1, TQ), I32))
        return jnp.where(done > 0.5, jnp.int32(2 ** 30), jm)

    jm = lax.cond(jnp.min(done) < 0.5, tie_break, lambda _: jnp.full((1, TQ), 2 ** 30, I32), 0)

    def write_chunk(kc, _):
        k = key_ref[kc]
        kidx = kc * TQ + kpos
        sel = (kidx <= qpos) & ((k > thr) | ((k == thr) & (kidx <= jm)))
        nm_ref[kc] = jnp.where(sel, 0.0, -BIG).astype(BF)
        return 0

    lax.fori_loop(0, nproc, write_chunk, 0)

    def fill_chunk(kc, _):
        nm_ref[kc] = jnp.full((TQ, TQ), -BIG, BF)
        return 0

    lax.fori_loop(nproc, nchunk, fill_chunk, 0)


def dsa_index(qbf, kw, b, s, col_qi):
    nchunk = s // TQ
    k_sel = min(IDX_TOPK, s // 4)
    q3 = qbf.reshape(b, s, qbf.shape[1])
    kw3 = kw.reshape(b, s, LANES)
    return pl.pallas_call(
        functools.partial(_dsa_index_kernel, k_sel=k_sel, nchunk=nchunk, idx_bits=int(math.log2(s))),
        grid=(b, nchunk),
        in_specs=[pl.BlockSpec((None, TQ, IDX_HEADS * IDX_DIM), lambda bi, qt: (bi, qt, col_qi)),
                  pl.BlockSpec((None, s, LANES), lambda bi, qt: (bi, 0, 0)),
                  pl.BlockSpec((None, TQ, LANES), lambda bi, qt: (bi, qt, 0))],
        out_specs=pl.BlockSpec((None, None, nchunk, TQ, TQ), lambda bi, qt: (bi, qt, 0, 0, 0)),
        out_shape=jax.ShapeDtypeStruct((b, nchunk, nchunk, TQ, TQ), BF),
        scratch_shapes=[pltpu.VMEM((s, LANES), BF), pltpu.VMEM((s, LANES), BF),
                        pltpu.VMEM((nchunk, TQ, TQ), I32)],
        compiler_params=_cp(("parallel", "arbitrary")),
        name="dsa_index",
    )(q3, kw3, kw3)


def _dsa_attn_kernel(q_ref, c_ref, nm_ref, band_ref, wuk_ref, wuvt_ref, o_ref, ct_ref, acc_ref, *, nchunk):
    qt = pl.program_id(1)
    hg = pl.program_id(2)

    @pl.when((qt == 0) & (hg == 0))
    def _():
        _store_transposed(c_ref, ct_ref, nchunk)

    q = q_ref[...]
    ql = jnp.concatenate(
        [(jnp.dot(q[:, j * HEAD_DIM:(j + 1) * HEAD_DIM], wuk_ref[j], preferred_element_type=F32)
          * (HEAD_DIM ** -0.5)).astype(BF) for j in range(HPG)], axis=0)

    def scores(kc):
        start = pl.multiple_of(kc * TQ, TQ)
        return _dot_nt(c_ref[pl.ds(start, TQ), :], ql)

    def masked(st, kc, bias=None):
        nm = nm_ref[kc].astype(F32)
        if bias is None:
            return lambda j: st[:, j * TQ:(j + 1) * TQ] + nm
        return lambda j: st[:, j * TQ:(j + 1) * TQ] + (nm + bias[:, j * TQ:(j + 1) * TQ])

    def far_chunk(kc, carry):
        ms, ls, st = carry
        st_next = scores(kc + 1)
        ms, ls = _osm_update((ms, ls), masked(st, kc), ct_ref[kc], acc_ref)
        return ms, ls, st_next

    nfar = jnp.maximum(qt - 1, 0)
    ms, ls, st = lax.fori_loop(0, nfar, far_chunk, _osm_init(acc_ref) + (scores(0),))
    st_last = scores(qt)
    carry = _osm_update((ms, ls), masked(st, nfar, jnp.where(qt >= 1, band_ref[1], -BIG)), ct_ref[nfar], acc_ref)
    m, l = _osm_update(carry, masked(st_last, qt, band_ref[0]), ct_ref[qt], acc_ref)
    for j in range(HPG):
        o_lat = (acc_ref[j] * (1.0 / l[j])).astype(BF)
        ot = jnp.dot(wuvt_ref[j], o_lat, preferred_element_type=F32)
        o_ref[:, j * HEAD_DIM:(j + 1) * HEAD_DIM] = ot.T.astype(o_ref.dtype)


def dsa_attention(qbf, cn, nmask, band, w_uk, w_uv, b, s):
    nchunk = s // TQ
    q3 = qbf.reshape(b, s, qbf.shape[1])
    c3 = cn.reshape(b, s, KV_RANK)
    gw = HPG * HEAD_DIM
    return pl.pallas_call(
        functools.partial(_dsa_attn_kernel, nchunk=nchunk),
        grid=(b, nchunk, N_HEADS // HPG),
        in_specs=[pl.BlockSpec((None, TQ, gw), lambda bi, qt, hg: (bi, qt, hg)),
                  pl.BlockSpec((None, s, KV_RANK), lambda bi, qt, hg: (bi, 0, 0)),
                  pl.BlockSpec((None, None, nchunk, TQ, TQ), lambda bi, qt, hg: (bi, qt, 0, 0, 0)),
                  pl.BlockSpec((None, 2, TQ, HPG * TQ), lambda bi, qt, hg: (hg, 0, 0, 0)),
                  pl.BlockSpec((HPG, HEAD_DIM, KV_RANK), lambda bi, qt, hg: (hg, 0, 0)),
                  pl.BlockSpec((HPG, HEAD_DIM, KV_RANK), lambda bi, qt, hg: (hg, 0, 0))],
        out_specs=pl.BlockSpec((None, TQ, gw), lambda bi, qt, hg: (bi, qt, hg)),
        out_shape=jax.ShapeDtypeStruct((b, s, N_HEADS * HEAD_DIM), BF),
        scratch_shapes=[pltpu.VMEM((nchunk, KV_RANK, TQ), BF), pltpu.VMEM((HPG, KV_RANK, TQ), F32)],
        compiler_params=_cp(("parallel", "arbitrary", "arbitrary")),
        name="dsa_attention",
    )(q3, c3, nmask, band, w_uk.astype(BF), jnp.swapaxes(w_uv, 1, 2).astype(BF))


ROUTER_TM = 256
MOE_TM = 512


def _router_kernel(x_ref, w_ref, b_ref, o_ref, cnt_ref, carry_ref):
    i = pl.program_id(0)

    @pl.when(i == 0)
    def _():
        carry_ref[...] = jnp.zeros_like(carry_ref)

    tm = x_ref.shape[0]
    logits = jnp.dot(x_ref[...], w_ref[...], preferred_element_type=F32,
                     precision=lax.Precision.HIGHEST) + b_ref[...]
    lane = lax.broadcasted_iota(I32, (tm, LANES), 1)
    lanef = lane.astype(F32)
    lg = jnp.where(lane < N_EXPERTS, logits, -BIG)
    m1 = jnp.max(lg, axis=-1, keepdims=True)
    i1 = jnp.min(jnp.where(lg == m1, lanef, float(LANES)), axis=-1, keepdims=True)
    lg2 = jnp.where(lanef == i1, -BIG, lg)
    m2 = jnp.max(lg2, axis=-1, keepdims=True)
    i2 = jnp.min(jnp.where(lg2 == m2, lanef, float(LANES)), axis=-1, keepdims=True)
    e2 = jnp.exp(m2 - m1)
    den = 1.0 + e2
    w1 = 1.0 / den
    w2 = e2 / den
    hit1 = lanef == i1
    hit2 = lanef == i2
    onehot = jnp.where(hit1 | hit2, 1.0, 0.0)
    r = lax.broadcasted_iota(I32, (tm, tm), 0)
    c = lax.broadcasted_iota(I32, (tm, tm), 1)
    tri = jnp.where(c < r, 1.0, 0.0).astype(BF)
    before = jnp.dot(tri, onehot.astype(BF), preferred_element_type=F32) + carry_ref[...]
    rank1 = jnp.sum(jnp.where(hit1, before, 0.0), axis=-1, keepdims=True)
    rank2 = jnp.sum(jnp.where(hit2, before, 0.0), axis=-1, keepdims=True)
    carry_ref[...] = carry_ref[...] + jnp.sum(onehot, axis=0, keepdims=True)
    vals = (i1, i2, w1, w2, rank1, rank2)
    out = jnp.zeros((tm, LANES), F32)
    for k, v in enumerate(vals):
        out = jnp.where(lane == k, v, out)
    o_ref[...] = out
    cnt_ref[...] = jnp.broadcast_to(carry_ref[...], cnt_ref.shape)


def moe_router(x, w_router, b_router):
    n, d = x.shape
    wp = jnp.pad(w_router.astype(F32), ((0, 0), (0, LANES - N_EXPERTS)))
    bp = jnp.pad(b_router.astype(F32), (0, LANES - N_EXPERTS)).reshape(1, LANES)
    return pl.pallas_call(
        _router_kernel,
        grid=(n // ROUTER_TM,),
        in_specs=[pl.BlockSpec((ROUTER_TM, d), lambda i: (i, 0)),
                  pl.BlockSpec((d, LANES), lambda i: (0, 0)),
                  pl.BlockSpec((1, LANES), lambda i: (0, 0))],
        out_specs=[pl.BlockSpec((ROUTER_TM, LANES), lambda i: (i, 0)),
                   pl.BlockSpec((8, LANES), lambda i: (0, 0))],
        out_shape=[jax.ShapeDtypeStruct((n, LANES), F32), jax.ShapeDtypeStruct((8, LANES), F32)],
        scratch_shapes=[pltpu.VMEM((1, LANES), F32)],
        compiler_params=_cp(("arbitrary",)),
        name="moe_router",
    )(x, wp, bp)


GATHER_ROWS = 512


def _row_slabs(x):
    return x.reshape(x.shape[0], x.shape[1] // LANES, LANES)


def _row_gather_kernel(idx_ref, x_hbm, o_ref, sem):
    def start(r, _):
        pltpu.make_async_copy(x_hbm.at[idx_ref[0, 0, r]], o_ref.at[r], sem).start()
        return 0

    lax.fori_loop(0, GATHER_ROWS, start, 0)
    pltpu.make_async_copy(x_hbm.at[pl.ds(0, GATHER_ROWS)], o_ref, sem).wait()


def row_gather(x, row_idx):
    n_rows = row_idx.shape[0]
    nt = n_rows // GATHER_ROWS
    x3 = _row_slabs(x)
    out = pl.pallas_call(
        _row_gather_kernel,
        grid=(nt,),
        in_specs=[pl.BlockSpec((1, 1, GATHER_ROWS), lambda i: (i, 0, 0), memory_space=pltpu.SMEM),
                  pl.BlockSpec(memory_space=pl.ANY)],
        out_specs=pl.BlockSpec((GATHER_ROWS,) + x3.shape[1:], lambda i: (i, 0, 0)),
        out_shape=jax.ShapeDtypeStruct((n_rows,) + x3.shape[1:], x.dtype),
        scratch_shapes=[pltpu.SemaphoreType.DMA],
        compiler_params=_cp(("arbitrary",)),
        name="moe_row_gather",
    )(row_idx.reshape(nt, 1, GATHER_ROWS), x3)
    return out.reshape(n_rows, x.shape[1])


def _moe_up_kernel(te_ref, tv_ref, x_ref, wg_ref, wu_ref, o_ref):
    i = pl.program_id(1)

    @pl.when(tv_ref[i] > 0)
    def _():
        x = x_ref[...]
        o_ref[...] = (_silu(_dot(x, wg_ref[...])) * _dot(x, wu_ref[...])).astype(o_ref.dtype)

    @pl.when(tv_ref[i] == 0)
    def _():
        o_ref[...] = jnp.zeros_like(o_ref)


def moe_up(xs, w_gate, w_up, tile_e, tile_v, tn):
    n_rows, d = xs.shape
    f = w_gate.shape[2]
    nt = n_rows // MOE_TM
    grid_spec = pltpu.PrefetchScalarGridSpec(
        num_scalar_prefetch=2,
        grid=(f // tn, nt),
        in_specs=[pl.BlockSpec((MOE_TM, d), lambda j, i, te, tv: (i, 0)),
                  pl.BlockSpec((None, d, tn), lambda j, i, te, tv: (te[i], 0, j)),
                  pl.BlockSpec((None, d, tn), lambda j, i, te, tv: (te[i], 0, j))],
        out_specs=pl.BlockSpec((MOE_TM, tn), lambda j, i, te, tv: (i, j)),
    )
    return pl.pallas_call(
        _moe_up_kernel,
        grid_spec=grid_spec,
        out_shape=jax.ShapeDtypeStruct((n_rows, f), BF),
        compiler_params=_cp(("parallel", "arbitrary")),
        name="moe_up",
    )(tile_e, tile_v, xs, w_gate, w_up)


def _moe_down_kernel(te_ref, tv_ref, h_ref, wd_ref, o_ref):
    i = pl.program_id(1)

    @pl.when(tv_ref[i] > 0)
    def _():
        o_ref[...] = _dot(h_ref[...], wd_ref[...])

    @pl.when(tv_ref[i] == 0)
    def _():
        o_ref[...] = jnp.zeros_like(o_ref)


def moe_down(h, w_down, tile_e, tile_v, tn):
    n_rows, f = h.shape
    d = w_down.shape[2]
    nt = n_rows // MOE_TM
    grid_spec = pltpu.PrefetchScalarGridSpec(
        num_scalar_prefetch=2,
        grid=(d // tn, nt),
        in_specs=[pl.BlockSpec((MOE_TM, f), lambda j, i, te, tv: (i, 0)),
                  pl.BlockSpec((None, f, tn), lambda j, i, te, tv: (te[i], 0, j))],
        out_specs=pl.BlockSpec((MOE_TM, tn), lambda j, i, te, tv: (i, j)),
    )
    return pl.pallas_call(
        _moe_down_kernel,
        grid_spec=grid_spec,
        out_shape=jax.ShapeDtypeStruct((n_rows, d), F32),
        compiler_params=_cp(("parallel", "arbitrary")),
        name="moe_down",
    )(tile_e, tile_v, h, w_down)


COMBINE_TM = 128


def _moe_combine_kernel(d1_ref, d2_ref, y_hbm, x_ref, rw_ref, g_ref, b_ref, o_ref, buf, sem):
    nslab = buf.shape[2]

    def start(r, _):
        pltpu.make_async_copy(y_hbm.at[d1_ref[0, 0, r]], buf.at[0, r], sem).start()
        pltpu.make_async_copy(y_hbm.at[d2_ref[0, 0, r]], buf.at[1, r], sem).start()
        return 0

    lax.fori_loop(0, COMBINE_TM, start, 0)
    for k in range(2):
        pltpu.make_async_copy(y_hbm.at[pl.ds(0, COMBINE_TM)], buf.at[k], sem).wait()
    rw = rw_ref[...]
    w1 = rw[:, 2:3]
    w2 = rw[:, 3:4]
    z = []
    tot = jnp.zeros((COMBINE_TM, 1), F32)
    for c in range(nslab):
        sl = slice(c * LANES, (c + 1) * LANES)
        zc = ALPHA * x_ref[:, sl] + (w1 * buf[0, :, c, :] + w2 * buf[1, :, c, :])
        z.append(zc)
        tot = tot + jnp.sum(zc, axis=-1, keepdims=True)
    d = nslab * LANES
    mu = tot * (1.0 / d)
    ss = jnp.zeros((COMBINE_TM, 1), F32)
    for c in range(nslab):
        z[c] = z[c] - mu
        ss = ss + jnp.sum(z[c] * z[c], axis=-1, keepdims=True)
    r = lax.rsqrt(ss * (1.0 / d) + LN_EPS)
    for c in range(nslab):
        sl = slice(c * LANES, (c + 1) * LANES)
        o_ref[:, sl] = z[c] * r * g_ref[:, sl] + b_ref[:, sl]


def moe_combine(y_rows, x, route, dest1, dest2, g, beta):
    n, d = x.shape
    nt = n // COMBINE_TM
    y3 = _row_slabs(y_rows)
    idx_spec = pl.BlockSpec((1, 1, COMBINE_TM), lambda i: (i, 0, 0), memory_space=pltpu.SMEM)
    return pl.pallas_call(
        _moe_combine_kernel,
        grid=(nt,),
        in_specs=[idx_spec, idx_spec,
                  pl.BlockSpec(memory_space=pl.ANY),
                  pl.BlockSpec((COMBINE_TM, d), lambda i: (i, 0)),
                  pl.BlockSpec((COMBINE_TM, LANES), lambda i: (i, 0)),
                  pl.BlockSpec((1, d), lambda i: (0, 0)),
                  pl.BlockSpec((1, d), lambda i: (0, 0))],
        out_specs=pl.BlockSpec((COMBINE_TM, d), lambda i: (i, 0)),
        out_shape=jax.ShapeDtypeStruct((n, d), F32),
        scratch_shapes=[pltpu.VMEM((2, COMBINE_TM) + y3.shape[1:], F32), pltpu.SemaphoreType.DMA],
        compiler_params=_cp(("arbitrary",)),
        name="moe_combine",
    )(dest1.reshape(nt, 1, COMBINE_TM), dest2.reshape(nt, 1, COMBINE_TM), y3, x, route,
      g.reshape(1, d).astype(F32), beta.reshape(1, d).astype(F32))


def moe_layer(x, xb, w_router, b_router, w_gate, w_up, w_down, g, beta):
    n, d = x.shape
    route, cnt = moe_router(x, w_router, b_router)
    e1 = route[:, 0].astype(I32)
    e2 = route[:, 1].astype(I32)
    counts = cnt[0, :N_EXPERTS].astype(I32)
    padded = (counts + MOE_TM - 1) // MOE_TM * MOE_TM
    pad_end = jnp.cumsum(padded)
    pad_start = pad_end - padded
    dest1 = pad_start[e1] + route[:, 4].astype(I32)
    dest2 = pad_start[e2] + route[:, 5].astype(I32)
    n_rows = 2 * n + N_EXPERTS * MOE_TM
    nt = n_rows // MOE_TM
    tok = jnp.arange(n, dtype=I32)
    row_tok = jnp.zeros((n_rows,), I32).at[dest1].set(tok).at[dest2].set(tok)
    tile_start = jnp.arange(nt, dtype=I32) * MOE_TM
    tile_e = jnp.minimum(jnp.searchsorted(pad_end, tile_start, side='right'), N_EXPERTS - 1).astype(I32)
    tile_v = (tile_start < pad_end[-1]).astype(I32)
    xs = row_gather(xb, row_tok)
    h = moe_up(xs, w_gate, w_up, tile_e, tile_v, tn=min(1024, w_gate.shape[2]))
    y_rows = moe_down(h, w_down, tile_e, tile_v, tn=min(512, d))
    return moe_combine(y_rows, x, route, dest1, dest2, g, beta)


def _cmp_sel_overlap(n_cmp_pad, n_sel):
    i = np.arange(n_cmp_pad)[:, None]
    j = np.arange(LANES)[None, :]
    lo = np.maximum(i * CMP_STRIDE, j * SEL_LEN)
    hi = np.minimum(i * CMP_STRIDE + CMP_LEN, (j + 1) * SEL_LEN)
    ov = np.maximum(hi - lo, 0) / CMP_LEN
    ov[:, n_sel:] = 0.0
    return ov.astype(np.float32)


def _gate_columns():
    src = -np.ones((KV_GROUPS * LANES,), np.int64)
    for g in range(KV_GROUPS):
        for j in range(HPG):
            for br in range(3):
                src[g * LANES + br * HPG + j] = g * HPG * 3 + j * 3 + br
    return src


def nsa_layer(xb, b, s, w_in, pe_k, w1_k, w2_k, pe_v, w1_v, w2_v, band, cmpb):
    n = xb.shape[0]
    hd = N_HEADS * HEAD_DIM
    gw = KV_GROUPS * HEAD_DIM
    w_main = w_in[:, :hd + 6 * gw].astype(BF)
    scale = jnp.concatenate([jnp.full((hd,), HEAD_DIM ** -0.5, F32), jnp.ones((6 * gw,), F32)])
    ybf = matmul_scaled(xb, w_main, scale, BF, tm=512, tn=512)
    src = _gate_columns()
    w_gl = jnp.where(jnp.asarray(src >= 0), w_in[:, hd + 6 * gw:][:, np.maximum(src, 0)], 0.0).astype(BF)
    gl = matmul_scaled(xb, w_gl, jnp.ones((w_gl.shape[1],), F32), F32, tm=512, tn=512)
    cb = hd // gw
    kcmp = nsa_compress(ybf, b, s, cb + 0, pe_k, w1_k, w2_k)
    vcmp = nsa_compress(ybf, b, s, cb + 1, pe_v, w1_v, w2_v)
    overlap_t = jnp.asarray(_cmp_sel_overlap(s // CMP_STRIDE, s // SEL_LEN).T).astype(BF)
    oc, nsel = nsa_cmp_select(ybf, kcmp, vcmp, cmpb, overlap_t, b, s)
    c128 = hd // HEAD_DIM
    o = nsa_main(ybf, nsel, band, gl, oc, b, s,
                 col_ksl=c128 + 2 * KV_GROUPS, col_vsl=c128 + 3 * KV_GROUPS,
                 col_kw=c128 + 4 * KV_GROUPS, col_vw=c128 + 5 * KV_GROUPS)
    return o.reshape(n, hd)


def dsa_layer(xb, b, s, w_in, kv_norm_g, w_uk, w_uv, band):
    n = xb.shape[0]
    hd = N_HEADS * HEAD_DIM
    o_c = hd
    o_qi = hd + KV_RANK
    o_ki = o_qi + IDX_HEADS * IDX_DIM
    o_wi = o_ki + IDX_DIM
    w_q = jnp.concatenate([w_in[:, :hd], w_in[:, o_qi:o_ki]], axis=1).astype(BF)
    scale_q = jnp.concatenate([jnp.ones((hd,), F32), jnp.full((IDX_HEADS * IDX_DIM,), IDX_DIM ** -0.5, F32)])
    qbf = matmul_scaled(xb, w_q, scale_q, BF, tm=512, tn=512)
    cn = matmul_rmsnorm(xb, w_in[:, o_c:o_qi].astype(BF), kv_norm_g, tm=512)
    w_kw = jnp.pad(w_in[:, o_ki:], ((0, 0), (0, LANES - IDX_DIM - IDX_HEADS))).astype(BF)
    scale_kw = jnp.concatenate([jnp.ones((IDX_DIM,), F32), jnp.full((IDX_HEADS,), IDX_HEADS ** -0.5, F32),
                                jnp.zeros((LANES - IDX_DIM - IDX_HEADS,), F32)])
    kw = matmul_scaled(xb, w_kw, scale_kw, F32, tm=512, tn=LANES)
    nmask = dsa_index(qbf, kw, b, s, col_qi=hd // (IDX_HEADS * IDX_DIM))
    o = dsa_attention(qbf, cn, nmask, band[:, :2], w_uk, w_uv, b, s)
    return o.reshape(n, hd)


def kernel(x, rel_bias, nsa_w_in, nsa_cmp_pe_k, nsa_cmp_w1_k, nsa_cmp_w2_k, nsa_cmp_pe_v, nsa_cmp_w1_v, nsa_cmp_w2_v, nsa_w_out, dsa_w_in, dsa_kv_norm_g, dsa_w_uk, dsa_w_uv, dsa_w_out, ffn_w_gate, ffn_w_up, ffn_w_down, moe_w_router, moe_b_router, moe_w_gate, moe_w_up, moe_w_down, ln_mix_g, ln_mix_b, ln_ffn_g, ln_ffn_b):
    b, s, d = x.shape
    n = b * s
    assert s % TQ == 0 and s // SEL_LEN <= LANES
    x0 = x.reshape(n, d)
    band, cmpb = _bias_tables(rel_bias)
    o = nsa_layer(x0.astype(BF), b, s, nsa_w_in, nsa_cmp_pe_k, nsa_cmp_w1_k, nsa_cmp_w2_k,
                  nsa_cmp_pe_v, nsa_cmp_w1_v, nsa_cmp_w2_v, band, cmpb)
    x1, x1b = matmul_residual_ln(o, nsa_w_out.astype(BF), x0, ln_mix_g[0], ln_mix_b[0], tm=256, tk=d)
    hff = swiglu_up(x1b, ffn_w_gate.astype(BF), ffn_w_up.astype(BF), tm=512, tn=512)
    x2, x2b = matmul_residual_ln(hff, ffn_w_down.astype(BF), x1, ln_ffn_g[0], ln_ffn_b[0], tm=256,
                                 tk=hff.shape[1] // 2)
    o = dsa_layer(x2b, b, s, dsa_w_in, dsa_kv_norm_g, dsa_w_uk, dsa_w_uv, band)
    x3, x3b = matmul_residual_ln(o, dsa_w_out.astype(BF), x2, ln_mix_g[1], ln_mix_b[1], tm=256, tk=d)
    out = moe_layer(x3, x3b, moe_w_router, moe_b_router, moe_w_gate, moe_w_up, moe_w_down,
                    ln_ffn_g[1], ln_ffn_b[1])
    return out.reshape(b, s, d)
```

```python
import functools
import math

import numpy as np
import jax
import jax.numpy as jnp
from jax import lax
from jax.experimental import pallas as pl
from jax.experimental.pallas import tpu as pltpu

F32 = jnp.float32
BF = jnp.bfloat16
I32 = jnp.int32

N_HEADS = 16
HEAD_DIM = 128
KV_GROUPS = 4
HPG = N_HEADS // KV_GROUPS
CMP_LEN = 32
CMP_STRIDE = 16
SEL_LEN = 64
SEL_TOP = 16
WINDOW = 512
SEL_FORCE = 1e4
KV_RANK = 256
IDX_HEADS = 8
IDX_DIM = 64
IDX_TOPK = 256
REL_BUCKETS = 32
REL_MAX_DIST = 128
N_EXPERTS = 8
DEPTH = 2
ALPHA = (2 * DEPTH) ** 0.25
LN_EPS = 1e-5
RMS_EPS = 1e-6
NEG_INF = -1e30

LOG2E = math.log2(math.e)
BIG = float(2.0 ** 100)
M_INIT = -3.0e38
TQ = 256
LANES = 128
VMEM_LIMIT_BYTES = 60000 * 1024


def _cp(sem, vmem=None):
    return pltpu.CompilerParams(dimension_semantics=sem, vmem_limit_bytes=vmem or VMEM_LIMIT_BYTES)


def _dot(a, b):
    return jnp.dot(a.astype(BF), b.astype(BF), preferred_element_type=F32)


def _dot_nt(a, b):
    return lax.dot_general(a.astype(BF), b.astype(BF), (((1,), (1,)), ((), ())),
                           preferred_element_type=F32)


def _layer_norm_rows(z, g, b):
    mu = jnp.mean(z, axis=-1, keepdims=True)
    zc = z - mu
    var = jnp.mean(zc * zc, axis=-1, keepdims=True)
    return zc * lax.rsqrt(var + LN_EPS) * g + b


def _mm_scale_kernel(a_ref, b_ref, s_ref, o_ref):
    o_ref[...] = (_dot(a_ref[...], b_ref[...]) * s_ref[...]).astype(o_ref.dtype)


def matmul_scaled(a, b, scale, out_dtype, tm, tn):
    m, k = a.shape
    n = b.shape[1]
    return pl.pallas_call(
        _mm_scale_kernel,
        grid=(m // tm, n // tn),
        in_specs=[pl.BlockSpec((tm, k), lambda i, j: (i, 0)),
                  pl.BlockSpec((k, tn), lambda i, j: (0, j)),
                  pl.BlockSpec((1, tn), lambda i, j: (0, j))],
        out_specs=pl.BlockSpec((tm, tn), lambda i, j: (i, j)),
        out_shape=jax.ShapeDtypeStruct((m, n), out_dtype),
        compiler_params=_cp(("parallel", "parallel")),
        name="matmul_scaled",
    )(a, b, scale.reshape(1, n).astype(F32))


def _mm_rms_kernel(a_ref, b_ref, g_ref, o_ref):
    c = _dot(a_ref[...], b_ref[...])
    r = lax.rsqrt(jnp.mean(c * c, axis=-1, keepdims=True) + RMS_EPS)
    o_ref[...] = (c * r * g_ref[...]).astype(o_ref.dtype)


def matmul_rmsnorm(a, b, g, tm):
    m, k = a.shape
    n = b.shape[1]
    return pl.pallas_call(
        _mm_rms_kernel,
        grid=(m // tm,),
        in_specs=[pl.BlockSpec((tm, k), lambda i: (i, 0)),
                  pl.BlockSpec((k, n), lambda i: (0, 0)),
                  pl.BlockSpec((1, n), lambda i: (0, 0))],
        out_specs=pl.BlockSpec((tm, n), lambda i: (i, 0)),
        out_shape=jax.ShapeDtypeStruct((m, n), BF),
        compiler_params=_cp(("parallel",)),
        name="matmul_rmsnorm",
    )(a, b, g.reshape(1, n).astype(F32))


def _mm_ln_kernel(a_ref, b_ref, r_ref, g_ref, be_ref, of_ref, ob_ref, acc_ref, *, nk):
    kk = pl.program_id(1)

    @pl.when(kk == 0)
    def _():
        acc_ref[...] = jnp.zeros_like(acc_ref)

    acc_ref[...] += _dot(a_ref[...], b_ref[...])

    @pl.when(kk == nk - 1)
    def _():
        y = _layer_norm_rows(ALPHA * r_ref[...] + acc_ref[...], g_ref[...], be_ref[...])
        of_ref[...] = y
        ob_ref[...] = y.astype(BF)


def matmul_residual_ln(a, b, res, g, beta, tm, tk):
    m, k = a.shape
    n = b.shape[1]
    nk = k // tk
    return pl.pallas_call(
        functools.partial(_mm_ln_kernel, nk=nk),
        grid=(m // tm, nk),
        in_specs=[pl.BlockSpec((tm, tk), lambda i, kk: (i, kk)),
                  pl.BlockSpec((tk, n), lambda i, kk: (kk, 0)),
                  pl.BlockSpec((tm, n), lambda i, kk: (i, 0)),
                  pl.BlockSpec((1, n), lambda i, kk: (0, 0)),
                  pl.BlockSpec((1, n), lambda i, kk: (0, 0))],
        out_specs=[pl.BlockSpec((tm, n), lambda i, kk: (i, 0)),
                   pl.BlockSpec((tm, n), lambda i, kk: (i, 0))],
        out_shape=[jax.ShapeDtypeStruct((m, n), F32), jax.ShapeDtypeStruct((m, n), BF)],
        scratch_shapes=[pltpu.VMEM((tm, n), F32)],
        compiler_params=_cp(("parallel", "arbitrary")),
        name="matmul_residual_ln",
    )(a, b, res, g.reshape(1, n).astype(F32), beta.reshape(1, n).astype(F32))


def _silu(x):
    return x * (1.0 / (1.0 + jnp.exp(-x)))


def _swiglu_up_kernel(a_ref, wg_ref, wu_ref, o_ref):
    a = a_ref[...]
    o_ref[...] = (_silu(_dot(a, wg_ref[...])) * _dot(a, wu_ref[...])).astype(o_ref.dtype)


def swiglu_up(a, wg, wu, tm, tn):
    m, k = a.shape
    n = wg.shape[1]
    return pl.pallas_call(
        _swiglu_up_kernel,
        grid=(m // tm, n // tn),
        in_specs=[pl.BlockSpec((tm, k), lambda i, j: (i, 0)),
                  pl.BlockSpec((k, tn), lambda i, j: (0, j)),
                  pl.BlockSpec((k, tn), lambda i, j: (0, j))],
        out_specs=pl.BlockSpec((tm, tn), lambda i, j: (i, j)),
        out_shape=jax.ShapeDtypeStruct((m, n), BF),
        compiler_params=_cp(("parallel", "parallel")),
        name="swiglu_up",
    )(a, wg, wu)


def _bucket_of_distance():
    n = np.arange(REL_MAX_DIST + 1)
    exact = REL_BUCKETS // 2
    nf = np.maximum(n, exact).astype(np.float64)
    large = exact + (np.log(nf / exact) / math.log(REL_MAX_DIST / exact) * (REL_BUCKETS - exact)).astype(np.int64)
    return np.where(n < exact, n, np.minimum(large, REL_BUCKETS - 1)).astype(np.int32)


def _bucket_starts():
    bk = _bucket_of_distance()
    return [int(np.argmax(bk >= b)) for b in range(REL_BUCKETS)]


def _bias_kernel(tab_ref, band_ref, cmp_ref, *, starts):
    h = pl.program_id(0)
    far = tab_ref[REL_BUCKETS - 1, h]

    def lookup(d):
        val = jnp.zeros(d.shape, F32)
        for b in range(REL_BUCKETS - 2, -1, -1):
            val = jnp.where(d < starts[b + 1], (tab_ref[b, h] - far) * LOG2E, val)
        return val

    j = lax.broadcasted_iota(I32, (TQ, TQ), 0)
    i = lax.broadcasted_iota(I32, (TQ, TQ), 1)
    for r in range(3):
        d = TQ * r + i - j
        band_ref[r] = jnp.where((d < 0) | (d >= WINDOW), -BIG, lookup(d))
    c = lax.broadcasted_iota(I32, (CMP_NEAR, TQ), 0)
    i = lax.broadcasted_iota(I32, (CMP_NEAR, TQ), 1)
    d = i - CMP_STRIDE * (c - CMP_NEAR // 2) - (CMP_LEN - 1)
    cmp_ref[...] = jnp.where(d < 0, -BIG, lookup(d))


CMP_NEAR = 2 * (TQ // CMP_STRIDE)


def _bias_tables(rel_bias):
    return pl.pallas_call(
        functools.partial(_bias_kernel, starts=_bucket_starts()),
        grid=(N_HEADS,),
        in_specs=[pl.BlockSpec(memory_space=pltpu.SMEM)],
        out_specs=[pl.BlockSpec((None, 3, TQ, TQ), lambda h: (h // HPG, 0, 0, h % HPG)),
                   pl.BlockSpec((None, CMP_NEAR, TQ), lambda h: (h // HPG, 0, h % HPG))],
        out_shape=[jax.ShapeDtypeStruct((N_HEADS // HPG, 3, TQ, HPG * TQ), F32),
                   jax.ShapeDtypeStruct((N_HEADS // HPG, CMP_NEAR, HPG * TQ), F32)],
        compiler_params=_cp(("parallel",)),
        name="bias_tables",
    )(rel_bias.astype(F32))


def _heads_to_rows(q):
    return jnp.concatenate([q[:, j * HEAD_DIM:(j + 1) * HEAD_DIM] for j in range(HPG)], axis=0)


def _osm_update(carry, score_fn, vts, acc_ref):
    ms, ls = carry
    new_m, new_l = [], []
    for j in range(HPG):
        st = score_fn(j)
        m_new = jnp.maximum(ms[j], jnp.max(st, axis=0, keepdims=True))
        alpha = jnp.exp2(ms[j] - m_new)
        p = jnp.exp2(st - m_new)
        new_m.append(m_new)
        new_l.append(alpha * ls[j] + jnp.sum(p, axis=0, keepdims=True))
        pb = p.astype(BF)
        pv = jnp.dot(vts[0], pb[:TQ], preferred_element_type=F32)
        for i in range(1, len(vts)):
            pv = pv + jnp.dot(vts[i], pb[i * TQ:(i + 1) * TQ], preferred_element_type=F32)
        acc_ref[j] = alpha * acc_ref[j] + pv
    return tuple(new_m), tuple(new_l)


def _osm_init(acc_ref):
    acc_ref[...] = jnp.zeros_like(acc_ref)
    return ((jnp.full((1, TQ), M_INIT, F32),) * HPG, (jnp.zeros((1, TQ), F32),) * HPG)


def _causal_chunk_attention(qt, score_fn, extra_fn, vt_fn, band_ref, acc_ref, sbuf_ref, pbuf_ref):
    rows = HPG * TQ
    sa_ref, sb_ref = sbuf_ref.at[0], sbuf_ref.at[1]
    pa_ref, pb_ref = pbuf_ref.at[0], pbuf_ref.at[1]

    def softmax_stage(m, l, s_ref, p_ref, kc, bias=None):
        st = s_ref[...]
        extra = extra_fn(kc)
        if extra is not None:
            st = st + extra
        if bias is not None:
            st = st + bias
        m_new = jnp.maximum(m, jnp.max(st, axis=0, keepdims=True))
        alpha = jnp.exp2(m - m_new)
        p = jnp.exp2(st - m_new)
        p_ref[...] = p.astype(BF)
        return m_new, alpha * l + jnp.sum(p, axis=0, keepdims=True), alpha

    def value_stage(alpha, p_ref, kc):
        acc_ref[...] = alpha * acc_ref[...] + jnp.dot(vt_fn(kc), p_ref[...], preferred_element_type=F32)

    def far_pair(i, carry):
        m, l, alpha_b = carry
        a = 2 * i
        sb_ref[...] = score_fn(a + 1)
        value_stage(alpha_b, pb_ref, jnp.maximum(a - 1, 0))
        m, l, alpha_a = softmax_stage(m, l, sa_ref, pa_ref, a)
        sa_ref[...] = score_fn(a + 2)
        value_stage(alpha_a, pa_ref, a)
        m, l, alpha_b = softmax_stage(m, l, sb_ref, pb_ref, a + 1)
        return m, l, alpha_b

    acc_ref[...] = jnp.zeros_like(acc_ref)
    pb_ref[...] = jnp.zeros_like(pb_ref)
    sa_ref[...] = score_fn(0)
    npair = lax.shift_right_logical(jnp.maximum(qt - 1, 0), 1)
    init = (jnp.full((1, rows), M_INIT, F32), jnp.zeros((1, rows), F32), jnp.ones((1, rows), F32))
    m, l, alpha_b = lax.fori_loop(0, npair, far_pair, init)
    a = 2 * npair
    odd = (qt & 1) == 1
    first = qt == 0
    sb_ref[...] = score_fn(a + 1)
    value_stage(alpha_b, pb_ref, jnp.maximum(a - 1, 0))
    m, l, alpha_a = softmax_stage(m, l, sa_ref, pa_ref, a,
                                  jnp.where(odd, band_ref[1], jnp.where(first, band_ref[0], 0.0)))
    value_stage(alpha_a, pa_ref, a)
    m, l, alpha_b = softmax_stage(m, l, sb_ref, pb_ref, a + 1,
                                  jnp.where(odd, band_ref[0], jnp.where(first, -BIG, band_ref[1])))
    value_stage(alpha_b, pb_ref, a + 1)

    def diag_chunk(carry):
        m, l = carry
        sa_ref[...] = score_fn(qt)
        m, l, alpha_a = softmax_stage(m, l, sa_ref, pa_ref, qt, band_ref[0])
        value_stage(alpha_a, pa_ref, qt)
        return m, l

    m, l = lax.cond((qt >= 2) & jnp.logical_not(odd), diag_chunk, lambda c: c, (m, l))
    return l


def _store_transposed(src_ref, dst_ref, nchunk):
    def body(kc, _):
        start = pl.multiple_of(kc * TQ, TQ)
        dst_ref[kc] = src_ref[pl.ds(start, TQ), :].astype(F32).T.astype(dst_ref.dtype)
        return 0

    lax.fori_loop(0, nchunk, body, 0)


def _gelu_tanh(x):
    return 0.5 * x * (1.0 + jnp.tanh(math.sqrt(2.0 / math.pi) * (x + 0.044715 * (x * x * x))))


def _compress_kernel(*refs, nch):
    x_refs = refs[:CMP_STRIDE]
    pe_ref, w1_ref, w2_ref, o_ref = refs[CMP_STRIDE:]
    acc_a = [jnp.zeros((nch, HEAD_DIM), F32) for _ in range(KV_GROUPS)]
    acc_b = [jnp.zeros((nch, HEAD_DIM), F32) for _ in range(KV_GROUPS)]
    for l in range(CMP_STRIDE):
        x = x_refs[l][...].astype(F32)
        xa = (x + pe_ref[l:l + 1, :]).astype(BF)
        xb = (x + pe_ref[CMP_STRIDE + l:CMP_STRIDE + l + 1, :]).astype(BF)
        for g in range(KV_GROUPS):
            sl = slice(g * HEAD_DIM, (g + 1) * HEAD_DIM)
            acc_a[g] = acc_a[g] + jnp.dot(xa[:, sl], w1_ref[l], preferred_element_type=F32)
            acc_b[g] = acc_b[g] + jnp.dot(xb[:, sl], w1_ref[CMP_STRIDE + l], preferred_element_type=F32)
    for g in range(KV_GROUPS):
        pre = acc_a[g] + pltpu.roll(acc_b[g], nch - 1, axis=0)
        hid = _gelu_tanh(pre).astype(BF)
        o_ref[:, g * HEAD_DIM:(g + 1) * HEAD_DIM] = jnp.dot(hid, w2_ref[...], preferred_element_type=F32).astype(BF)


def nsa_compress(ybf, b, s, col_block, pe, w1, w2):
    ncols = ybf.shape[1]
    nch = s // CMP_STRIDE
    blk_w = KV_GROUPS * HEAD_DIM
    per_tok = ncols // blk_w
    y3 = ybf.reshape(b, nch, CMP_STRIDE * ncols)
    pe_t = jnp.tile(pe.astype(F32), (1, KV_GROUPS))
    in_specs = [pl.BlockSpec((None, nch, blk_w), (lambda bi, l=l: (bi, 0, l * per_tok + col_block)))
                for l in range(CMP_STRIDE)]
    in_specs += [pl.BlockSpec((CMP_LEN, blk_w), lambda bi: (0, 0)),
                 pl.BlockSpec((CMP_LEN, HEAD_DIM, HEAD_DIM), lambda bi: (0, 0, 0)),
                 pl.BlockSpec((HEAD_DIM, HEAD_DIM), lambda bi: (0, 0))]
    return pl.pallas_call(
        functools.partial(_compress_kernel, nch=nch),
        grid=(b,),
        in_specs=in_specs,
        out_specs=pl.BlockSpec((None, nch, blk_w), lambda bi: (bi, 0, 0)),
        out_shape=jax.ShapeDtypeStruct((b, nch, blk_w), BF),
        compiler_params=_cp(("parallel",)),
        name="nsa_compress",
    )(*([y3] * CMP_STRIDE), pe_t, w1.astype(BF), w2.astype(BF))


def _nsa_cmp_kernel(q_ref, kc_ref, vc_ref, cb_ref, ovt_ref, oc_ref, ns_ref, vct_ref, s_ref, *, n_sel, n_top):
    qt = pl.program_id(2)
    t0 = qt * TQ
    ncp = kc_ref.shape[0]
    pad = CMP_NEAR // 2

    @pl.when(qt == 0)
    def _():
        vct_ref[...] = vc_ref[...].astype(F32).T.astype(BF)
        s_ref[0:pad, :] = jnp.zeros((pad, HPG * TQ), F32)

    qs = _heads_to_rows(q_ref[...])
    s_ref[pad:pad + ncp, :] = _dot_nt(kc_ref[...], qs)
    near = pl.multiple_of(qt * (TQ // CMP_STRIDE), TQ // CMP_STRIDE)
    s_ref[pl.ds(near, CMP_NEAR), :] = s_ref[pl.ds(near, CMP_NEAR), :] + cb_ref[...]
    s = s_ref[pad:pad + ncp, :]
    key = lax.broadcasted_iota(I32, (ncp, HPG * TQ), 0)
    tq = t0 + (lax.broadcasted_iota(I32, (ncp, HPG * TQ), 1) & (TQ - 1))
    vis = (key * CMP_STRIDE + (CMP_LEN - 1)) <= tq
    s = jnp.where(vis, s, -BIG)
    m = jnp.max(s, axis=0, keepdims=True)
    p = jnp.where(vis, jnp.exp2(s - m), 0.0)
    l = jnp.sum(p, axis=0, keepdims=True)
    p = p * jnp.where(l > 0.0, 1.0 / l, 0.0)
    pb = p.astype(BF)
    oct = jnp.dot(vct_ref[...], pb, preferred_element_type=F32)
    score = jnp.zeros((LANES, TQ), F32)
    for j in range(HPG):
        cs = slice(j * TQ, (j + 1) * TQ)
        oc_ref[:, j * HEAD_DIM:(j + 1) * HEAD_DIM] = oct[:, cs].T.astype(oc_ref.dtype)
        score = score + jnp.dot(ovt_ref[...], pb[:, cs], preferred_element_type=F32)
    blk = lax.broadcasted_iota(I32, (LANES, TQ), 0)
    t = t0 + lax.broadcasted_iota(I32, (LANES, TQ), 1)
    cur = lax.shift_right_logical(t, int(math.log2(SEL_LEN)))
    forced = (blk == 0) | (blk == cur) | (blk == cur - 1)
    visible = blk * SEL_LEN <= t
    sc = jnp.where(forced, SEL_FORCE, jnp.where(visible, score, -1.0))
    sc = jnp.where(blk < n_sel, sc, -2.0)
    blkf = blk.astype(F32)

    def pick_one(_, carry):
        sc, sel = carry
        mx = jnp.max(sc, axis=0, keepdims=True)
        first = jnp.min(jnp.where(sc == mx, blkf, float(LANES)), axis=0, keepdims=True)
        hit = blkf == first
        return jnp.where(hit, -3.0, sc), jnp.where(hit, 1.0, sel)

    _, sel = lax.fori_loop(0, n_top, pick_one, (sc, jnp.zeros((LANES, TQ), F32)))
    ns_ref[...] = jnp.where(sel.T > 0.5, 0.0, BIG).astype(BF)


def nsa_cmp_select(ybf, kcmp, vcmp, cmpb, overlap_t, b, s):
    ncp = kcmp.shape[1]
    n_sel = s // SEL_LEN
    n_top = min(SEL_TOP, n_sel)
    y3 = ybf.reshape(b, s, ybf.shape[1])
    gw = HPG * HEAD_DIM
    return pl.pallas_call(
        functools.partial(_nsa_cmp_kernel, n_sel=n_sel, n_top=n_top),
        grid=(b, KV_GROUPS, s // TQ),
        in_specs=[pl.BlockSpec((None, TQ, gw), lambda bi, g, qt: (bi, qt, g)),
                  pl.BlockSpec((None, ncp, HEAD_DIM), lambda bi, g, qt: (bi, 0, g)),
                  pl.BlockSpec((None, ncp, HEAD_DIM), lambda bi, g, qt: (bi, 0, g)),
                  pl.BlockSpec((None, CMP_NEAR, HPG * TQ), lambda bi, g, qt: (g, 0, 0)),
                  pl.BlockSpec((LANES, ncp), lambda bi, g, qt: (0, 0))],
        out_specs=[pl.BlockSpec((None, TQ, gw), lambda bi, g, qt: (bi, qt, g)),
                   pl.BlockSpec((None, None, TQ, LANES), lambda bi, g, qt: (bi, g, qt, 0))],
        out_shape=[jax.ShapeDtypeStruct((b, s, KV_GROUPS * gw), BF),
                   jax.ShapeDtypeStruct((b, KV_GROUPS, s, LANES), BF)],
        scratch_shapes=[pltpu.VMEM((HEAD_DIM, ncp), BF),
                        pltpu.VMEM((ncp + CMP_NEAR, HPG * TQ), F32)],
        compiler_params=_cp(("parallel", "parallel", "arbitrary")),
        name="nsa_cmp_select",
    )(y3, kcmp, vcmp, cmpb, overlap_t)


def _nsa_main_kernel(q_ref, ks_ref, vs_ref, kw_ref, vw_ref, ns_ref, band_ref, gl_ref, oc_ref, o_ref,
                     vst_ref, vwt_ref, accs_ref, accw_ref, sbuf_ref, pbuf_ref, *, nchunk):
    qt = pl.program_id(2)

    @pl.when(qt == 0)
    def _():
        _store_transposed(vs_ref, vst_ref, nchunk)
        _store_transposed(vw_ref, vwt_ref, nchunk)

    q = q_ref[...]
    ns = ns_ref[...]
    qs = _heads_to_rows(q)
    qp = jnp.concatenate([qs, jnp.concatenate([ns] * HPG, axis=0)], axis=1)
    def head_cols(st, j):
        return st[:, j * TQ:(j + 1) * TQ]

    def sel_scores(kc, nk):
        start = pl.multiple_of(kc * TQ, TQ)
        k = ks_ref[pl.ds(start, nk * TQ), :]
        krow = lax.broadcasted_iota(I32, (nk * TQ, LANES), 0)
        klane = lax.broadcasted_iota(I32, (nk * TQ, LANES), 1)
        kblk = lax.shift_right_logical(krow, int(math.log2(SEL_LEN))) + kc * (TQ // SEL_LEN)
        oh = jnp.where(klane == kblk, -1.0, 0.0).astype(BF)
        return _dot_nt(jnp.concatenate([k, oh], axis=1), qp)

    l_s = _causal_chunk_attention(qt, lambda kc: sel_scores(kc, 1), lambda kc: None, lambda kc: vst_ref[kc],
                                  band_ref, accs_ref, sbuf_ref, pbuf_ref)

    def win_chunk(r, carry):
        kc = jnp.maximum(qt - r, 0)
        start = pl.multiple_of(kc * TQ, TQ)
        st = _dot_nt(kw_ref[pl.ds(start, TQ), :], qs) + jnp.where(qt >= r, band_ref[r], -BIG)
        return _osm_update(carry, lambda j: head_cols(st, j), [vwt_ref[kc]], accw_ref)

    carry = _osm_init(accw_ref)
    for r in (2, 1, 0):
        carry = win_chunk(r, carry)
    m_w, l_w = carry

    gates = 1.0 / (1.0 + jnp.exp(-gl_ref[...]))
    oc = oc_ref[...].astype(F32)
    for j in range(HPG):
        sl = slice(j * HEAD_DIM, (j + 1) * HEAD_DIM)
        cs = slice(j * TQ, (j + 1) * TQ)
        o_s = (accs_ref[:, cs] * (1.0 / l_s[:, cs])).T
        o_w = (accw_ref[j] * (1.0 / l_w[j])).T
        o = (gates[:, j:j + 1] * oc[:, sl] + gates[:, HPG + j:HPG + j + 1] * o_s
             + gates[:, 2 * HPG + j:2 * HPG + j + 1] * o_w)
        o_ref[:, sl] = o.astype(o_ref.dtype)


def nsa_main(ybf, nsel, band, gl, oc, b, s, col_ksl, col_vsl, col_kw, col_vw):
    y3 = ybf.reshape(b, s, ybf.shape[1])
    gw = HPG * HEAD_DIM
    kv_spec = lambda cb: pl.BlockSpec((None, s, HEAD_DIM), lambda bi, g, qt: (bi, 0, cb + g))
    nchunk = s // TQ
    return pl.pallas_call(
        functools.partial(_nsa_main_kernel, nchunk=nchunk),
        grid=(b, KV_GROUPS, nchunk),
        in_specs=[pl.BlockSpec((None, TQ, gw), lambda bi, g, qt: (bi, qt, g)),
                  kv_spec(col_ksl), kv_spec(col_vsl), kv_spec(col_kw), kv_spec(col_vw),
                  pl.BlockSpec((None, None, TQ, LANES), lambda bi, g, qt: (bi, g, qt, 0)),
                  pl.BlockSpec((None, 3, TQ, HPG * TQ), lambda bi, g, qt: (g, 0, 0, 0)),
                  pl.BlockSpec((None, TQ, LANES), lambda bi, g, qt: (bi, qt, g)),
                  pl.BlockSpec((None, TQ, gw), lambda bi, g, qt: (bi, qt, g))],
        out_specs=pl.BlockSpec((None, TQ, gw), lambda bi, g, qt: (bi, qt, g)),
        out_shape=jax.ShapeDtypeStruct((b, s, KV_GROUPS * gw), BF),
        scratch_shapes=[pltpu.VMEM((nchunk, HEAD_DIM, TQ), BF), pltpu.VMEM((nchunk, HEAD_DIM, TQ), BF),
                        pltpu.VMEM((HEAD_DIM, HPG * TQ), F32), pltpu.VMEM((HPG, HEAD_DIM, TQ), F32),
                        pltpu.VMEM((2, TQ, HPG * TQ), F32), pltpu.VMEM((2, TQ, HPG * TQ), BF)],
        compiler_params=_cp(("parallel", "parallel", "arbitrary")),
        name="nsa_main",
    )(y3, y3, y3, y3, y3, nsel, band, gl.reshape(b, s, gl.shape[1]), oc)


def _dsa_index_kernel(qi_ref, kw_ref, wq_ref, nm_ref, ka_ref, kb_ref, key_ref, *, k_sel, nchunk, idx_bits):
    qt = pl.program_id(1)
    t0 = qt * TQ
    half = LANES // 2

    @pl.when(qt == 0)
    def _():
        lane = lax.broadcasted_iota(I32, kw_ref.shape, 1)
        ka = jnp.where(lane < IDX_DIM, kw_ref[...], 0.0)
        ka_ref[...] = ka.astype(BF)
        kb_ref[...] = pltpu.roll(ka, half, axis=1).astype(BF)

    qi = qi_ref[...]
    npair = IDX_HEADS // 2
    lq = jnp.concatenate([qi[:, p * LANES:(p + 1) * LANES] for p in range(npair)], axis=0)
    wt = wq_ref[...].T
    w_r = [wt[IDX_DIM + h:IDX_DIM + h + 1, :] for h in range(IDX_HEADS)]
    kpos = lax.broadcasted_iota(I32, (TQ, TQ), 0)
    qpos = t0 + lax.broadcasted_iota(I32, (TQ, TQ), 1)

    def score_chunk(kc, _):
        start = pl.multiple_of(kc * TQ, TQ)
        da = _dot_nt(ka_ref[pl.ds(start, TQ), :], lq)
        db = _dot_nt(kb_ref[pl.ds(start, TQ), :], lq)
        sc = jnp.zeros((TQ, TQ), F32)
        for p in range(npair):
            cs = slice(p * TQ, (p + 1) * TQ)
            sc = sc + w_r[2 * p] * jnp.maximum(da[:, cs], 0.0) + w_r[2 * p + 1] * jnp.maximum(db[:, cs], 0.0)
        sc = jnp.where(sc == 0.0, 0.0, sc)
        sc = jnp.where(kc * TQ + kpos <= qpos, sc, NEG_INF)
        bits = lax.bitcast_convert_type(sc, I32)
        key_ref[kc] = jnp.where(bits < 0, bits ^ 0x7FFFFFFF, bits)
        return 0

    nproc = qt + 1
    lax.fori_loop(0, nproc, score_chunk, 0)

    def count(pred):
        def body(kc, acc):
            return acc + jnp.sum(jnp.where(pred(key_ref[kc], kc), 1.0, 0.0), axis=0, keepdims=True)
        return lax.fori_loop(0, nproc, body, jnp.zeros((1, TQ), F32))

    def count_ge(cand):
        return count(lambda k, kc: k >= cand)

    kf = float(k_sel)
    thr0 = jnp.full((1, TQ), -2 ** 31, I32)
    done0 = jnp.where(count_ge(thr0) == kf, 1.0, 0.0)

    def search_cond(state):
        i, _, done = state
        return (i < 32) & (jnp.min(done) < 0.5)

    def search_body(state):
        i, thr, done = state
        cand = thr ^ lax.shift_left(jnp.int32(1), jnp.int32(31) - i)
        cnt = count_ge(cand)
        take = (cnt >= kf) & (done < 0.5)
        thr = jnp.where(take, cand, thr)
        done = jnp.where(take & (cnt == kf), 1.0, done)
        return i + 1, thr, done

    _, thr, done = lax.while_loop(search_cond, search_body, (jnp.int32(0), thr0, done0))

    def tie_break(_):
        need = kf - count_ge(thr + 1)

        def ties_below(bound):
            return count(lambda k, kc: (k == thr) & (kc * TQ + kpos < bound))

        def idx_bit(i, jm):
            cand = jm | lax.shift_left(jnp.int32(1), jnp.int32(idx_bits - 1) - i)
            return jnp.where(ties_below(cand) < need, cand, jm)

        jm = lax.fori_loop(0, idx_bits, idx_bit, jnp.zeros((1, TQ), I32))
        return jnp.where(done > 0.5, jnp.int32(2 ** 30), jm)

    jm = lax.cond(jnp.min(done) < 0.5, tie_break, lambda _: jnp.full((1, TQ), 2 ** 30, I32), 0)

    def write_chunk(kc, _):
        k = key_ref[kc]
        kidx = kc * TQ + kpos
        sel = (kidx <= qpos) & ((k > thr) | ((k == thr) & (kidx <= jm)))
        nm_ref[kc] = jnp.where(sel, 0.0, -BIG).astype(BF)
        return 0

    lax.fori_loop(0, nproc, write_chunk, 0)

    def fill_chunk(kc, _):
        nm_ref[kc] = jnp.full((TQ, TQ), -BIG, BF)
        return 0

    lax.fori_loop(nproc, nchunk, fill_chunk, 0)


def dsa_index(qbf, kw, b, s, col_qi):
    nchunk = s // TQ
    k_sel = min(IDX_TOPK, s // 4)
    q3 = qbf.reshape(b, s, qbf.shape[1])
    kw3 = kw.reshape(b, s, LANES)
    return pl.pallas_call(
        functools.partial(_dsa_index_kernel, k_sel=k_sel, nchunk=nchunk, idx_bits=int(math.log2(s))),
        grid=(b, nchunk),
        in_specs=[pl.BlockSpec((None, TQ, IDX_HEADS * IDX_DIM), lambda bi, qt: (bi, qt, col_qi)),
                  pl.BlockSpec((None, s, LANES), lambda bi, qt: (bi, 0, 0)),
                  pl.BlockSpec((None, TQ, LANES), lambda bi, qt: (bi, qt, 0))],
        out_specs=pl.BlockSpec((None, None, nchunk, TQ, TQ), lambda bi, qt: (bi, qt, 0, 0, 0)),
        out_shape=jax.ShapeDtypeStruct((b, nchunk, nchunk, TQ, TQ), BF),
        scratch_shapes=[pltpu.VMEM((s, LANES), BF), pltpu.VMEM((s, LANES), BF),
                        pltpu.VMEM((nchunk, TQ, TQ), I32)],
        compiler_params=_cp(("parallel", "arbitrary")),
        name="dsa_index",
    )(q3, kw3, kw3)


def _dsa_attn_kernel(q_ref, c_ref, nm_ref, band_ref, wuk_ref, wuvt_ref, o_ref, ct_ref, acc_ref, sbuf_ref,
                     pbuf_ref, *, nchunk):
    qt = pl.program_id(1)
    hg = pl.program_id(2)

    @pl.when((qt == 0) & (hg == 0))
    def _():
        _store_transposed(c_ref, ct_ref, nchunk)

    q = q_ref[...]
    ql = jnp.concatenate(
        [(jnp.dot(q[:, j * HEAD_DIM:(j + 1) * HEAD_DIM], wuk_ref[j], preferred_element_type=F32)
          * (HEAD_DIM ** -0.5 * LOG2E)).astype(BF) for j in range(HPG)], axis=0)

    def scores(kc):
        start = pl.multiple_of(kc * TQ, TQ)
        return _dot_nt(c_ref[pl.ds(start, TQ), :], ql)

    def member_mask(kc):
        nm = nm_ref[kc].astype(F32)
        return jnp.concatenate([nm] * HPG, axis=1)

    l = _causal_chunk_attention(qt, scores, member_mask, lambda kc: ct_ref[kc], band_ref, acc_ref,
                                sbuf_ref, pbuf_ref)
    o_lat = (acc_ref[...] * (1.0 / l)).astype(BF)
    for j in range(HPG):
        ot = jnp.dot(wuvt_ref[j], o_lat[:, j * TQ:(j + 1) * TQ], preferred_element_type=F32)
        o_ref[:, j * HEAD_DIM:(j + 1) * HEAD_DIM] = ot.T.astype(o_ref.dtype)


def dsa_attention(qbf, cn, nmask, band, w_uk, w_uv, b, s):
    nchunk = s // TQ
    q3 = qbf.reshape(b, s, qbf.shape[1])
    c3 = cn.reshape(b, s, KV_RANK)
    gw = HPG * HEAD_DIM
    return pl.pallas_call(
        functools.partial(_dsa_attn_kernel, nchunk=nchunk),
        grid=(b, nchunk, N_HEADS // HPG),
        in_specs=[pl.BlockSpec((None, TQ, gw), lambda bi, qt, hg: (bi, qt, hg)),
                  pl.BlockSpec((None, s, KV_RANK), lambda bi, qt, hg: (bi, 0, 0)),
                  pl.BlockSpec((None, None, nchunk, TQ, TQ), lambda bi, qt, hg: (bi, qt, 0, 0, 0)),
                  pl.BlockSpec((None, 2, TQ, HPG * TQ), lambda bi, qt, hg: (hg, 0, 0, 0)),
                  pl.BlockSpec((HPG, HEAD_DIM, KV_RANK), lambda bi, qt, hg: (hg, 0, 0)),
                  pl.BlockSpec((HPG, HEAD_DIM, KV_RANK), lambda bi, qt, hg: (hg, 0, 0))],
        out_specs=pl.BlockSpec((None, TQ, gw), lambda bi, qt, hg: (bi, qt, hg)),
        out_shape=jax.ShapeDtypeStruct((b, s, N_HEADS * HEAD_DIM), BF),
        scratch_shapes=[pltpu.VMEM((nchunk, KV_RANK, TQ), BF), pltpu.VMEM((KV_RANK, HPG * TQ), F32),
                        pltpu.VMEM((2, TQ, HPG * TQ), F32), pltpu.VMEM((2, TQ, HPG * TQ), BF)],
        compiler_params=_cp(("parallel", "arbitrary", "arbitrary")),
        name="dsa_attention",
    )(q3, c3, nmask, band, w_uk.astype(BF), jnp.swapaxes(w_uv, 1, 2).astype(BF))


ROUTER_TM = 256
MOE_TM = 512


def _router_kernel(x_ref, w_ref, b_ref, o_ref, cnt_ref, carry_ref):
    i = pl.program_id(0)

    @pl.when(i == 0)
    def _():
        carry_ref[...] = jnp.zeros_like(carry_ref)

    tm = x_ref.shape[0]
    logits = jnp.dot(x_ref[...], w_ref[...], preferred_element_type=F32,
                     precision=lax.Precision.HIGHEST) + b_ref[...]
    lane = lax.broadcasted_iota(I32, (tm, LANES), 1)
    lanef = lane.astype(F32)
    lg = jnp.where(lane < N_EXPERTS, logits, -BIG)
    m1 = jnp.max(lg, axis=-1, keepdims=True)
    i1 = jnp.min(jnp.where(lg == m1, lanef, float(LANES)), axis=-1, keepdims=True)
    lg2 = jnp.where(lanef == i1, -BIG, lg)
    m2 = jnp.max(lg2, axis=-1, keepdims=True)
    i2 = jnp.min(jnp.where(lg2 == m2, lanef, float(LANES)), axis=-1, keepdims=True)
    e2 = jnp.exp(m2 - m1)
    den = 1.0 + e2
    w1 = 1.0 / den
    w2 = e2 / den
    hit1 = lanef == i1
    hit2 = lanef == i2
    onehot = jnp.where(hit1 | hit2, 1.0, 0.0)
    r = lax.broadcasted_iota(I32, (tm, tm), 0)
    c = lax.broadcasted_iota(I32, (tm, tm), 1)
    tri = jnp.where(c < r, 1.0, 0.0).astype(BF)
    before = jnp.dot(tri, onehot.astype(BF), preferred_element_type=F32) + carry_ref[...]
    rank1 = jnp.sum(jnp.where(hit1, before, 0.0), axis=-1, keepdims=True)
    rank2 = jnp.sum(jnp.where(hit2, before, 0.0), axis=-1, keepdims=True)
    carry_ref[...] = carry_ref[...] + jnp.sum(onehot, axis=0, keepdims=True)
    vals = (i1, i2, w1, w2, rank1, rank2)
    out = jnp.zeros((tm, LANES), F32)
    for k, v in enumerate(vals):
        out = jnp.where(lane == k, v, out)
    o_ref[...] = out
    cnt_ref[...] = jnp.broadcast_to(carry_ref[...], cnt_ref.shape)


def moe_router(x, w_router, b_router):
    n, d = x.shape
    wp = jnp.pad(w_router.astype(F32), ((0, 0), (0, LANES - N_EXPERTS)))
    bp = jnp.pad(b_router.astype(F32), (0, LANES - N_EXPERTS)).reshape(1, LANES)
    return pl.pallas_call(
        _router_kernel,
        grid=(n // ROUTER_TM,),
        in_specs=[pl.BlockSpec((ROUTER_TM, d), lambda i: (i, 0)),
                  pl.BlockSpec((d, LANES), lambda i: (0, 0)),
                  pl.BlockSpec((1, LANES), lambda i: (0, 0))],
        out_specs=[pl.BlockSpec((ROUTER_TM, LANES), lambda i: (i, 0)),
                   pl.BlockSpec((8, LANES), lambda i: (0, 0))],
        out_shape=[jax.ShapeDtypeStruct((n, LANES), F32), jax.ShapeDtypeStruct((8, LANES), F32)],
        scratch_shapes=[pltpu.VMEM((1, LANES), F32)],
        compiler_params=_cp(("arbitrary",)),
        name="moe_router",
    )(x, wp, bp)


GATHER_ROWS = 512


def _row_slabs(x):
    return x.reshape(x.shape[0], x.shape[1] // LANES, LANES)


def _row_gather_kernel(idx_ref, x_hbm, o_ref, sem):
    def start(r, _):
        pltpu.make_async_copy(x_hbm.at[idx_ref[0, 0, r]], o_ref.at[r], sem).start()
        return 0

    lax.fori_loop(0, GATHER_ROWS, start, 0)
    pltpu.make_async_copy(x_hbm.at[pl.ds(0, GATHER_ROWS)], o_ref, sem).wait()


def row_gather(x, row_idx):
    n_rows = row_idx.shape[0]
    nt = n_rows // GATHER_ROWS
    x3 = _row_slabs(x)
    out = pl.pallas_call(
        _row_gather_kernel,
        grid=(nt,),
        in_specs=[pl.BlockSpec((1, 1, GATHER_ROWS), lambda i: (i, 0, 0), memory_space=pltpu.SMEM),
                  pl.BlockSpec(memory_space=pl.ANY)],
        out_specs=pl.BlockSpec((GATHER_ROWS,) + x3.shape[1:], lambda i: (i, 0, 0)),
        out_shape=jax.ShapeDtypeStruct((n_rows,) + x3.shape[1:], x.dtype),
        scratch_shapes=[pltpu.SemaphoreType.DMA],
        compiler_params=_cp(("arbitrary",)),
        name="moe_row_gather",
    )(row_idx.reshape(nt, 1, GATHER_ROWS), x3)
    return out.reshape(n_rows, x.shape[1])


def _moe_up_kernel(te_ref, tv_ref, x_ref, wg_ref, wu_ref, o_ref):
    i = pl.program_id(1)

    @pl.when(tv_ref[i] > 0)
    def _():
        x = x_ref[...]
        o_ref[...] = (_silu(_dot(x, wg_ref[...])) * _dot(x, wu_ref[...])).astype(o_ref.dtype)

    @pl.when(tv_ref[i] == 0)
    def _():
        o_ref[...] = jnp.zeros_like(o_ref)


def moe_up(xs, w_gate, w_up, tile_e, tile_v, tn):
    n_rows, d = xs.shape
    f = w_gate.shape[2]
    nt = n_rows // MOE_TM
    grid_spec = pltpu.PrefetchScalarGridSpec(
        num_scalar_prefetch=2,
        grid=(f // tn, nt),
        in_specs=[pl.BlockSpec((MOE_TM, d), lambda j, i, te, tv: (i, 0)),
                  pl.BlockSpec((None, d, tn), lambda j, i, te, tv: (te[i], 0, j)),
                  pl.BlockSpec((None, d, tn), lambda j, i, te, tv: (te[i], 0, j))],
        out_specs=pl.BlockSpec((MOE_TM, tn), lambda j, i, te, tv: (i, j)),
    )
    return pl.pallas_call(
        _moe_up_kernel,
        grid_spec=grid_spec,
        out_shape=jax.ShapeDtypeStruct((n_rows, f), BF),
        compiler_params=_cp(("parallel", "arbitrary")),
        name="moe_up",
    )(tile_e, tile_v, xs, w_gate, w_up)


def _moe_down_kernel(te_ref, tv_ref, h_ref, wd_ref, o_ref):
    i = pl.program_id(1)

    @pl.when(tv_ref[i] > 0)
    def _():
        o_ref[...] = _dot(h_ref[...], wd_ref[...])

    @pl.when(tv_ref[i] == 0)
    def _():
        o_ref[...] = jnp.zeros_like(o_ref)


def moe_down(h, w_down, tile_e, tile_v, tn):
    n_rows, f = h.shape
    d = w_down.shape[2]
    nt = n_rows // MOE_TM
    grid_spec = pltpu.PrefetchScalarGridSpec(
        num_scalar_prefetch=2,
        grid=(d // tn, nt),
        in_specs=[pl.BlockSpec((MOE_TM, f), lambda j, i, te, tv: (i, 0)),
                  pl.BlockSpec((None, f, tn), lambda j, i, te, tv: (te[i], 0, j))],
        out_specs=pl.BlockSpec((MOE_TM, tn), lambda j, i, te, tv: (i, j)),
    )
    return pl.pallas_call(
        _moe_down_kernel,
        grid_spec=grid_spec,
        out_shape=jax.ShapeDtypeStruct((n_rows, d), F32),
        compiler_params=_cp(("parallel", "arbitrary")),
        name="moe_down",
    )(tile_e, tile_v, h, w_down)


COMBINE_TM = 128


def _moe_combine_kernel(d1_ref, d2_ref, y_hbm, x_ref, rw_ref, g_ref, b_ref, o_ref, buf, sem):
    nslab = buf.shape[2]

    def start(r, _):
        pltpu.make_async_copy(y_hbm.at[d1_ref[0, 0, r]], buf.at[0, r], sem).start()
        pltpu.make_async_copy(y_hbm.at[d2_ref[0, 0, r]], buf.at[1, r], sem).start()
        return 0

    lax.fori_loop(0, COMBINE_TM, start, 0)
    for k in range(2):
        pltpu.make_async_copy(y_hbm.at[pl.ds(0, COMBINE_TM)], buf.at[k], sem).wait()
    rw = rw_ref[...]
    w1 = rw[:, 2:3]
    w2 = rw[:, 3:4]
    z = []
    tot = jnp.zeros((COMBINE_TM, 1), F32)
    for c in range(nslab):
        sl = slice(c * LANES, (c + 1) * LANES)
        zc = ALPHA * x_ref[:, sl] + (w1 * buf[0, :, c, :] + w2 * buf[1, :, c, :])
        z.append(zc)
        tot = tot + jnp.sum(zc, axis=-1, keepdims=True)
    d = nslab * LANES
    mu = tot * (1.0 / d)
    ss = jnp.zeros((COMBINE_TM, 1), F32)
    for c in range(nslab):
        z[c] = z[c] - mu
        ss = ss + jnp.sum(z[c] * z[c], axis=-1, keepdims=True)
    r = lax.rsqrt(ss * (1.0 / d) + LN_EPS)
    for c in range(nslab):
        sl = slice(c * LANES, (c + 1) * LANES)
        o_ref[:, sl] = z[c] * r * g_ref[:, sl] + b_ref[:, sl]


def moe_combine(y_rows, x, route, dest1, dest2, g, beta):
    n, d = x.shape
    nt = n // COMBINE_TM
    y3 = _row_slabs(y_rows)
    idx_spec = pl.BlockSpec((1, 1, COMBINE_TM), lambda i: (i, 0, 0), memory_space=pltpu.SMEM)
    return pl.pallas_call(
        _moe_combine_kernel,
        grid=(nt,),
        in_specs=[idx_spec, idx_spec,
                  pl.BlockSpec(memory_space=pl.ANY),
                  pl.BlockSpec((COMBINE_TM, d), lambda i: (i, 0)),
                  pl.BlockSpec((COMBINE_TM, LANES), lambda i: (i, 0)),
                  pl.BlockSpec((1, d), lambda i: (0, 0)),
                  pl.BlockSpec((1, d), lambda i: (0, 0))],
        out_specs=pl.BlockSpec((COMBINE_TM, d), lambda i: (i, 0)),
        out_shape=jax.ShapeDtypeStruct((n, d), F32),
        scratch_shapes=[pltpu.VMEM((2, COMBINE_TM) + y3.shape[1:], F32), pltpu.SemaphoreType.DMA],
        compiler_params=_cp(("arbitrary",)),
        name="moe_combine",
    )(dest1.reshape(nt, 1, COMBINE_TM), dest2.reshape(nt, 1, COMBINE_TM), y3, x, route,
      g.reshape(1, d).astype(F32), beta.reshape(1, d).astype(F32))


def moe_layer(x, xb, w_router, b_router, w_gate, w_up, w_down, g, beta):
    n, d = x.shape
    route, cnt = moe_router(x, w_router, b_router)
    e1 = route[:, 0].astype(I32)
    e2 = route[:, 1].astype(I32)
    counts = cnt[0, :N_EXPERTS].astype(I32)
    padded = (counts + MOE_TM - 1) // MOE_TM * MOE_TM
    pad_end = jnp.cumsum(padded)
    pad_start = pad_end - padded
    dest1 = pad_start[e1] + route[:, 4].astype(I32)
    dest2 = pad_start[e2] + route[:, 5].astype(I32)
    n_rows = 2 * n + N_EXPERTS * MOE_TM
    nt = n_rows // MOE_TM
    tok = jnp.arange(n, dtype=I32)
    row_tok = jnp.zeros((n_rows,), I32).at[dest1].set(tok).at[dest2].set(tok)
    tile_start = jnp.arange(nt, dtype=I32) * MOE_TM
    tile_e = jnp.minimum(jnp.searchsorted(pad_end, tile_start, side='right'), N_EXPERTS - 1).astype(I32)
    tile_v = (tile_start < pad_end[-1]).astype(I32)
    xs = row_gather(xb, row_tok)
    h = moe_up(xs, w_gate, w_up, tile_e, tile_v, tn=min(1024, w_gate.shape[2]))
    y_rows = moe_down(h, w_down, tile_e, tile_v, tn=min(512, d))
    return moe_combine(y_rows, x, route, dest1, dest2, g, beta)


def _cmp_sel_overlap(n_cmp_pad, n_sel):
    i = np.arange(n_cmp_pad)[:, None]
    j = np.arange(LANES)[None, :]
    lo = np.maximum(i * CMP_STRIDE, j * SEL_LEN)
    hi = np.minimum(i * CMP_STRIDE + CMP_LEN, (j + 1) * SEL_LEN)
    ov = np.maximum(hi - lo, 0) / CMP_LEN
    ov[:, n_sel:] = 0.0
    return ov.astype(np.float32)


def _gate_columns():
    src = -np.ones((KV_GROUPS * LANES,), np.int64)
    for g in range(KV_GROUPS):
        for j in range(HPG):
            for br in range(3):
                src[g * LANES + br * HPG + j] = g * HPG * 3 + j * 3 + br
    return src


def nsa_layer(xb, b, s, w_in, pe_k, w1_k, w2_k, pe_v, w1_v, w2_v, band, cmpb):
    n = xb.shape[0]
    hd = N_HEADS * HEAD_DIM
    gw = KV_GROUPS * HEAD_DIM
    w_main = w_in[:, :hd + 6 * gw].astype(BF)
    scale = jnp.concatenate([jnp.full((hd,), HEAD_DIM ** -0.5 * LOG2E, F32), jnp.ones((6 * gw,), F32)])
    ybf = matmul_scaled(xb, w_main, scale, BF, tm=512, tn=512)
    src = _gate_columns()
    w_gl = jnp.where(jnp.asarray(src >= 0), w_in[:, hd + 6 * gw:][:, np.maximum(src, 0)], 0.0).astype(BF)
    gl = matmul_scaled(xb, w_gl, jnp.ones((w_gl.shape[1],), F32), F32, tm=512, tn=512)
    cb = hd // gw
    kcmp = nsa_compress(ybf, b, s, cb + 0, pe_k, w1_k, w2_k)
    vcmp = nsa_compress(ybf, b, s, cb + 1, pe_v, w1_v, w2_v)
    overlap_t = jnp.asarray(_cmp_sel_overlap(s // CMP_STRIDE, s // SEL_LEN).T).astype(BF)
    oc, nsel = nsa_cmp_select(ybf, kcmp, vcmp, cmpb, overlap_t, b, s)
    c128 = hd // HEAD_DIM
    o = nsa_main(ybf, nsel, band, gl, oc, b, s,
                 col_ksl=c128 + 2 * KV_GROUPS, col_vsl=c128 + 3 * KV_GROUPS,
                 col_kw=c128 + 4 * KV_GROUPS, col_vw=c128 + 5 * KV_GROUPS)
    return o.reshape(n, hd)


def dsa_layer(xb, b, s, w_in, kv_norm_g, w_uk, w_uv, band):
    n = xb.shape[0]
    hd = N_HEADS * HEAD_DIM
    o_c = hd
    o_qi = hd + KV_RANK
    o_ki = o_qi + IDX_HEADS * IDX_DIM
    o_wi = o_ki + IDX_DIM
    w_q = jnp.concatenate([w_in[:, :hd], w_in[:, o_qi:o_ki]], axis=1).astype(BF)
    scale_q = jnp.concatenate([jnp.ones((hd,), F32), jnp.full((IDX_HEADS * IDX_DIM,), IDX_DIM ** -0.5, F32)])
    qbf = matmul_scaled(xb, w_q, scale_q, BF, tm=512, tn=512)
    cn = matmul_rmsnorm(xb, w_in[:, o_c:o_qi].astype(BF), kv_norm_g, tm=512)
    w_kw = jnp.pad(w_in[:, o_ki:], ((0, 0), (0, LANES - IDX_DIM - IDX_HEADS))).astype(BF)
    scale_kw = jnp.concatenate([jnp.ones((IDX_DIM,), F32), jnp.full((IDX_HEADS,), IDX_HEADS ** -0.5, F32),
                                jnp.zeros((LANES - IDX_DIM - IDX_HEADS,), F32)])
    kw = matmul_scaled(xb, w_kw, scale_kw, F32, tm=512, tn=LANES)
    nmask = dsa_index(qbf, kw, b, s, col_qi=hd // (IDX_HEADS * IDX_DIM))
    o = dsa_attention(qbf, cn, nmask, band[:, :2], w_uk, w_uv, b, s)
    return o.reshape(n, hd)


def kernel(x, rel_bias, nsa_w_in, nsa_cmp_pe_k, nsa_cmp_w1_k, nsa_cmp_w2_k, nsa_cmp_pe_v, nsa_cmp_w1_v, nsa_cmp_w2_v, nsa_w_out, dsa_w_in, dsa_kv_norm_g, dsa_w_uk, dsa_w_uv, dsa_w_out, ffn_w_gate, ffn_w_up, ffn_w_down, moe_w_router, moe_b_router, moe_w_gate, moe_w_up, moe_w_down, ln_mix_g, ln_mix_b, ln_ffn_g, ln_ffn_b):
    b, s, d = x.shape
    n = b * s
    assert s % TQ == 0 and s // SEL_LEN <= LANES
    x0 = x.reshape(n, d)
    band, cmpb = _bias_tables(rel_bias)
    o = nsa_layer(x0.astype(BF), b, s, nsa_w_in, nsa_cmp_pe_k, nsa_cmp_w1_k, nsa_cmp_w2_k,
                  nsa_cmp_pe_v, nsa_cmp_w1_v, nsa_cmp_w2_v, band, cmpb)
    x1, x1b = matmul_residual_ln(o, nsa_w_out.astype(BF), x0, ln_mix_g[0], ln_mix_b[0], tm=256, tk=d)
    hff = swiglu_up(x1b, ffn_w_gate.astype(BF), ffn_w_up.astype(BF), tm=512, tn=512)
    x2, x2b = matmul_residual_ln(hff, ffn_w_down.astype(BF), x1, ln_ffn_g[0], ln_ffn_b[0], tm=256,
                                 tk=hff.shape[1] // 2)
    o = dsa_layer(x2b, b, s, dsa_w_in, dsa_kv_norm_g, dsa_w_uk, dsa_w_uv, band)
    x3, x3b = matmul_residual_ln(o, dsa_w_out.astype(BF), x2, ln_mix_g[1], ln_mix_b[1], tm=256, tk=d)
    out = moe_layer(x3, x3b, moe_w_router, moe_b_router, moe_w_gate, moe_w_up, moe_w_down,
                    ln_ffn_g[1], ln_ffn_b[1])
    return out.reshape(b, s, d)
```

```python
import functools
import math

import numpy as np
import jax
import jax.numpy as jnp
from jax import lax
from jax.experimental import pallas as pl
from jax.experimental.pallas import tpu as pltpu

F32 = jnp.float32
BF = jnp.bfloat16
I32 = jnp.int32

N_HEADS = 16
HEAD_DIM = 128
KV_GROUPS = 4
HPG = N_HEADS // KV_GROUPS
CMP_LEN = 32
CMP_STRIDE = 16
SEL_LEN = 64
SEL_TOP = 16
WINDOW = 512
SEL_FORCE = 1e4
KV_RANK = 256
IDX_HEADS = 8
IDX_DIM = 64
IDX_TOPK = 256
REL_BUCKETS = 32
REL_MAX_DIST = 128
N_EXPERTS = 8
DEPTH = 2
ALPHA = (2 * DEPTH) ** 0.25
LN_EPS = 1e-5
RMS_EPS = 1e-6
NEG_INF = -1e30

LOG2E = math.log2(math.e)
BIG = float(2.0 ** 100)
M_INIT = -3.0e38
TQ = 256
LANES = 128
VMEM_LIMIT_BYTES = 60000 * 1024


def _cp(sem, vmem=None):
    return pltpu.CompilerParams(dimension_semantics=sem, vmem_limit_bytes=vmem or VMEM_LIMIT_BYTES)


def _dot(a, b):
    return jnp.dot(a.astype(BF), b.astype(BF), preferred_element_type=F32)


def _dot_nt(a, b):
    return lax.dot_general(a.astype(BF), b.astype(BF), (((1,), (1,)), ((), ())),
                           preferred_element_type=F32)


def _layer_norm_rows(z, g, b):
    mu = jnp.mean(z, axis=-1, keepdims=True)
    zc = z - mu
    var = jnp.mean(zc * zc, axis=-1, keepdims=True)
    return zc * lax.rsqrt(var + LN_EPS) * g + b


def _mm_scale_kernel(a_ref, b_ref, s_ref, o_ref):
    o_ref[...] = (_dot(a_ref[...], b_ref[...]) * s_ref[...]).astype(o_ref.dtype)


def matmul_scaled(a, b, scale, out_dtype, tm, tn):
    m, k = a.shape
    n = b.shape[1]
    return pl.pallas_call(
        _mm_scale_kernel,
        grid=(m // tm, n // tn),
        in_specs=[pl.BlockSpec((tm, k), lambda i, j: (i, 0)),
                  pl.BlockSpec((k, tn), lambda i, j: (0, j)),
                  pl.BlockSpec((1, tn), lambda i, j: (0, j))],
        out_specs=pl.BlockSpec((tm, tn), lambda i, j: (i, j)),
        out_shape=jax.ShapeDtypeStruct((m, n), out_dtype),
        compiler_params=_cp(("parallel", "parallel")),
        name="matmul_scaled",
    )(a, b, scale.reshape(1, n).astype(F32))


def _mm_rms_kernel(a_ref, b_ref, g_ref, o_ref):
    c = _dot(a_ref[...], b_ref[...])
    r = lax.rsqrt(jnp.mean(c * c, axis=-1, keepdims=True) + RMS_EPS)
    o_ref[...] = (c * r * g_ref[...]).astype(o_ref.dtype)


def matmul_rmsnorm(a, b, g, tm):
    m, k = a.shape
    n = b.shape[1]
    return pl.pallas_call(
        _mm_rms_kernel,
        grid=(m // tm,),
        in_specs=[pl.BlockSpec((tm, k), lambda i: (i, 0)),
                  pl.BlockSpec((k, n), lambda i: (0, 0)),
                  pl.BlockSpec((1, n), lambda i: (0, 0))],
        out_specs=pl.BlockSpec((tm, n), lambda i: (i, 0)),
        out_shape=jax.ShapeDtypeStruct((m, n), BF),
        compiler_params=_cp(("parallel",)),
        name="matmul_rmsnorm",
    )(a, b, g.reshape(1, n).astype(F32))


def _mm_ln_kernel(a_ref, b_ref, r_ref, g_ref, be_ref, of_ref, ob_ref, acc_ref, *, nk):
    kk = pl.program_id(1)

    @pl.when(kk == 0)
    def _():
        acc_ref[...] = jnp.zeros_like(acc_ref)

    acc_ref[...] += _dot(a_ref[...], b_ref[...])

    @pl.when(kk == nk - 1)
    def _():
        y = _layer_norm_rows(ALPHA * r_ref[...] + acc_ref[...], g_ref[...], be_ref[...])
        of_ref[...] = y
        ob_ref[...] = y.astype(BF)


def matmul_residual_ln(a, b, res, g, beta, tm, tk):
    m, k = a.shape
    n = b.shape[1]
    nk = k // tk
    return pl.pallas_call(
        functools.partial(_mm_ln_kernel, nk=nk),
        grid=(m // tm, nk),
        in_specs=[pl.BlockSpec((tm, tk), lambda i, kk: (i, kk)),
                  pl.BlockSpec((tk, n), lambda i, kk: (kk, 0)),
                  pl.BlockSpec((tm, n), lambda i, kk: (i, 0)),
                  pl.BlockSpec((1, n), lambda i, kk: (0, 0)),
                  pl.BlockSpec((1, n), lambda i, kk: (0, 0))],
        out_specs=[pl.BlockSpec((tm, n), lambda i, kk: (i, 0)),
                   pl.BlockSpec((tm, n), lambda i, kk: (i, 0))],
        out_shape=[jax.ShapeDtypeStruct((m, n), F32), jax.ShapeDtypeStruct((m, n), BF)],
        scratch_shapes=[pltpu.VMEM((tm, n), F32)],
        compiler_params=_cp(("parallel", "arbitrary")),
        name="matmul_residual_ln",
    )(a, b, res, g.reshape(1, n).astype(F32), beta.reshape(1, n).astype(F32))


def _silu(x):
    return x * (1.0 / (1.0 + jnp.exp(-x)))


def _swiglu_up_kernel(a_ref, wg_ref, wu_ref, o_ref):
    a = a_ref[...]
    o_ref[...] = (_silu(_dot(a, wg_ref[...])) * _dot(a, wu_ref[...])).astype(o_ref.dtype)


def swiglu_up(a, wg, wu, tm, tn):
    m, k = a.shape
    n = wg.shape[1]
    return pl.pallas_call(
        _swiglu_up_kernel,
        grid=(m // tm, n // tn),
        in_specs=[pl.BlockSpec((tm, k), lambda i, j: (i, 0)),
                  pl.BlockSpec((k, tn), lambda i, j: (0, j)),
                  pl.BlockSpec((k, tn), lambda i, j: (0, j))],
        out_specs=pl.BlockSpec((tm, tn), lambda i, j: (i, j)),
        out_shape=jax.ShapeDtypeStruct((m, n), BF),
        compiler_params=_cp(("parallel", "parallel")),
        name="swiglu_up",
    )(a, wg, wu)


def _bucket_of_distance():
    n = np.arange(REL_MAX_DIST + 1)
    exact = REL_BUCKETS // 2
    nf = np.maximum(n, exact).astype(np.float64)
    large = exact + (np.log(nf / exact) / math.log(REL_MAX_DIST / exact) * (REL_BUCKETS - exact)).astype(np.int64)
    return np.where(n < exact, n, np.minimum(large, REL_BUCKETS - 1)).astype(np.int32)


def _bucket_starts():
    bk = _bucket_of_distance()
    return [int(np.argmax(bk >= b)) for b in range(REL_BUCKETS)]


def _bias_kernel(tab_ref, band_ref, cmp_ref, *, starts):
    h = pl.program_id(0)
    far = tab_ref[REL_BUCKETS - 1, h]

    def lookup(d):
        val = jnp.zeros(d.shape, F32)
        for b in range(REL_BUCKETS - 2, -1, -1):
            val = jnp.where(d < starts[b + 1], (tab_ref[b, h] - far) * LOG2E, val)
        return val

    j = lax.broadcasted_iota(I32, (TQ, TQ), 0)
    i = lax.broadcasted_iota(I32, (TQ, TQ), 1)
    for r in range(3):
        d = TQ * r + i - j
        band_ref[r] = jnp.where((d < 0) | (d >= WINDOW), -BIG, lookup(d))
    c = lax.broadcasted_iota(I32, (CMP_NEAR, TQ), 0)
    i = lax.broadcasted_iota(I32, (CMP_NEAR, TQ), 1)
    d = i - CMP_STRIDE * (c - CMP_NEAR // 2) - (CMP_LEN - 1)
    cmp_ref[...] = jnp.where(d < 0, -BIG, lookup(d))


CMP_NEAR = 2 * (TQ // CMP_STRIDE)


def _bias_tables(rel_bias):
    return pl.pallas_call(
        functools.partial(_bias_kernel, starts=_bucket_starts()),
        grid=(N_HEADS,),
        in_specs=[pl.BlockSpec(memory_space=pltpu.SMEM)],
        out_specs=[pl.BlockSpec((None, 3, TQ, TQ), lambda h: (h // HPG, 0, 0, h % HPG)),
                   pl.BlockSpec((None, CMP_NEAR, TQ), lambda h: (h // HPG, 0, h % HPG))],
        out_shape=[jax.ShapeDtypeStruct((N_HEADS // HPG, 3, TQ, HPG * TQ), F32),
                   jax.ShapeDtypeStruct((N_HEADS // HPG, CMP_NEAR, HPG * TQ), F32)],
        compiler_params=_cp(("parallel",)),
        name="bias_tables",
    )(rel_bias.astype(F32))


def _heads_to_rows(q):
    return jnp.concatenate([q[:, j * HEAD_DIM:(j + 1) * HEAD_DIM] for j in range(HPG)], axis=0)


def _causal_chunk_attention(qt, score_fn, extra_fn, vt_fn, band_ref, acc_ref, sbuf_ref, pbuf_ref):
    rows = HPG * TQ
    sa_ref, sb_ref = sbuf_ref.at[0], sbuf_ref.at[1]
    pa_ref, pb_ref = pbuf_ref.at[0], pbuf_ref.at[1]

    def softmax_stage(m, s_ref, p_ref, kc, bias=None):
        st = s_ref[...]
        extra = extra_fn(kc)
        if extra is not None:
            st = st + extra
        if bias is not None:
            st = st + bias
        m_new = jnp.maximum(m, jnp.max(st, axis=0, keepdims=True))
        p_ref[...] = jnp.exp2(st - m_new).astype(BF)
        return m_new, jnp.exp2(m - m_new)

    def value_stage(alpha, p_ref, kc):
        acc_ref[...] = alpha * acc_ref[...] + jnp.dot(vt_fn(kc), p_ref[...], preferred_element_type=F32)

    def far_pair(i, carry):
        m, alpha_b = carry
        a = 2 * i
        sb_ref[...] = score_fn(a + 1)
        value_stage(alpha_b, pb_ref, jnp.maximum(a - 1, 0))
        m, alpha_a = softmax_stage(m, sa_ref, pa_ref, a)
        sa_ref[...] = score_fn(a + 2)
        value_stage(alpha_a, pa_ref, a)
        m, alpha_b = softmax_stage(m, sb_ref, pb_ref, a + 1)
        return m, alpha_b

    acc_ref[...] = jnp.zeros_like(acc_ref)
    pb_ref[...] = jnp.zeros_like(pb_ref)
    sa_ref[...] = score_fn(0)
    npair = lax.shift_right_logical(jnp.maximum(qt - 1, 0), 1)
    m, alpha_b = lax.fori_loop(0, npair, far_pair,
                               (jnp.full((1, rows), M_INIT, F32), jnp.ones((1, rows), F32)))
    a = 2 * npair
    odd = (qt & 1) == 1
    first = qt == 0
    sb_ref[...] = score_fn(a + 1)
    value_stage(alpha_b, pb_ref, jnp.maximum(a - 1, 0))
    m, alpha_a = softmax_stage(m, sa_ref, pa_ref, a,
                               jnp.where(odd, band_ref[1], jnp.where(first, band_ref[0], 0.0)))
    value_stage(alpha_a, pa_ref, a)
    m, alpha_b = softmax_stage(m, sb_ref, pb_ref, a + 1,
                               jnp.where(odd, band_ref[0], jnp.where(first, -BIG, band_ref[1])))
    value_stage(alpha_b, pb_ref, a + 1)

    @pl.when((qt >= 2) & jnp.logical_not(odd))
    def _():
        sa_ref[...] = score_fn(qt)
        _, alpha_a = softmax_stage(m, sa_ref, pa_ref, qt, band_ref[0])
        value_stage(alpha_a, pa_ref, qt)


ONES_ROWS = 16


def _store_transposed(src_ref, dst_ref, nchunk):
    dv = src_ref.shape[1]
    tail = jnp.where(lax.broadcasted_iota(I32, (ONES_ROWS, TQ), 0) == 0, 1.0, 0.0).astype(dst_ref.dtype)

    def body(kc, _):
        start = pl.multiple_of(kc * TQ, TQ)
        dst_ref[kc, 0:dv, :] = src_ref[pl.ds(start, TQ), :].astype(F32).T.astype(dst_ref.dtype)
        dst_ref[kc, dv:dv + ONES_ROWS, :] = tail
        return 0

    lax.fori_loop(0, nchunk, body, 0)


def _gelu_tanh(x):
    return 0.5 * x * (1.0 + jnp.tanh(math.sqrt(2.0 / math.pi) * (x + 0.044715 * (x * x * x))))


def _compress_kernel(*refs, nch):
    x_refs = refs[:CMP_STRIDE]
    pe_ref, w1_ref, w2_ref, o_ref = refs[CMP_STRIDE:]
    acc_a = [jnp.zeros((nch, HEAD_DIM), F32) for _ in range(KV_GROUPS)]
    acc_b = [jnp.zeros((nch, HEAD_DIM), F32) for _ in range(KV_GROUPS)]
    for l in range(CMP_STRIDE):
        x = x_refs[l][...].astype(F32)
        xa = (x + pe_ref[l:l + 1, :]).astype(BF)
        xb = (x + pe_ref[CMP_STRIDE + l:CMP_STRIDE + l + 1, :]).astype(BF)
        for g in range(KV_GROUPS):
            sl = slice(g * HEAD_DIM, (g + 1) * HEAD_DIM)
            acc_a[g] = acc_a[g] + jnp.dot(xa[:, sl], w1_ref[l], preferred_element_type=F32)
            acc_b[g] = acc_b[g] + jnp.dot(xb[:, sl], w1_ref[CMP_STRIDE + l], preferred_element_type=F32)
    for g in range(KV_GROUPS):
        pre = acc_a[g] + pltpu.roll(acc_b[g], nch - 1, axis=0)
        hid = _gelu_tanh(pre).astype(BF)
        o_ref[:, g * HEAD_DIM:(g + 1) * HEAD_DIM] = jnp.dot(hid, w2_ref[...], preferred_element_type=F32).astype(BF)


def nsa_compress(ybf, b, s, col_block, pe, w1, w2):
    ncols = ybf.shape[1]
    nch = s // CMP_STRIDE
    blk_w = KV_GROUPS * HEAD_DIM
    per_tok = ncols // blk_w
    y3 = ybf.reshape(b, nch, CMP_STRIDE * ncols)
    pe_t = jnp.tile(pe.astype(F32), (1, KV_GROUPS))
    in_specs = [pl.BlockSpec((None, nch, blk_w), (lambda bi, l=l: (bi, 0, l * per_tok + col_block)))
                for l in range(CMP_STRIDE)]
    in_specs += [pl.BlockSpec((CMP_LEN, blk_w), lambda bi: (0, 0)),
                 pl.BlockSpec((CMP_LEN, HEAD_DIM, HEAD_DIM), lambda bi: (0, 0, 0)),
                 pl.BlockSpec((HEAD_DIM, HEAD_DIM), lambda bi: (0, 0))]
    return pl.pallas_call(
        functools.partial(_compress_kernel, nch=nch),
        grid=(b,),
        in_specs=in_specs,
        out_specs=pl.BlockSpec((None, nch, blk_w), lambda bi: (bi, 0, 0)),
        out_shape=jax.ShapeDtypeStruct((b, nch, blk_w), BF),
        compiler_params=_cp(("parallel",)),
        name="nsa_compress",
    )(*([y3] * CMP_STRIDE), pe_t, w1.astype(BF), w2.astype(BF))


def _nsa_cmp_kernel(q_ref, kc_ref, vc_ref, cb_ref, ovt_ref, oc_ref, ns_ref, vct_ref, s_ref, *, n_sel, n_top):
    qt = pl.program_id(2)
    t0 = qt * TQ
    ncp = kc_ref.shape[0]
    pad = CMP_NEAR // 2

    @pl.when(qt == 0)
    def _():
        vct_ref[...] = vc_ref[...].astype(F32).T.astype(BF)
        s_ref[0:pad, :] = jnp.zeros((pad, HPG * TQ), F32)

    qs = _heads_to_rows(q_ref[...])
    s_ref[pad:pad + ncp, :] = _dot_nt(kc_ref[...], qs)
    near = pl.multiple_of(qt * (TQ // CMP_STRIDE), TQ // CMP_STRIDE)
    s_ref[pl.ds(near, CMP_NEAR), :] = s_ref[pl.ds(near, CMP_NEAR), :] + cb_ref[...]
    s = s_ref[pad:pad + ncp, :]
    key = lax.broadcasted_iota(I32, (ncp, HPG * TQ), 0)
    tq = t0 + (lax.broadcasted_iota(I32, (ncp, HPG * TQ), 1) & (TQ - 1))
    vis = (key * CMP_STRIDE + (CMP_LEN - 1)) <= tq
    s = jnp.where(vis, s, -BIG)
    m = jnp.max(s, axis=0, keepdims=True)
    p = jnp.where(vis, jnp.exp2(s - m), 0.0)
    l = jnp.sum(p, axis=0, keepdims=True)
    p = p * jnp.where(l > 0.0, 1.0 / l, 0.0)
    pb = p.astype(BF)
    oct = jnp.dot(vct_ref[...], pb, preferred_element_type=F32)
    score = jnp.zeros((LANES, TQ), F32)
    for j in range(HPG):
        cs = slice(j * TQ, (j + 1) * TQ)
        oc_ref[:, j * HEAD_DIM:(j + 1) * HEAD_DIM] = oct[:, cs].T.astype(oc_ref.dtype)
        score = score + jnp.dot(ovt_ref[...], pb[:, cs], preferred_element_type=F32)
    blk = lax.broadcasted_iota(I32, (LANES, TQ), 0)
    t = t0 + lax.broadcasted_iota(I32, (LANES, TQ), 1)
    cur = lax.shift_right_logical(t, int(math.log2(SEL_LEN)))
    forced = (blk == 0) | (blk == cur) | (blk == cur - 1)
    visible = blk * SEL_LEN <= t
    sc = jnp.where(forced, SEL_FORCE, jnp.where(visible, score, -1.0))
    sc = jnp.where(blk < n_sel, sc, -2.0)
    blkf = blk.astype(F32)

    def pick_one(_, carry):
        sc, sel = carry
        mx = jnp.max(sc, axis=0, keepdims=True)
        first = jnp.min(jnp.where(sc == mx, blkf, float(LANES)), axis=0, keepdims=True)
        hit = blkf == first
        return jnp.where(hit, -3.0, sc), jnp.where(hit, 1.0, sel)

    _, sel = lax.fori_loop(0, n_top, pick_one, (sc, jnp.zeros((LANES, TQ), F32)))
    ns_ref[...] = jnp.where(sel.T > 0.5, 0.0, BIG).astype(BF)


def nsa_cmp_select(ybf, kcmp, vcmp, cmpb, overlap_t, b, s):
    ncp = kcmp.shape[1]
    n_sel = s // SEL_LEN
    n_top = min(SEL_TOP, n_sel)
    y3 = ybf.reshape(b, s, ybf.shape[1])
    gw = HPG * HEAD_DIM
    return pl.pallas_call(
        functools.partial(_nsa_cmp_kernel, n_sel=n_sel, n_top=n_top),
        grid=(b, KV_GROUPS, s // TQ),
        in_specs=[pl.BlockSpec((None, TQ, gw), lambda bi, g, qt: (bi, qt, g)),
                  pl.BlockSpec((None, ncp, HEAD_DIM), lambda bi, g, qt: (bi, 0, g)),
                  pl.BlockSpec((None, ncp, HEAD_DIM), lambda bi, g, qt: (bi, 0, g)),
                  pl.BlockSpec((None, CMP_NEAR, HPG * TQ), lambda bi, g, qt: (g, 0, 0)),
                  pl.BlockSpec((LANES, ncp), lambda bi, g, qt: (0, 0))],
        out_specs=[pl.BlockSpec((None, TQ, gw), lambda bi, g, qt: (bi, qt, g)),
                   pl.BlockSpec((None, None, TQ, LANES), lambda bi, g, qt: (bi, g, qt, 0))],
        out_shape=[jax.ShapeDtypeStruct((b, s, KV_GROUPS * gw), BF),
                   jax.ShapeDtypeStruct((b, KV_GROUPS, s, LANES), BF)],
        scratch_shapes=[pltpu.VMEM((HEAD_DIM, ncp), BF),
                        pltpu.VMEM((ncp + CMP_NEAR, HPG * TQ), F32)],
        compiler_params=_cp(("parallel", "parallel", "arbitrary")),
        name="nsa_cmp_select",
    )(y3, kcmp, vcmp, cmpb, overlap_t)


def _nsa_main_kernel(q_ref, ks_ref, vs_ref, kw_ref, vw_ref, ns_ref, band_ref, gl_ref, oc_ref, o_ref,
                     vst_ref, vwt_ref, accs_ref, accw_ref, sbuf_ref, pbuf_ref, *, nchunk):
    qt = pl.program_id(2)

    @pl.when(qt == 0)
    def _():
        _store_transposed(vs_ref, vst_ref, nchunk)
        _store_transposed(vw_ref, vwt_ref, nchunk)

    q = q_ref[...]
    ns = ns_ref[...]
    qs = _heads_to_rows(q)
    qp = jnp.concatenate([qs, jnp.concatenate([ns] * HPG, axis=0)], axis=1)

    def sel_scores(kc):
        start = pl.multiple_of(kc * TQ, TQ)
        k = ks_ref[pl.ds(start, TQ), :]
        krow = lax.broadcasted_iota(I32, (TQ, LANES), 0)
        klane = lax.broadcasted_iota(I32, (TQ, LANES), 1)
        kblk = lax.shift_right_logical(krow, int(math.log2(SEL_LEN))) + kc * (TQ // SEL_LEN)
        oh = jnp.where(klane == kblk, -1.0, 0.0).astype(BF)
        return _dot_nt(jnp.concatenate([k, oh], axis=1), qp)

    _causal_chunk_attention(qt, sel_scores, lambda kc: None, lambda kc: vst_ref[kc],
                            band_ref, accs_ref, sbuf_ref, pbuf_ref)

    accw_ref[...] = jnp.zeros_like(accw_ref)
    m_w = [jnp.full((1, TQ), M_INIT, F32)] * HPG
    for r in (2, 1, 0):
        kc = jnp.maximum(qt - r, 0)
        start = pl.multiple_of(kc * TQ, TQ)
        st_all = _dot_nt(kw_ref[pl.ds(start, TQ), :], qs) + jnp.where(qt >= r, band_ref[r], -BIG)
        for j in range(HPG):
            cs = slice(j * TQ, (j + 1) * TQ)
            st = st_all[:, cs]
            m_new = jnp.maximum(m_w[j], jnp.max(st, axis=0, keepdims=True))
            alpha = jnp.exp2(m_w[j] - m_new)
            p = jnp.exp2(st - m_new).astype(BF)
            m_w[j] = m_new
            accw_ref[:, cs] = alpha * accw_ref[:, cs] + jnp.dot(vwt_ref[kc], p, preferred_element_type=F32)

    gates = 1.0 / (1.0 + jnp.exp(-gl_ref[...]))
    oc = oc_ref[...].astype(F32)
    for j in range(HPG):
        sl = slice(j * HEAD_DIM, (j + 1) * HEAD_DIM)
        cs = slice(j * TQ, (j + 1) * TQ)
        o_s = (accs_ref[0:HEAD_DIM, cs] * (1.0 / accs_ref[HEAD_DIM:HEAD_DIM + 1, cs])).T
        o_w = (accw_ref[0:HEAD_DIM, cs] * (1.0 / accw_ref[HEAD_DIM:HEAD_DIM + 1, cs])).T
        o = (gates[:, j:j + 1] * oc[:, sl] + gates[:, HPG + j:HPG + j + 1] * o_s
             + gates[:, 2 * HPG + j:2 * HPG + j + 1] * o_w)
        o_ref[:, sl] = o.astype(o_ref.dtype)


def nsa_main(ybf, nsel, band, gl, oc, b, s, col_ksl, col_vsl, col_kw, col_vw):
    y3 = ybf.reshape(b, s, ybf.shape[1])
    gw = HPG * HEAD_DIM
    kv_spec = lambda cb: pl.BlockSpec((None, s, HEAD_DIM), lambda bi, g, qt: (bi, 0, cb + g))
    nchunk = s // TQ
    return pl.pallas_call(
        functools.partial(_nsa_main_kernel, nchunk=nchunk),
        grid=(b, KV_GROUPS, nchunk),
        in_specs=[pl.BlockSpec((None, TQ, gw), lambda bi, g, qt: (bi, qt, g)),
                  kv_spec(col_ksl), kv_spec(col_vsl), kv_spec(col_kw), kv_spec(col_vw),
                  pl.BlockSpec((None, None, TQ, LANES), lambda bi, g, qt: (bi, g, qt, 0)),
                  pl.BlockSpec((None, 3, TQ, HPG * TQ), lambda bi, g, qt: (g, 0, 0, 0)),
                  pl.BlockSpec((None, TQ, LANES), lambda bi, g, qt: (bi, qt, g)),
                  pl.BlockSpec((None, TQ, gw), lambda bi, g, qt: (bi, qt, g))],
        out_specs=pl.BlockSpec((None, TQ, gw), lambda bi, g, qt: (bi, qt, g)),
        out_shape=jax.ShapeDtypeStruct((b, s, KV_GROUPS * gw), BF),
        scratch_shapes=[pltpu.VMEM((nchunk, HEAD_DIM + ONES_ROWS, TQ), BF),
                        pltpu.VMEM((nchunk, HEAD_DIM + ONES_ROWS, TQ), BF),
                        pltpu.VMEM((HEAD_DIM + ONES_ROWS, HPG * TQ), F32),
                        pltpu.VMEM((HEAD_DIM + ONES_ROWS, HPG * TQ), F32),
                        pltpu.VMEM((2, TQ, HPG * TQ), F32), pltpu.VMEM((2, TQ, HPG * TQ), BF)],
        compiler_params=_cp(("parallel", "parallel", "arbitrary")),
        name="nsa_main",
    )(y3, y3, y3, y3, y3, nsel, band, gl.reshape(b, s, gl.shape[1]), oc)


def _dsa_index_kernel(qi_ref, kw_ref, wq_ref, nm_ref, ka_ref, kb_ref, key_ref, *, k_sel, nchunk, idx_bits):
    qt = pl.program_id(1)
    t0 = qt * TQ
    half = LANES // 2

    @pl.when(qt == 0)
    def _():
        lane = lax.broadcasted_iota(I32, kw_ref.shape, 1)
        ka = jnp.where(lane < IDX_DIM, kw_ref[...], 0.0)
        ka_ref[...] = ka.astype(BF)
        kb_ref[...] = pltpu.roll(ka, half, axis=1).astype(BF)

    qi = qi_ref[...]
    npair = IDX_HEADS // 2
    lq = jnp.concatenate([qi[:, p * LANES:(p + 1) * LANES] for p in range(npair)], axis=0)
    wt = wq_ref[...].T
    w_r = [wt[IDX_DIM + h:IDX_DIM + h + 1, :] for h in range(IDX_HEADS)]
    kpos = lax.broadcasted_iota(I32, (TQ, TQ), 0)
    qpos = t0 + lax.broadcasted_iota(I32, (TQ, TQ), 1)

    def score_chunk(kc, _):
        start = pl.multiple_of(kc * TQ, TQ)
        da = _dot_nt(ka_ref[pl.ds(start, TQ), :], lq)
        db = _dot_nt(kb_ref[pl.ds(start, TQ), :], lq)
        sc = jnp.zeros((TQ, TQ), F32)
        for p in range(npair):
            cs = slice(p * TQ, (p + 1) * TQ)
            sc = sc + w_r[2 * p] * jnp.maximum(da[:, cs], 0.0) + w_r[2 * p + 1] * jnp.maximum(db[:, cs], 0.0)
        sc = jnp.where(sc == 0.0, 0.0, sc)
        sc = jnp.where(kc * TQ + kpos <= qpos, sc, NEG_INF)
        bits = lax.bitcast_convert_type(sc, I32)
        key_ref[kc] = jnp.where(bits < 0, bits ^ 0x7FFFFFFF, bits)
        return 0

    nproc = qt + 1
    lax.fori_loop(0, nproc, score_chunk, 0)

    def count(pred):
        def body(kc, acc):
            return acc + jnp.sum(jnp.where(pred(key_ref[kc], kc), 1.0, 0.0), axis=0, keepdims=True)
        return lax.fori_loop(0, nproc, body, jnp.zeros((1, TQ), F32))

    def count_ge(cand):
        return count(lambda k, kc: k >= cand)

    kf = float(k_sel)
    thr0 = jnp.full((1, TQ), -2 ** 31, I32)
    done0 = jnp.where(count_ge(thr0) == kf, 1.0, 0.0)

    def search_cond(state):
        i, _, done = state
        return (i < 32) & (jnp.min(done) < 0.5)

    def search_body(state):
        i, thr, done = state
        cand = thr ^ lax.shift_left(jnp.int32(1), jnp.int32(31) - i)
        cnt = count_ge(cand)
        take = (cnt >= kf) & (done < 0.5)
        thr = jnp.where(take, cand, thr)
        done = jnp.where(take & (cnt == kf), 1.0, done)
        return i + 1, thr, done

    _, thr, done = lax.while_loop(search_cond, search_body, (jnp.int32(0), thr0, done0))

    def tie_break(_):
        need = kf - count_ge(thr + 1)

        def ties_below(bound):
            return count(lambda k, kc: (k == thr) & (kc * TQ + kpos < bound))

        def idx_bit(i, jm):
            cand = jm | lax.shift_left(jnp.int32(1), jnp.int32(idx_bits - 1) - i)
            return jnp.where(ties_below(cand) < need, cand, jm)

        jm = lax.fori_loop(0, idx_bits, idx_bit, jnp.zeros((1, TQ), I32))
        return jnp.where(done > 0.5, jnp.int32(2 ** 30), jm)

    jm = lax.cond(jnp.min(done) < 0.5, tie_break, lambda _: jnp.full((1, TQ), 2 ** 30, I32), 0)

    def write_chunk(kc, _):
        k = key_ref[kc]
        kidx = kc * TQ + kpos
        sel = (kidx <= qpos) & ((k > thr) | ((k == thr) & (kidx <= jm)))
        nm_ref[kc] = jnp.where(sel, 0.0, -BIG).astype(BF)
        return 0

    lax.fori_loop(0, nproc, write_chunk, 0)

    def fill_chunk(kc, _):
        nm_ref[kc] = jnp.full((TQ, TQ), -BIG, BF)
        return 0

    lax.fori_loop(nproc, nchunk, fill_chunk, 0)


def dsa_index(qbf, kw, b, s, col_qi):
    nchunk = s // TQ
    k_sel = min(IDX_TOPK, s // 4)
    q3 = qbf.reshape(b, s, qbf.shape[1])
    kw3 = kw.reshape(b, s, LANES)
    return pl.pallas_call(
        functools.partial(_dsa_index_kernel, k_sel=k_sel, nchunk=nchunk, idx_bits=int(math.log2(s))),
        grid=(b, nchunk),
        in_specs=[pl.BlockSpec((None, TQ, IDX_HEADS * IDX_DIM), lambda bi, qt: (bi, qt, col_qi)),
                  pl.BlockSpec((None, s, LANES), lambda bi, qt: (bi, 0, 0)),
                  pl.BlockSpec((None, TQ, LANES), lambda bi, qt: (bi, qt, 0))],
        out_specs=pl.BlockSpec((None, None, nchunk, TQ, TQ), lambda bi, qt: (bi, qt, 0, 0, 0)),
        out_shape=jax.ShapeDtypeStruct((b, nchunk, nchunk, TQ, TQ), BF),
        scratch_shapes=[pltpu.VMEM((s, LANES), BF), pltpu.VMEM((s, LANES), BF),
                        pltpu.VMEM((nchunk, TQ, TQ), I32)],
        compiler_params=_cp(("parallel", "arbitrary")),
        name="dsa_index",
    )(q3, kw3, kw3)


def _dsa_attn_kernel(q_ref, c_ref, nm_ref, band_ref, wuk_ref, wuvt_ref, o_ref, ct_ref, acc_ref, sbuf_ref,
                     pbuf_ref, *, nchunk):
    qt = pl.program_id(1)
    hg = pl.program_id(2)

    @pl.when((qt == 0) & (hg == 0))
    def _():
        _store_transposed(c_ref, ct_ref, nchunk)

    q = q_ref[...]
    ql = jnp.concatenate(
        [(jnp.dot(q[:, j * HEAD_DIM:(j + 1) * HEAD_DIM], wuk_ref[j], preferred_element_type=F32)
          * (HEAD_DIM ** -0.5 * LOG2E)).astype(BF) for j in range(HPG)], axis=0)

    def scores(kc):
        start = pl.multiple_of(kc * TQ, TQ)
        return _dot_nt(c_ref[pl.ds(start, TQ), :], ql)

    def member_mask(kc):
        nm = nm_ref[kc].astype(F32)
        return jnp.concatenate([nm] * HPG, axis=1)

    _causal_chunk_attention(qt, scores, member_mask, lambda kc: ct_ref[kc], band_ref, acc_ref,
                            sbuf_ref, pbuf_ref)
    o_lat = (acc_ref[0:KV_RANK, :] * (1.0 / acc_ref[KV_RANK:KV_RANK + 1, :])).astype(BF)
    for j in range(HPG):
        ot = jnp.dot(wuvt_ref[j], o_lat[:, j * TQ:(j + 1) * TQ], preferred_element_type=F32)
        o_ref[:, j * HEAD_DIM:(j + 1) * HEAD_DIM] = ot.T.astype(o_ref.dtype)


def dsa_attention(qbf, cn, nmask, band, w_uk, w_uv, b, s):
    nchunk = s // TQ
    q3 = qbf.reshape(b, s, qbf.shape[1])
    c3 = cn.reshape(b, s, KV_RANK)
    gw = HPG * HEAD_DIM
    return pl.pallas_call(
        functools.partial(_dsa_attn_kernel, nchunk=nchunk),
        grid=(b, nchunk, N_HEADS // HPG),
        in_specs=[pl.BlockSpec((None, TQ, gw), lambda bi, qt, hg: (bi, qt, hg)),
                  pl.BlockSpec((None, s, KV_RANK), lambda bi, qt, hg: (bi, 0, 0)),
                  pl.BlockSpec((None, None, nchunk, TQ, TQ), lambda bi, qt, hg: (bi, qt, 0, 0, 0)),
                  pl.BlockSpec((None, 2, TQ, HPG * TQ), lambda bi, qt, hg: (hg, 0, 0, 0)),
                  pl.BlockSpec((HPG, HEAD_DIM, KV_RANK), lambda bi, qt, hg: (hg, 0, 0)),
                  pl.BlockSpec((HPG, HEAD_DIM, KV_RANK), lambda bi, qt, hg: (hg, 0, 0))],
        out_specs=pl.BlockSpec((None, TQ, gw), lambda bi, qt, hg: (bi, qt, hg)),
        out_shape=jax.ShapeDtypeStruct((b, s, N_HEADS * HEAD_DIM), BF),
        scratch_shapes=[pltpu.VMEM((nchunk, KV_RANK + ONES_ROWS, TQ), BF),
                        pltpu.VMEM((KV_RANK + ONES_ROWS, HPG * TQ), F32),
                        pltpu.VMEM((2, TQ, HPG * TQ), F32), pltpu.VMEM((2, TQ, HPG * TQ), BF)],
        compiler_params=_cp(("parallel", "arbitrary", "arbitrary")),
        name="dsa_attention",
    )(q3, c3, nmask, band, w_uk.astype(BF), jnp.swapaxes(w_uv, 1, 2).astype(BF))


ROUTER_TM = 256
MOE_TM = 512


def _router_kernel(x_ref, w_ref, b_ref, o_ref, cnt_ref, carry_ref):
    i = pl.program_id(0)

    @pl.when(i == 0)
    def _():
        carry_ref[...] = jnp.zeros_like(carry_ref)

    tm = x_ref.shape[0]
    logits = jnp.dot(x_ref[...], w_ref[...], preferred_element_type=F32,
                     precision=lax.Precision.HIGHEST) + b_ref[...]
    lane = lax.broadcasted_iota(I32, (tm, LANES), 1)
    lanef = lane.astype(F32)
    lg = jnp.where(lane < N_EXPERTS, logits, -BIG)
    m1 = jnp.max(lg, axis=-1, keepdims=True)
    i1 = jnp.min(jnp.where(lg == m1, lanef, float(LANES)), axis=-1, keepdims=True)
    lg2 = jnp.where(lanef == i1, -BIG, lg)
    m2 = jnp.max(lg2, axis=-1, keepdims=True)
    i2 = jnp.min(jnp.where(lg2 == m2, lanef, float(LANES)), axis=-1, keepdims=True)
    e2 = jnp.exp(m2 - m1)
    den = 1.0 + e2
    w1 = 1.0 / den
    w2 = e2 / den
    hit1 = lanef == i1
    hit2 = lanef == i2
    onehot = jnp.where(hit1 | hit2, 1.0, 0.0)
    r = lax.broadcasted_iota(I32, (tm, tm), 0)
    c = lax.broadcasted_iota(I32, (tm, tm), 1)
    tri = jnp.where(c < r, 1.0, 0.0).astype(BF)
    before = jnp.dot(tri, onehot.astype(BF), preferred_element_type=F32) + carry_ref[...]
    rank1 = jnp.sum(jnp.where(hit1, before, 0.0), axis=-1, keepdims=True)
    rank2 = jnp.sum(jnp.where(hit2, before, 0.0), axis=-1, keepdims=True)
    carry_ref[...] = carry_ref[...] + jnp.sum(onehot, axis=0, keepdims=True)
    vals = (i1, i2, w1, w2, rank1, rank2)
    out = jnp.zeros((tm, LANES), F32)
    for k, v in enumerate(vals):
        out = jnp.where(lane == k, v, out)
    o_ref[...] = out
    cnt_ref[...] = jnp.broadcast_to(carry_ref[...], cnt_ref.shape)


def moe_router(x, w_router, b_router):
    n, d = x.shape
    wp = jnp.pad(w_router.astype(F32), ((0, 0), (0, LANES - N_EXPERTS)))
    bp = jnp.pad(b_router.astype(F32), (0, LANES - N_EXPERTS)).reshape(1, LANES)
    return pl.pallas_call(
        _router_kernel,
        grid=(n // ROUTER_TM,),
        in_specs=[pl.BlockSpec((ROUTER_TM, d), lambda i: (i, 0)),
                  pl.BlockSpec((d, LANES), lambda i: (0, 0)),
                  pl.BlockSpec((1, LANES), lambda i: (0, 0))],
        out_specs=[pl.BlockSpec((ROUTER_TM, LANES), lambda i: (i, 0)),
                   pl.BlockSpec((8, LANES), lambda i: (0, 0))],
        out_shape=[jax.ShapeDtypeStruct((n, LANES), F32), jax.ShapeDtypeStruct((8, LANES), F32)],
        scratch_shapes=[pltpu.VMEM((1, LANES), F32)],
        compiler_params=_cp(("arbitrary",)),
        name="moe_router",
    )(x, wp, bp)


GATHER_ROWS = 1024


def _row_slabs(x):
    return x.reshape(x.shape[0], x.shape[1] // LANES, LANES)


def _row_gather_kernel(idx_ref, x_hbm, o_ref, sem):
    def start(r, _):
        pltpu.make_async_copy(x_hbm.at[idx_ref[0, 0, r]], o_ref.at[r], sem).start()
        return 0

    lax.fori_loop(0, GATHER_ROWS, start, 0)
    pltpu.make_async_copy(x_hbm.at[pl.ds(0, GATHER_ROWS)], o_ref, sem).wait()


def row_gather(x, row_idx):
    n_rows = row_idx.shape[0]
    nt = n_rows // GATHER_ROWS
    x3 = _row_slabs(x)
    out = pl.pallas_call(
        _row_gather_kernel,
        grid=(nt,),
        in_specs=[pl.BlockSpec((1, 1, GATHER_ROWS), lambda i: (i, 0, 0), memory_space=pltpu.SMEM),
                  pl.BlockSpec(memory_space=pl.ANY)],
        out_specs=pl.BlockSpec((GATHER_ROWS,) + x3.shape[1:], lambda i: (i, 0, 0)),
        out_shape=jax.ShapeDtypeStruct((n_rows,) + x3.shape[1:], x.dtype),
        scratch_shapes=[pltpu.SemaphoreType.DMA],
        compiler_params=_cp(("arbitrary",)),
        name="moe_row_gather",
    )(row_idx.reshape(nt, 1, GATHER_ROWS), x3)
    return out.reshape(n_rows, x.shape[1])


def _moe_up_kernel(te_ref, tv_ref, x_ref, wg_ref, wu_ref, o_ref):
    i = pl.program_id(1)

    @pl.when(tv_ref[i] > 0)
    def _():
        x = x_ref[...]
        o_ref[...] = (_silu(_dot(x, wg_ref[...])) * _dot(x, wu_ref[...])).astype(o_ref.dtype)

    @pl.when(tv_ref[i] == 0)
    def _():
        o_ref[...] = jnp.zeros_like(o_ref)


def moe_up(xs, w_gate, w_up, tile_e, tile_v, tn):
    n_rows, d = xs.shape
    f = w_gate.shape[2]
    nt = n_rows // MOE_TM
    grid_spec = pltpu.PrefetchScalarGridSpec(
        num_scalar_prefetch=2,
        grid=(f // tn, nt),
        in_specs=[pl.BlockSpec((MOE_TM, d), lambda j, i, te, tv: (i, 0)),
                  pl.BlockSpec((None, d, tn), lambda j, i, te, tv: (te[i], 0, j)),
                  pl.BlockSpec((None, d, tn), lambda j, i, te, tv: (te[i], 0, j))],
        out_specs=pl.BlockSpec((MOE_TM, tn), lambda j, i, te, tv: (i, j)),
    )
    return pl.pallas_call(
        _moe_up_kernel,
        grid_spec=grid_spec,
        out_shape=jax.ShapeDtypeStruct((n_rows, f), BF),
        compiler_params=_cp(("parallel", "arbitrary")),
        name="moe_up",
    )(tile_e, tile_v, xs, w_gate, w_up)


def _moe_down_kernel(te_ref, tv_ref, h_ref, wd_ref, o_ref):
    i = pl.program_id(1)

    @pl.when(tv_ref[i] > 0)
    def _():
        o_ref[...] = _dot(h_ref[...], wd_ref[...])

    @pl.when(tv_ref[i] == 0)
    def _():
        o_ref[...] = jnp.zeros_like(o_ref)


def moe_down(h, w_down, tile_e, tile_v, tn):
    n_rows, f = h.shape
    d = w_down.shape[2]
    nt = n_rows // MOE_TM
    grid_spec = pltpu.PrefetchScalarGridSpec(
        num_scalar_prefetch=2,
        grid=(d // tn, nt),
        in_specs=[pl.BlockSpec((MOE_TM, f), lambda j, i, te, tv: (i, 0)),
                  pl.BlockSpec((None, f, tn), lambda j, i, te, tv: (te[i], 0, j))],
        out_specs=pl.BlockSpec((MOE_TM, tn), lambda j, i, te, tv: (i, j)),
    )
    return pl.pallas_call(
        _moe_down_kernel,
        grid_spec=grid_spec,
        out_shape=jax.ShapeDtypeStruct((n_rows, d), F32),
        compiler_params=_cp(("parallel", "arbitrary")),
        name="moe_down",
    )(tile_e, tile_v, h, w_down)


COMBINE_TM = 256


def _moe_combine_kernel(d1_ref, d2_ref, y_hbm, x_ref, rw_ref, g_ref, b_ref, o_ref, buf, sem):
    nslab = buf.shape[2]

    def start(r, _):
        pltpu.make_async_copy(y_hbm.at[d1_ref[0, 0, r]], buf.at[0, r], sem).start()
        pltpu.make_async_copy(y_hbm.at[d2_ref[0, 0, r]], buf.at[1, r], sem).start()
        return 0

    lax.fori_loop(0, COMBINE_TM, start, 0)
    for k in range(2):
        pltpu.make_async_copy(y_hbm.at[pl.ds(0, COMBINE_TM)], buf.at[k], sem).wait()
    rw = rw_ref[...]
    w1 = rw[:, 2:3]
    w2 = rw[:, 3:4]
    z = []
    tot = jnp.zeros((COMBINE_TM, 1), F32)
    for c in range(nslab):
        sl = slice(c * LANES, (c + 1) * LANES)
        zc = ALPHA * x_ref[:, sl] + (w1 * buf[0, :, c, :] + w2 * buf[1, :, c, :])
        z.append(zc)
        tot = tot + jnp.sum(zc, axis=-1, keepdims=True)
    d = nslab * LANES
    mu = tot * (1.0 / d)
    ss = jnp.zeros((COMBINE_TM, 1), F32)
    for c in range(nslab):
        z[c] = z[c] - mu
        ss = ss + jnp.sum(z[c] * z[c], axis=-1, keepdims=True)
    r = lax.rsqrt(ss * (1.0 / d) + LN_EPS)
    for c in range(nslab):
        sl = slice(c * LANES, (c + 1) * LANES)
        o_ref[:, sl] = z[c] * r * g_ref[:, sl] + b_ref[:, sl]


def moe_combine(y_rows, x, route, dest1, dest2, g, beta):
    n, d = x.shape
    nt = n // COMBINE_TM
    y3 = _row_slabs(y_rows)
    idx_spec = pl.BlockSpec((1, 1, COMBINE_TM), lambda i: (i, 0, 0), memory_space=pltpu.SMEM)
    return pl.pallas_call(
        _moe_combine_kernel,
        grid=(nt,),
        in_specs=[idx_spec, idx_spec,
                  pl.BlockSpec(memory_space=pl.ANY),
                  pl.BlockSpec((COMBINE_TM, d), lambda i: (i, 0)),
                  pl.BlockSpec((COMBINE_TM, LANES), lambda i: (i, 0)),
                  pl.BlockSpec((1, d), lambda i: (0, 0)),
                  pl.BlockSpec((1, d), lambda i: (0, 0))],
        out_specs=pl.BlockSpec((COMBINE_TM, d), lambda i: (i, 0)),
        out_shape=jax.ShapeDtypeStruct((n, d), F32),
        scratch_shapes=[pltpu.VMEM((2, COMBINE_TM) + y3.shape[1:], F32), pltpu.SemaphoreType.DMA],
        compiler_params=_cp(("arbitrary",)),
        name="moe_combine",
    )(dest1.reshape(nt, 1, COMBINE_TM), dest2.reshape(nt, 1, COMBINE_TM), y3, x, route,
      g.reshape(1, d).astype(F32), beta.reshape(1, d).astype(F32))


def moe_layer(x, xb, w_router, b_router, w_gate, w_up, w_down, g, beta):
    n, d = x.shape
    route, cnt = moe_router(x, w_router, b_router)
    e1 = route[:, 0].astype(I32)
    e2 = route[:, 1].astype(I32)
    counts = cnt[0, :N_EXPERTS].astype(I32)
    padded = (counts + MOE_TM - 1) // MOE_TM * MOE_TM
    pad_end = jnp.cumsum(padded)
    pad_start = pad_end - padded
    dest1 = pad_start[e1] + route[:, 4].astype(I32)
    dest2 = pad_start[e2] + route[:, 5].astype(I32)
    n_rows = 2 * n + N_EXPERTS * MOE_TM
    nt = n_rows // MOE_TM
    tok = jnp.arange(n, dtype=I32)
    row_tok = jnp.zeros((n_rows,), I32).at[dest1].set(tok).at[dest2].set(tok)
    tile_start = jnp.arange(nt, dtype=I32) * MOE_TM
    tile_e = jnp.minimum(jnp.searchsorted(pad_end, tile_start, side='right'), N_EXPERTS - 1).astype(I32)
    tile_v = (tile_start < pad_end[-1]).astype(I32)
    xs = row_gather(xb, row_tok)
    h = moe_up(xs, w_gate, w_up, tile_e, tile_v, tn=min(1024, w_gate.shape[2]))
    y_rows = moe_down(h, w_down, tile_e, tile_v, tn=min(512, d))
    return moe_combine(y_rows, x, route, dest1, dest2, g, beta)


def _cmp_sel_overlap(n_cmp_pad, n_sel):
    i = np.arange(n_cmp_pad)[:, None]
    j = np.arange(LANES)[None, :]
    lo = np.maximum(i * CMP_STRIDE, j * SEL_LEN)
    hi = np.minimum(i * CMP_STRIDE + CMP_LEN, (j + 1) * SEL_LEN)
    ov = np.maximum(hi - lo, 0) / CMP_LEN
    ov[:, n_sel:] = 0.0
    return ov.astype(np.float32)


def _gate_columns():
    src = -np.ones((KV_GROUPS * LANES,), np.int64)
    for g in range(KV_GROUPS):
        for j in range(HPG):
            for br in range(3):
                src[g * LANES + br * HPG + j] = g * HPG * 3 + j * 3 + br
    return src


def nsa_layer(xb, b, s, w_in, pe_k, w1_k, w2_k, pe_v, w1_v, w2_v, band, cmpb):
    n = xb.shape[0]
    hd = N_HEADS * HEAD_DIM
    gw = KV_GROUPS * HEAD_DIM
    w_main = w_in[:, :hd + 6 * gw].astype(BF)
    scale = jnp.concatenate([jnp.full((hd,), HEAD_DIM ** -0.5 * LOG2E, F32), jnp.ones((6 * gw,), F32)])
    ybf = matmul_scaled(xb, w_main, scale, BF, tm=512, tn=1280)
    src = _gate_columns()
    w_gl = jnp.where(jnp.asarray(src >= 0), w_in[:, hd + 6 * gw:][:, np.maximum(src, 0)], 0.0).astype(BF)
    gl = matmul_scaled(xb, w_gl, jnp.ones((w_gl.shape[1],), F32), F32, tm=512, tn=512)
    cb = hd // gw
    kcmp = nsa_compress(ybf, b, s, cb + 0, pe_k, w1_k, w2_k)
    vcmp = nsa_compress(ybf, b, s, cb + 1, pe_v, w1_v, w2_v)
    overlap_t = jnp.asarray(_cmp_sel_overlap(s // CMP_STRIDE, s // SEL_LEN).T).astype(BF)
    oc, nsel = nsa_cmp_select(ybf, kcmp, vcmp, cmpb, overlap_t, b, s)
    c128 = hd // HEAD_DIM
    o = nsa_main(ybf, nsel, band, gl, oc, b, s,
                 col_ksl=c128 + 2 * KV_GROUPS, col_vsl=c128 + 3 * KV_GROUPS,
                 col_kw=c128 + 4 * KV_GROUPS, col_vw=c128 + 5 * KV_GROUPS)
    return o.reshape(n, hd)


def dsa_layer(xb, b, s, w_in, kv_norm_g, w_uk, w_uv, band):
    n = xb.shape[0]
    hd = N_HEADS * HEAD_DIM
    o_c = hd
    o_qi = hd + KV_RANK
    o_ki = o_qi + IDX_HEADS * IDX_DIM
    o_wi = o_ki + IDX_DIM
    w_q = jnp.concatenate([w_in[:, :hd], w_in[:, o_qi:o_ki]], axis=1).astype(BF)
    scale_q = jnp.concatenate([jnp.ones((hd,), F32), jnp.full((IDX_HEADS * IDX_DIM,), IDX_DIM ** -0.5, F32)])
    qbf = matmul_scaled(xb, w_q, scale_q, BF, tm=512, tn=1280)
    cn = matmul_rmsnorm(xb, w_in[:, o_c:o_qi].astype(BF), kv_norm_g, tm=512)
    w_kw = jnp.pad(w_in[:, o_ki:], ((0, 0), (0, LANES - IDX_DIM - IDX_HEADS))).astype(BF)
    scale_kw = jnp.concatenate([jnp.ones((IDX_DIM,), F32), jnp.full((IDX_HEADS,), IDX_HEADS ** -0.5, F32),
                                jnp.zeros((LANES - IDX_DIM - IDX_HEADS,), F32)])
    kw = matmul_scaled(xb, w_kw, scale_kw, F32, tm=512, tn=LANES)
    nmask = dsa_index(qbf, kw, b, s, col_qi=hd // (IDX_HEADS * IDX_DIM))
    o = dsa_attention(qbf, cn, nmask, band[:, :2], w_uk, w_uv, b, s)
    return o.reshape(n, hd)


def kernel(x, rel_bias, nsa_w_in, nsa_cmp_pe_k, nsa_cmp_w1_k, nsa_cmp_w2_k, nsa_cmp_pe_v, nsa_cmp_w1_v, nsa_cmp_w2_v, nsa_w_out, dsa_w_in, dsa_kv_norm_g, dsa_w_uk, dsa_w_uv, dsa_w_out, ffn_w_gate, ffn_w_up, ffn_w_down, moe_w_router, moe_b_router, moe_w_gate, moe_w_up, moe_w_down, ln_mix_g, ln_mix_b, ln_ffn_g, ln_ffn_b):
    b, s, d = x.shape
    n = b * s
    assert s % TQ == 0 and s // SEL_LEN <= LANES
    x0 = x.reshape(n, d)
    band, cmpb = _bias_tables(rel_bias)
    o = nsa_layer(x0.astype(BF), b, s, nsa_w_in, nsa_cmp_pe_k, nsa_cmp_w1_k, nsa_cmp_w2_k,
                  nsa_cmp_pe_v, nsa_cmp_w1_v, nsa_cmp_w2_v, band, cmpb)
    x1, x1b = matmul_residual_ln(o, nsa_w_out.astype(BF), x0, ln_mix_g[0], ln_mix_b[0], tm=256, tk=d)
    hff = swiglu_up(x1b, ffn_w_gate.astype(BF), ffn_w_up.astype(BF), tm=512, tn=512)
    x2, x2b = matmul_residual_ln(hff, ffn_w_down.astype(BF), x1, ln_ffn_g[0], ln_ffn_b[0], tm=256,
                                 tk=hff.shape[1] // 2)
    o = dsa_layer(x2b, b, s, dsa_w_in, dsa_kv_norm_g, dsa_w_uk, dsa_w_uv, band)
    x3, x3b = matmul_residual_ln(o, dsa_w_out.astype(BF), x2, ln_mix_g[1], ln_mix_b[1], tm=256, tk=d)
    out = moe_layer(x3, x3b, moe_w_router, moe_b_router, moe_w_gate, moe_w_up, moe_w_down,
                    ln_ffn_g[1], ln_ffn_b[1])
    return out.reshape(b, s, d)
```

```python
import functools
import math

import numpy as np
import jax
import jax.numpy as jnp
from jax import lax
from jax.experimental import pallas as pl
from jax.experimental.pallas import tpu as pltpu

F32 = jnp.float32
BF = jnp.bfloat16
I32 = jnp.int32

N_HEADS = 16
HEAD_DIM = 128
KV_GROUPS = 4
HPG = N_HEADS // KV_GROUPS
CMP_LEN = 32
CMP_STRIDE = 16
SEL_LEN = 64
SEL_TOP = 16
WINDOW = 512
SEL_FORCE = 1e4
KV_RANK = 256
IDX_HEADS = 8
IDX_DIM = 64
IDX_TOPK = 256
REL_BUCKETS = 32
REL_MAX_DIST = 128
N_EXPERTS = 8
DEPTH = 2
ALPHA = (2 * DEPTH) ** 0.25
LN_EPS = 1e-5
RMS_EPS = 1e-6
NEG_INF = -1e30

LOG2E = math.log2(math.e)
BIG = float(2.0 ** 100)
M_INIT = -3.0e38
TQ = 256
LANES = 128
VMEM_LIMIT_BYTES = 60000 * 1024


def _cp(sem, vmem=None):
    return pltpu.CompilerParams(dimension_semantics=sem, vmem_limit_bytes=vmem or VMEM_LIMIT_BYTES)


def _dot(a, b):
    return jnp.dot(a.astype(BF), b.astype(BF), preferred_element_type=F32)


def _dot_nt(a, b):
    return lax.dot_general(a.astype(BF), b.astype(BF), (((1,), (1,)), ((), ())),
                           preferred_element_type=F32)


def _layer_norm_rows(z, g, b):
    mu = jnp.mean(z, axis=-1, keepdims=True)
    zc = z - mu
    var = jnp.mean(zc * zc, axis=-1, keepdims=True)
    return zc * lax.rsqrt(var + LN_EPS) * g + b


def _mm_scale_kernel(a_ref, b_ref, s_ref, o_ref):
    o_ref[...] = (_dot(a_ref[...], b_ref[...]) * s_ref[...]).astype(o_ref.dtype)


def matmul_scaled(a, b, scale, out_dtype, tm, tn):
    m, k = a.shape
    n = b.shape[1]
    return pl.pallas_call(
        _mm_scale_kernel,
        grid=(m // tm, n // tn),
        in_specs=[pl.BlockSpec((tm, k), lambda i, j: (i, 0)),
                  pl.BlockSpec((k, tn), lambda i, j: (0, j)),
                  pl.BlockSpec((1, tn), lambda i, j: (0, j))],
        out_specs=pl.BlockSpec((tm, tn), lambda i, j: (i, j)),
        out_shape=jax.ShapeDtypeStruct((m, n), out_dtype),
        compiler_params=_cp(("parallel", "parallel")),
        name="matmul_scaled",
    )(a, b, scale.reshape(1, n).astype(F32))


def _mm_rms_kernel(a_ref, b_ref, g_ref, o_ref):
    c = _dot(a_ref[...], b_ref[...])
    r = lax.rsqrt(jnp.mean(c * c, axis=-1, keepdims=True) + RMS_EPS)
    o_ref[...] = (c * r * g_ref[...]).astype(o_ref.dtype)


def matmul_rmsnorm(a, b, g, tm):
    m, k = a.shape
    n = b.shape[1]
    return pl.pallas_call(
        _mm_rms_kernel,
        grid=(m // tm,),
        in_specs=[pl.BlockSpec((tm, k), lambda i: (i, 0)),
                  pl.BlockSpec((k, n), lambda i: (0, 0)),
                  pl.BlockSpec((1, n), lambda i: (0, 0))],
        out_specs=pl.BlockSpec((tm, n), lambda i: (i, 0)),
        out_shape=jax.ShapeDtypeStruct((m, n), BF),
        compiler_params=_cp(("parallel",)),
        name="matmul_rmsnorm",
    )(a, b, g.reshape(1, n).astype(F32))


def _mm_ln_kernel(a_ref, b_ref, r_ref, g_ref, be_ref, of_ref, ob_ref, acc_ref, *, nk):
    kk = pl.program_id(1)

    @pl.when(kk == 0)
    def _():
        acc_ref[...] = jnp.zeros_like(acc_ref)

    acc_ref[...] += _dot(a_ref[...], b_ref[...])

    @pl.when(kk == nk - 1)
    def _():
        y = _layer_norm_rows(ALPHA * r_ref[...] + acc_ref[...], g_ref[...], be_ref[...])
        of_ref[...] = y
        ob_ref[...] = y.astype(BF)


def matmul_residual_ln(a, b, res, g, beta, tm, tk):
    m, k = a.shape
    n = b.shape[1]
    nk = k // tk
    return pl.pallas_call(
        functools.partial(_mm_ln_kernel, nk=nk),
        grid=(m // tm, nk),
        in_specs=[pl.BlockSpec((tm, tk), lambda i, kk: (i, kk)),
                  pl.BlockSpec((tk, n), lambda i, kk: (kk, 0)),
                  pl.BlockSpec((tm, n), lambda i, kk: (i, 0)),
                  pl.BlockSpec((1, n), lambda i, kk: (0, 0)),
                  pl.BlockSpec((1, n), lambda i, kk: (0, 0))],
        out_specs=[pl.BlockSpec((tm, n), lambda i, kk: (i, 0)),
                   pl.BlockSpec((tm, n), lambda i, kk: (i, 0))],
        out_shape=[jax.ShapeDtypeStruct((m, n), F32), jax.ShapeDtypeStruct((m, n), BF)],
        scratch_shapes=[pltpu.VMEM((tm, n), F32)],
        compiler_params=_cp(("parallel", "arbitrary")),
        name="matmul_residual_ln",
    )(a, b, res, g.reshape(1, n).astype(F32), beta.reshape(1, n).astype(F32))


def _silu(x):
    return x * (1.0 / (1.0 + jnp.exp(-x)))


def _swiglu_up_kernel(a_ref, wg_ref, wu_ref, o_ref):
    a = a_ref[...]
    o_ref[...] = (_silu(_dot(a, wg_ref[...])) * _dot(a, wu_ref[...])).astype(o_ref.dtype)


def swiglu_up(a, wg, wu, tm, tn):
    m, k = a.shape
    n = wg.shape[1]
    return pl.pallas_call(
        _swiglu_up_kernel,
        grid=(m // tm, n // tn),
        in_specs=[pl.BlockSpec((tm, k), lambda i, j: (i, 0)),
                  pl.BlockSpec((k, tn), lambda i, j: (0, j)),
                  pl.BlockSpec((k, tn), lambda i, j: (0, j))],
        out_specs=pl.BlockSpec((tm, tn), lambda i, j: (i, j)),
        out_shape=jax.ShapeDtypeStruct((m, n), BF),
        compiler_params=_cp(("parallel", "parallel")),
        name="swiglu_up",
    )(a, wg, wu)


def _bucket_of_distance():
    n = np.arange(REL_MAX_DIST + 1)
    exact = REL_BUCKETS // 2
    nf = np.maximum(n, exact).astype(np.float64)
    large = exact + (np.log(nf / exact) / math.log(REL_MAX_DIST / exact) * (REL_BUCKETS - exact)).astype(np.int64)
    return np.where(n < exact, n, np.minimum(large, REL_BUCKETS - 1)).astype(np.int32)


def _bucket_starts():
    bk = _bucket_of_distance()
    return [int(np.argmax(bk >= b)) for b in range(REL_BUCKETS)]


def _bias_kernel(tab_ref, band_ref, cmp_ref, *, starts):
    h = pl.program_id(0)
    far = tab_ref[REL_BUCKETS - 1, h]

    def lookup(d):
        val = jnp.zeros(d.shape, F32)
        for b in range(REL_BUCKETS - 2, -1, -1):
            val = jnp.where(d < starts[b + 1], (tab_ref[b, h] - far) * LOG2E, val)
        return val

    j = lax.broadcasted_iota(I32, (TQ, TQ), 0)
    i = lax.broadcasted_iota(I32, (TQ, TQ), 1)
    for r in range(3):
        d = TQ * r + i - j
        band_ref[r] = jnp.where((d < 0) | (d >= WINDOW), -BIG, lookup(d))
    c = lax.broadcasted_iota(I32, (CMP_NEAR, TQ), 0)
    i = lax.broadcasted_iota(I32, (CMP_NEAR, TQ), 1)
    d = i - CMP_STRIDE * (c - CMP_NEAR // 2) - (CMP_LEN - 1)
    cmp_ref[...] = jnp.where(d < 0, -BIG, lookup(d))


CMP_NEAR = 2 * (TQ // CMP_STRIDE)


def _bias_tables(rel_bias):
    return pl.pallas_call(
        functools.partial(_bias_kernel, starts=_bucket_starts()),
        grid=(N_HEADS,),
        in_specs=[pl.BlockSpec(memory_space=pltpu.SMEM)],
        out_specs=[pl.BlockSpec((None, 3, TQ, TQ), lambda h: (h // HPG, 0, 0, h % HPG)),
                   pl.BlockSpec((None, CMP_NEAR, TQ), lambda h: (h // HPG, 0, h % HPG))],
        out_shape=[jax.ShapeDtypeStruct((N_HEADS // HPG, 3, TQ, HPG * TQ), F32),
                   jax.ShapeDtypeStruct((N_HEADS // HPG, CMP_NEAR, HPG * TQ), F32)],
        compiler_params=_cp(("parallel",)),
        name="bias_tables",
    )(rel_bias.astype(F32))


def _heads_to_rows(q):
    return jnp.concatenate([q[:, j * HEAD_DIM:(j + 1) * HEAD_DIM] for j in range(HPG)], axis=0)


def _causal_chunk_attention(qt, score_fn, extra_fn, vt_fn, band_ref, acc_ref, sbuf_ref, pbuf_ref):
    rows = HPG * TQ
    sa_ref, sb_ref = sbuf_ref.at[0], sbuf_ref.at[1]
    pa_ref, pb_ref = pbuf_ref.at[0], pbuf_ref.at[1]

    def softmax_stage(m, s_ref, p_ref, kc, bias=None):
        st = s_ref[...]
        extra = extra_fn(kc)
        if extra is not None:
            st = st + extra
        if bias is not None:
            st = st + bias
        m_new = jnp.maximum(m, jnp.max(st, axis=0, keepdims=True))
        p_ref[...] = jnp.exp2(st - m_new).astype(BF)
        return m_new, jnp.exp2(m - m_new)

    def value_stage(alpha, p_ref, kc):
        acc_ref[...] = alpha * acc_ref[...] + jnp.dot(vt_fn(kc), p_ref[...], preferred_element_type=F32)

    def far_pair(i, carry):
        m, alpha_b = carry
        a = 2 * i
        sb_ref[...] = score_fn(a + 1)
        value_stage(alpha_b, pb_ref, jnp.maximum(a - 1, 0))
        m, alpha_a = softmax_stage(m, sa_ref, pa_ref, a)
        sa_ref[...] = score_fn(a + 2)
        value_stage(alpha_a, pa_ref, a)
        m, alpha_b = softmax_stage(m, sb_ref, pb_ref, a + 1)
        return m, alpha_b

    acc_ref[...] = jnp.zeros_like(acc_ref)
    pb_ref[...] = jnp.zeros_like(pb_ref)
    sa_ref[...] = score_fn(0)
    npair = lax.shift_right_logical(jnp.maximum(qt - 1, 0), 1)
    m, alpha_b = lax.fori_loop(0, npair, far_pair,
                               (jnp.full((1, rows), M_INIT, F32), jnp.ones((1, rows), F32)))
    a = 2 * npair
    odd = (qt & 1) == 1
    first = qt == 0
    sb_ref[...] = score_fn(a + 1)
    value_stage(alpha_b, pb_ref, jnp.maximum(a - 1, 0))
    m, alpha_a = softmax_stage(m, sa_ref, pa_ref, a,
                               jnp.where(odd, band_ref[1], jnp.where(first, band_ref[0], 0.0)))
    value_stage(alpha_a, pa_ref, a)
    m, alpha_b = softmax_stage(m, sb_ref, pb_ref, a + 1,
                               jnp.where(odd, band_ref[0], jnp.where(first, -BIG, band_ref[1])))
    value_stage(alpha_b, pb_ref, a + 1)

    @pl.when((qt >= 2) & jnp.logical_not(odd))
    def _():
        sa_ref[...] = score_fn(qt)
        _, alpha_a = softmax_stage(m, sa_ref, pa_ref, qt, band_ref[0])
        value_stage(alpha_a, pa_ref, qt)


ONES_ROWS = 16


def _store_transposed(src_ref, dst_ref, nchunk):
    dv = src_ref.shape[1]
    tail = jnp.where(lax.broadcasted_iota(I32, (ONES_ROWS, TQ), 0) == 0, 1.0, 0.0).astype(dst_ref.dtype)

    def body(kc, _):
        start = pl.multiple_of(kc * TQ, TQ)
        dst_ref[kc, 0:dv, :] = src_ref[pl.ds(start, TQ), :].astype(F32).T.astype(dst_ref.dtype)
        dst_ref[kc, dv:dv + ONES_ROWS, :] = tail
        return 0

    lax.fori_loop(0, nchunk, body, 0)


def _gelu_tanh(x):
    return 0.5 * x * (1.0 + jnp.tanh(math.sqrt(2.0 / math.pi) * (x + 0.044715 * (x * x * x))))


def _compress_kernel(*refs, nch):
    x_refs = refs[:CMP_STRIDE]
    pe_ref, w1_ref, w2_ref, o_ref = refs[CMP_STRIDE:]
    acc_a = [jnp.zeros((nch, HEAD_DIM), F32) for _ in range(KV_GROUPS)]
    acc_b = [jnp.zeros((nch, HEAD_DIM), F32) for _ in range(KV_GROUPS)]
    for l in range(CMP_STRIDE):
        x = x_refs[l][...].astype(F32)
        xa = (x + pe_ref[l:l + 1, :]).astype(BF)
        xb = (x + pe_ref[CMP_STRIDE + l:CMP_STRIDE + l + 1, :]).astype(BF)
        for g in range(KV_GROUPS):
            sl = slice(g * HEAD_DIM, (g + 1) * HEAD_DIM)
            acc_a[g] = acc_a[g] + jnp.dot(xa[:, sl], w1_ref[l], preferred_element_type=F32)
            acc_b[g] = acc_b[g] + jnp.dot(xb[:, sl], w1_ref[CMP_STRIDE + l], preferred_element_type=F32)
    for g in range(KV_GROUPS):
        pre = acc_a[g] + pltpu.roll(acc_b[g], nch - 1, axis=0)
        hid = _gelu_tanh(pre).astype(BF)
        o_ref[:, g * HEAD_DIM:(g + 1) * HEAD_DIM] = jnp.dot(hid, w2_ref[...], preferred_element_type=F32).astype(BF)


def nsa_compress(ybf, b, s, col_block, pe, w1, w2):
    ncols = ybf.shape[1]
    nch = s // CMP_STRIDE
    blk_w = KV_GROUPS * HEAD_DIM
    per_tok = ncols // blk_w
    y3 = ybf.reshape(b, nch, CMP_STRIDE * ncols)
    pe_t = jnp.tile(pe.astype(F32), (1, KV_GROUPS))
    in_specs = [pl.BlockSpec((None, nch, blk_w), (lambda bi, l=l: (bi, 0, l * per_tok + col_block)))
                for l in range(CMP_STRIDE)]
    in_specs += [pl.BlockSpec((CMP_LEN, blk_w), lambda bi: (0, 0)),
                 pl.BlockSpec((CMP_LEN, HEAD_DIM, HEAD_DIM), lambda bi: (0, 0, 0)),
                 pl.BlockSpec((HEAD_DIM, HEAD_DIM), lambda bi: (0, 0))]
    return pl.pallas_call(
        functools.partial(_compress_kernel, nch=nch),
        grid=(b,),
        in_specs=in_specs,
        out_specs=pl.BlockSpec((None, nch, blk_w), lambda bi: (bi, 0, 0)),
        out_shape=jax.ShapeDtypeStruct((b, nch, blk_w), BF),
        compiler_params=_cp(("parallel",)),
        name="nsa_compress",
    )(*([y3] * CMP_STRIDE), pe_t, w1.astype(BF), w2.astype(BF))


def _nsa_cmp_kernel(q_ref, kc_ref, vc_ref, cb_ref, ovt_ref, oc_ref, ns_ref, vct_ref, s_ref, *, n_sel, n_top):
    qt = pl.program_id(2)
    t0 = qt * TQ
    ncp = kc_ref.shape[0]
    pad = CMP_NEAR // 2

    @pl.when(qt == 0)
    def _():
        vct_ref[...] = vc_ref[...].astype(F32).T.astype(BF)
        s_ref[0:pad, :] = jnp.zeros((pad, HPG * TQ), F32)

    qs = _heads_to_rows(q_ref[...])
    s_ref[pad:pad + ncp, :] = _dot_nt(kc_ref[...], qs)
    near = pl.multiple_of(qt * (TQ // CMP_STRIDE), TQ // CMP_STRIDE)
    s_ref[pl.ds(near, CMP_NEAR), :] = s_ref[pl.ds(near, CMP_NEAR), :] + cb_ref[...]
    s = s_ref[pad:pad + ncp, :]
    key = lax.broadcasted_iota(I32, (ncp, HPG * TQ), 0)
    tq = t0 + (lax.broadcasted_iota(I32, (ncp, HPG * TQ), 1) & (TQ - 1))
    vis = (key * CMP_STRIDE + (CMP_LEN - 1)) <= tq
    s = jnp.where(vis, s, -BIG)
    m = jnp.max(s, axis=0, keepdims=True)
    p = jnp.where(vis, jnp.exp2(s - m), 0.0)
    l = jnp.sum(p, axis=0, keepdims=True)
    p = p * jnp.where(l > 0.0, 1.0 / l, 0.0)
    pb = p.astype(BF)
    oct = jnp.dot(vct_ref[...], pb, preferred_element_type=F32)
    score = jnp.zeros((LANES, TQ), F32)
    for j in range(HPG):
        cs = slice(j * TQ, (j + 1) * TQ)
        oc_ref[:, j * HEAD_DIM:(j + 1) * HEAD_DIM] = oct[:, cs].T.astype(oc_ref.dtype)
        score = score + jnp.dot(ovt_ref[...], pb[:, cs], preferred_element_type=F32)
    blk = lax.broadcasted_iota(I32, (LANES, TQ), 0)
    t = t0 + lax.broadcasted_iota(I32, (LANES, TQ), 1)
    cur = lax.shift_right_logical(t, int(math.log2(SEL_LEN)))
    forced = (blk == 0) | (blk == cur) | (blk == cur - 1)
    visible = blk * SEL_LEN <= t
    sc = jnp.where(forced, SEL_FORCE, jnp.where(visible, score, -1.0))
    sc = jnp.where(blk < n_sel, sc, -2.0)
    blkf = blk.astype(F32)

    def pick_one(_, carry):
        sc, sel = carry
        mx = jnp.max(sc, axis=0, keepdims=True)
        first = jnp.min(jnp.where(sc == mx, blkf, float(LANES)), axis=0, keepdims=True)
        hit = blkf == first
        return jnp.where(hit, -3.0, sc), jnp.where(hit, 1.0, sel)

    _, sel = lax.fori_loop(0, n_top, pick_one, (sc, jnp.zeros((LANES, TQ), F32)))
    ns_ref[...] = jnp.where(sel.T > 0.5, 0.0, BIG).astype(BF)


def nsa_cmp_select(ybf, kcmp, vcmp, cmpb, overlap_t, b, s):
    ncp = kcmp.shape[1]
    n_sel = s // SEL_LEN
    n_top = min(SEL_TOP, n_sel)
    y3 = ybf.reshape(b, s, ybf.shape[1])
    gw = HPG * HEAD_DIM
    return pl.pallas_call(
        functools.partial(_nsa_cmp_kernel, n_sel=n_sel, n_top=n_top),
        grid=(b, KV_GROUPS, s // TQ),
        in_specs=[pl.BlockSpec((None, TQ, gw), lambda bi, g, qt: (bi, qt, g)),
                  pl.BlockSpec((None, ncp, HEAD_DIM), lambda bi, g, qt: (bi, 0, g)),
                  pl.BlockSpec((None, ncp, HEAD_DIM), lambda bi, g, qt: (bi, 0, g)),
                  pl.BlockSpec((None, CMP_NEAR, HPG * TQ), lambda bi, g, qt: (g, 0, 0)),
                  pl.BlockSpec((LANES, ncp), lambda bi, g, qt: (0, 0))],
        out_specs=[pl.BlockSpec((None, TQ, gw), lambda bi, g, qt: (bi, qt, g)),
                   pl.BlockSpec((None, None, TQ, LANES), lambda bi, g, qt: (bi, g, qt, 0))],
        out_shape=[jax.ShapeDtypeStruct((b, s, KV_GROUPS * gw), BF),
                   jax.ShapeDtypeStruct((b, KV_GROUPS, s, LANES), BF)],
        scratch_shapes=[pltpu.VMEM((HEAD_DIM, ncp), BF),
                        pltpu.VMEM((ncp + CMP_NEAR, HPG * TQ), F32)],
        compiler_params=_cp(("parallel", "parallel", "arbitrary")),
        name="nsa_cmp_select",
    )(y3, kcmp, vcmp, cmpb, overlap_t)


def _nsa_main_kernel(q_ref, ks_ref, vs_ref, kw_ref, vw_ref, ns_ref, band_ref, gl_ref, oc_ref, o_ref,
                     vst_ref, vwt_ref, accs_ref, accw_ref, sbuf_ref, pbuf_ref, *, nchunk):
    qt = pl.program_id(2)

    @pl.when(qt == 0)
    def _():
        _store_transposed(vs_ref, vst_ref, nchunk)
        _store_transposed(vw_ref, vwt_ref, nchunk)

    q = q_ref[...]
    ns = ns_ref[...]
    qs = _heads_to_rows(q)
    qp = jnp.concatenate([qs, jnp.concatenate([ns] * HPG, axis=0)], axis=1)

    def sel_scores(kc):
        start = pl.multiple_of(kc * TQ, TQ)
        k = ks_ref[pl.ds(start, TQ), :]
        krow = lax.broadcasted_iota(I32, (TQ, LANES), 0)
        klane = lax.broadcasted_iota(I32, (TQ, LANES), 1)
        kblk = lax.shift_right_logical(krow, int(math.log2(SEL_LEN))) + kc * (TQ // SEL_LEN)
        oh = jnp.where(klane == kblk, -1.0, 0.0).astype(BF)
        return _dot_nt(jnp.concatenate([k, oh], axis=1), qp)

    _causal_chunk_attention(qt, sel_scores, lambda kc: None, lambda kc: vst_ref[kc],
                            band_ref, accs_ref, sbuf_ref, pbuf_ref)

    accw_ref[...] = jnp.zeros_like(accw_ref)
    m_w = [jnp.full((1, TQ), M_INIT, F32)] * HPG
    for r in (2, 1, 0):
        kc = jnp.maximum(qt - r, 0)
        start = pl.multiple_of(kc * TQ, TQ)
        st_all = _dot_nt(kw_ref[pl.ds(start, TQ), :], qs) + jnp.where(qt >= r, band_ref[r], -BIG)
        for j in range(HPG):
            cs = slice(j * TQ, (j + 1) * TQ)
            st = st_all[:, cs]
            m_new = jnp.maximum(m_w[j], jnp.max(st, axis=0, keepdims=True))
            alpha = jnp.exp2(m_w[j] - m_new)
            p = jnp.exp2(st - m_new).astype(BF)
            m_w[j] = m_new
            accw_ref[:, cs] = alpha * accw_ref[:, cs] + jnp.dot(vwt_ref[kc], p, preferred_element_type=F32)

    gates = 1.0 / (1.0 + jnp.exp(-gl_ref[...]))
    oc = oc_ref[...].astype(F32)
    for j in range(HPG):
        sl = slice(j * HEAD_DIM, (j + 1) * HEAD_DIM)
        cs = slice(j * TQ, (j + 1) * TQ)
        o_s = (accs_ref[0:HEAD_DIM, cs] * (1.0 / accs_ref[HEAD_DIM:HEAD_DIM + 1, cs])).T
        o_w = (accw_ref[0:HEAD_DIM, cs] * (1.0 / accw_ref[HEAD_DIM:HEAD_DIM + 1, cs])).T
        o = (gates[:, j:j + 1] * oc[:, sl] + gates[:, HPG + j:HPG + j + 1] * o_s
             + gates[:, 2 * HPG + j:2 * HPG + j + 1] * o_w)
        o_ref[:, sl] = o.astype(o_ref.dtype)


def nsa_main(ybf, nsel, band, gl, oc, b, s, col_ksl, col_vsl, col_kw, col_vw):
    y3 = ybf.reshape(b, s, ybf.shape[1])
    gw = HPG * HEAD_DIM
    kv_spec = lambda cb: pl.BlockSpec((None, s, HEAD_DIM), lambda bi, g, qt: (bi, 0, cb + g))
    nchunk = s // TQ
    return pl.pallas_call(
        functools.partial(_nsa_main_kernel, nchunk=nchunk),
        grid=(b, KV_GROUPS, nchunk),
        in_specs=[pl.BlockSpec((None, TQ, gw), lambda bi, g, qt: (bi, qt, g)),
                  kv_spec(col_ksl), kv_spec(col_vsl), kv_spec(col_kw), kv_spec(col_vw),
                  pl.BlockSpec((None, None, TQ, LANES), lambda bi, g, qt: (bi, g, qt, 0)),
                  pl.BlockSpec((None, 3, TQ, HPG * TQ), lambda bi, g, qt: (g, 0, 0, 0)),
                  pl.BlockSpec((None, TQ, LANES), lambda bi, g, qt: (bi, qt, g)),
                  pl.BlockSpec((None, TQ, gw), lambda bi, g, qt: (bi, qt, g))],
        out_specs=pl.BlockSpec((None, TQ, gw), lambda bi, g, qt: (bi, qt, g)),
        out_shape=jax.ShapeDtypeStruct((b, s, KV_GROUPS * gw), BF),
        scratch_shapes=[pltpu.VMEM((nchunk, HEAD_DIM + ONES_ROWS, TQ), BF),
                        pltpu.VMEM((nchunk, HEAD_DIM + ONES_ROWS, TQ), BF),
                        pltpu.VMEM((HEAD_DIM + ONES_ROWS, HPG * TQ), F32),
                        pltpu.VMEM((HEAD_DIM + ONES_ROWS, HPG * TQ), F32),
                        pltpu.VMEM((2, TQ, HPG * TQ), F32), pltpu.VMEM((2, TQ, HPG * TQ), BF)],
        compiler_params=_cp(("parallel", "parallel", "arbitrary")),
        name="nsa_main",
    )(y3, y3, y3, y3, y3, nsel, band, gl.reshape(b, s, gl.shape[1]), oc)


def _dsa_index_kernel(qi_ref, kw_ref, wq_ref, nm_ref, ka_ref, kb_ref, key_ref, d_ref, *, k_sel, nchunk,
                      idx_bits):
    qt = pl.program_id(1)
    t0 = qt * TQ
    half = LANES // 2

    @pl.when(qt == 0)
    def _():
        lane = lax.broadcasted_iota(I32, kw_ref.shape, 1)
        ka = jnp.where(lane < IDX_DIM, kw_ref[...], 0.0)
        ka_ref[...] = ka.astype(BF)
        kb_ref[...] = pltpu.roll(ka, half, axis=1).astype(BF)

    qi = qi_ref[...]
    npair = IDX_HEADS // 2
    lq = jnp.concatenate([qi[:, p * LANES:(p + 1) * LANES] for p in range(npair)], axis=0)
    wt = wq_ref[...].T
    w_r = [wt[IDX_DIM + h:IDX_DIM + h + 1, :] for h in range(IDX_HEADS)]
    kpos = lax.broadcasted_iota(I32, (TQ, TQ), 0)
    qpos = t0 + lax.broadcasted_iota(I32, (TQ, TQ), 1)

    def head_dots(kc, buf):
        start = pl.multiple_of(jnp.minimum(kc, nchunk - 1) * TQ, TQ)
        d_ref[buf, 0] = _dot_nt(ka_ref[pl.ds(start, TQ), :], lq)
        d_ref[buf, 1] = _dot_nt(kb_ref[pl.ds(start, TQ), :], lq)

    def to_keys(kc, buf):
        sc = jnp.zeros((TQ, TQ), F32)
        for p in range(npair):
            cs = slice(p * TQ, (p + 1) * TQ)
            sc = (sc + w_r[2 * p] * jnp.maximum(d_ref[buf, 0, :, cs], 0.0)
                  + w_r[2 * p + 1] * jnp.maximum(d_ref[buf, 1, :, cs], 0.0))
        sc = jnp.where(sc == 0.0, 0.0, sc)
        sc = jnp.where(kc * TQ + kpos <= qpos, sc, NEG_INF)
        bits = lax.bitcast_convert_type(sc, I32)
        key_ref[kc] = jnp.where(bits < 0, bits ^ 0x7FFFFFFF, bits)

    def score_pair(i, _):
        a = 2 * i
        head_dots(a + 1, 1)
        to_keys(a, 0)
        head_dots(a + 2, 0)
        to_keys(a + 1, 1)
        return 0

    nproc = qt + 1
    head_dots(0, 0)
    lax.fori_loop(0, lax.shift_right_logical(nproc, 1), score_pair, 0)

    @pl.when((nproc & 1) == 1)
    def _():
        to_keys(nproc - 1, 0)

    def count(pred):
        def one(kc):
            return jnp.sum(jnp.where(pred(key_ref[kc], kc), 1.0, 0.0), axis=0, keepdims=True)

        def body(i, acc):
            return acc + (one(2 * i) + one(2 * i + 1))

        acc = lax.fori_loop(0, lax.shift_right_logical(nproc, 1), body, jnp.zeros((1, TQ), F32))
        return acc + jnp.where((nproc & 1) == 1, one(nproc - 1), 0.0)

    def count_ge(cand):
        return count(lambda k, kc: k >= cand)

    kf = float(k_sel)
    thr0 = jnp.full((1, TQ), -2 ** 31, I32)
    done0 = jnp.where(count_ge(thr0) == kf, 1.0, 0.0)

    def search_cond(state):
        i, _, done = state
        return (i < 32) & (jnp.min(done) < 0.5)

    def search_body(state):
        i, thr, done = state
        cand = thr ^ lax.shift_left(jnp.int32(1), jnp.int32(31) - i)
        cnt = count_ge(cand)
        take = (cnt >= kf) & (done < 0.5)
        thr = jnp.where(take, cand, thr)
        done = jnp.where(take & (cnt == kf), 1.0, done)
        return i + 1, thr, done

    _, thr, done = lax.while_loop(search_cond, search_body, (jnp.int32(0), thr0, done0))

    def tie_break(_):
        need = kf - count_ge(thr + 1)

        def ties_below(bound):
            return count(lambda k, kc: (k == thr) & (kc * TQ + kpos < bound))

        def idx_bit(i, jm):
            cand = jm | lax.shift_left(jnp.int32(1), jnp.int32(idx_bits - 1) - i)
            return jnp.where(ties_below(cand) < need, cand, jm)

        jm = lax.fori_loop(0, idx_bits, idx_bit, jnp.zeros((1, TQ), I32))
        return jnp.where(done > 0.5, jnp.int32(2 ** 30), jm)

    jm = lax.cond(jnp.min(done) < 0.5, tie_break, lambda _: jnp.full((1, TQ), 2 ** 30, I32), 0)

    def write_chunk(kc, _):
        k = key_ref[kc]
        kidx = kc * TQ + kpos
        sel = (kidx <= qpos) & ((k > thr) | ((k == thr) & (kidx <= jm)))
        nm_ref[kc] = jnp.where(sel, 0.0, -BIG).astype(BF)
        return 0

    lax.fori_loop(0, nproc, write_chunk, 0)

    def fill_chunk(kc, _):
        nm_ref[kc] = jnp.full((TQ, TQ), -BIG, BF)
        return 0

    lax.fori_loop(nproc, nchunk, fill_chunk, 0)


def dsa_index(qbf, kw, b, s, col_qi):
    nchunk = s // TQ
    k_sel = min(IDX_TOPK, s // 4)
    q3 = qbf.reshape(b, s, qbf.shape[1])
    kw3 = kw.reshape(b, s, LANES)
    return pl.pallas_call(
        functools.partial(_dsa_index_kernel, k_sel=k_sel, nchunk=nchunk, idx_bits=int(math.log2(s))),
        grid=(b, nchunk),
        in_specs=[pl.BlockSpec((None, TQ, IDX_HEADS * IDX_DIM), lambda bi, qt: (bi, qt, col_qi)),
                  pl.BlockSpec((None, s, LANES), lambda bi, qt: (bi, 0, 0)),
                  pl.BlockSpec((None, TQ, LANES), lambda bi, qt: (bi, qt, 0))],
        out_specs=pl.BlockSpec((None, None, nchunk, TQ, TQ), lambda bi, qt: (bi, qt, 0, 0, 0)),
        out_shape=jax.ShapeDtypeStruct((b, nchunk, nchunk, TQ, TQ), BF),
        scratch_shapes=[pltpu.VMEM((s, LANES), BF), pltpu.VMEM((s, LANES), BF),
                        pltpu.VMEM((nchunk, TQ, TQ), I32),
                        pltpu.VMEM((2, 2, TQ, (IDX_HEADS // 2) * TQ), F32)],
        compiler_params=_cp(("parallel", "arbitrary")),
        name="dsa_index",
    )(q3, kw3, kw3)


def _dsa_attn_kernel(q_ref, c_ref, nm_ref, band_ref, wuk_ref, wuvt_ref, o_ref, ct_ref, acc_ref, sbuf_ref,
                     pbuf_ref, *, nchunk):
    qt = pl.program_id(1)
    hg = pl.program_id(2)

    @pl.when((qt == 0) & (hg == 0))
    def _():
        _store_transposed(c_ref, ct_ref, nchunk)

    q = q_ref[...]
    ql = jnp.concatenate(
        [(jnp.dot(q[:, j * HEAD_DIM:(j + 1) * HEAD_DIM], wuk_ref[j], preferred_element_type=F32)
          * (HEAD_DIM ** -0.5 * LOG2E)).astype(BF) for j in range(HPG)], axis=0)

    def scores(kc):
        start = pl.multiple_of(kc * TQ, TQ)
        return _dot_nt(c_ref[pl.ds(start, TQ), :], ql)

    def member_mask(kc):
        nm = nm_ref[kc].astype(F32)
        return jnp.concatenate([nm] * HPG, axis=1)

    _causal_chunk_attention(qt, scores, member_mask, lambda kc: ct_ref[kc], band_ref, acc_ref,
                            sbuf_ref, pbuf_ref)
    o_lat = (acc_ref[0:KV_RANK, :] * (1.0 / acc_ref[KV_RANK:KV_RANK + 1, :])).astype(BF)
    for j in range(HPG):
        ot = jnp.dot(wuvt_ref[j], o_lat[:, j * TQ:(j + 1) * TQ], preferred_element_type=F32)
        o_ref[:, j * HEAD_DIM:(j + 1) * HEAD_DIM] = ot.T.astype(o_ref.dtype)


def dsa_attention(qbf, cn, nmask, band, w_uk, w_uv, b, s):
    nchunk = s // TQ
    q3 = qbf.reshape(b, s, qbf.shape[1])
    c3 = cn.reshape(b, s, KV_RANK)
    gw = HPG * HEAD_DIM
    return pl.pallas_call(
        functools.partial(_dsa_attn_kernel, nchunk=nchunk),
        grid=(b, nchunk, N_HEADS // HPG),
        in_specs=[pl.BlockSpec((None, TQ, gw), lambda bi, qt, hg: (bi, qt, hg)),
                  pl.BlockSpec((None, s, KV_RANK), lambda bi, qt, hg: (bi, 0, 0)),
                  pl.BlockSpec((None, None, nchunk, TQ, TQ), lambda bi, qt, hg: (bi, qt, 0, 0, 0)),
                  pl.BlockSpec((None, 2, TQ, HPG * TQ), lambda bi, qt, hg: (hg, 0, 0, 0)),
                  pl.BlockSpec((HPG, HEAD_DIM, KV_RANK), lambda bi, qt, hg: (hg, 0, 0)),
                  pl.BlockSpec((HPG, HEAD_DIM, KV_RANK), lambda bi, qt, hg: (hg, 0, 0))],
        out_specs=pl.BlockSpec((None, TQ, gw), lambda bi, qt, hg: (bi, qt, hg)),
        out_shape=jax.ShapeDtypeStruct((b, s, N_HEADS * HEAD_DIM), BF),
        scratch_shapes=[pltpu.VMEM((nchunk, KV_RANK + ONES_ROWS, TQ), BF),
                        pltpu.VMEM((KV_RANK + ONES_ROWS, HPG * TQ), F32),
                        pltpu.VMEM((2, TQ, HPG * TQ), F32), pltpu.VMEM((2, TQ, HPG * TQ), BF)],
        compiler_params=_cp(("parallel", "arbitrary", "arbitrary")),
        name="dsa_attention",
    )(q3, c3, nmask, band, w_uk.astype(BF), jnp.swapaxes(w_uv, 1, 2).astype(BF))


ROUTER_TM = 256
MOE_TM = 512


def _router_kernel(x_ref, w_ref, b_ref, o_ref, cnt_ref, carry_ref):
    i = pl.program_id(0)

    @pl.when(i == 0)
    def _():
        carry_ref[...] = jnp.zeros_like(carry_ref)

    tm = x_ref.shape[0]
    logits = jnp.dot(x_ref[...], w_ref[...], preferred_element_type=F32,
                     precision=lax.Precision.HIGHEST) + b_ref[...]
    lane = lax.broadcasted_iota(I32, (tm, LANES), 1)
    lanef = lane.astype(F32)
    lg = jnp.where(lane < N_EXPERTS, logits, -BIG)
    m1 = jnp.max(lg, axis=-1, keepdims=True)
    i1 = jnp.min(jnp.where(lg == m1, lanef, float(LANES)), axis=-1, keepdims=True)
    lg2 = jnp.where(lanef == i1, -BIG, lg)
    m2 = jnp.max(lg2, axis=-1, keepdims=True)
    i2 = jnp.min(jnp.where(lg2 == m2, lanef, float(LANES)), axis=-1, keepdims=True)
    e2 = jnp.exp(m2 - m1)
    den = 1.0 + e2
    w1 = 1.0 / den
    w2 = e2 / den
    hit1 = lanef == i1
    hit2 = lanef == i2
    onehot = jnp.where(hit1 | hit2, 1.0, 0.0)
    r = lax.broadcasted_iota(I32, (tm, tm), 0)
    c = lax.broadcasted_iota(I32, (tm, tm), 1)
    tri = jnp.where(c < r, 1.0, 0.0).astype(BF)
    before = jnp.dot(tri, onehot.astype(BF), preferred_element_type=F32) + carry_ref[...]
    rank1 = jnp.sum(jnp.where(hit1, before, 0.0), axis=-1, keepdims=True)
    rank2 = jnp.sum(jnp.where(hit2, before, 0.0), axis=-1, keepdims=True)
    carry_ref[...] = carry_ref[...] + jnp.sum(onehot, axis=0, keepdims=True)
    vals = (i1, i2, w1, w2, rank1, rank2)
    out = jnp.zeros((tm, LANES), F32)
    for k, v in enumerate(vals):
        out = jnp.where(lane == k, v, out)
    o_ref[...] = out
    cnt_ref[...] = jnp.broadcast_to(carry_ref[...], cnt_ref.shape)


def moe_router(x, w_router, b_router):
    n, d = x.shape
    wp = jnp.pad(w_router.astype(F32), ((0, 0), (0, LANES - N_EXPERTS)))
    bp = jnp.pad(b_router.astype(F32), (0, LANES - N_EXPERTS)).reshape(1, LANES)
    return pl.pallas_call(
        _router_kernel,
        grid=(n // ROUTER_TM,),
        in_specs=[pl.BlockSpec((ROUTER_TM, d), lambda i: (i, 0)),
                  pl.BlockSpec((d, LANES), lambda i: (0, 0)),
                  pl.BlockSpec((1, LANES), lambda i: (0, 0))],
        out_specs=[pl.BlockSpec((ROUTER_TM, LANES), lambda i: (i, 0)),
                   pl.BlockSpec((8, LANES), lambda i: (0, 0))],
        out_shape=[jax.ShapeDtypeStruct((n, LANES), F32), jax.ShapeDtypeStruct((8, LANES), F32)],
        scratch_shapes=[pltpu.VMEM((1, LANES), F32)],
        compiler_params=_cp(("arbitrary",)),
        name="moe_router",
    )(x, wp, bp)


GATHER_ROWS = 1024


def _row_slabs(x):
    return x.reshape(x.shape[0], x.shape[1] // LANES, LANES)


def _row_gather_kernel(idx_ref, x_hbm, o_ref, sem):
    def start(r, _):
        pltpu.make_async_copy(x_hbm.at[idx_ref[0, 0, r]], o_ref.at[r], sem).start()
        return 0

    lax.fori_loop(0, GATHER_ROWS, start, 0)
    pltpu.make_async_copy(x_hbm.at[pl.ds(0, GATHER_ROWS)], o_ref, sem).wait()


def row_gather(x, row_idx):
    n_rows = row_idx.shape[0]
    nt = n_rows // GATHER_ROWS
    x3 = _row_slabs(x)
    out = pl.pallas_call(
        _row_gather_kernel,
        grid=(nt,),
        in_specs=[pl.BlockSpec((1, 1, GATHER_ROWS), lambda i: (i, 0, 0), memory_space=pltpu.SMEM),
                  pl.BlockSpec(memory_space=pl.ANY)],
        out_specs=pl.BlockSpec((GATHER_ROWS,) + x3.shape[1:], lambda i: (i, 0, 0)),
        out_shape=jax.ShapeDtypeStruct((n_rows,) + x3.shape[1:], x.dtype),
        scratch_shapes=[pltpu.SemaphoreType.DMA],
        compiler_params=_cp(("arbitrary",)),
        name="moe_row_gather",
    )(row_idx.reshape(nt, 1, GATHER_ROWS), x3)
    return out.reshape(n_rows, x.shape[1])


def _moe_up_kernel(te_ref, tv_ref, x_ref, wg_ref, wu_ref, o_ref):
    i = pl.program_id(1)

    @pl.when(tv_ref[i] > 0)
    def _():
        x = x_ref[...]
        o_ref[...] = (_silu(_dot(x, wg_ref[...])) * _dot(x, wu_ref[...])).astype(o_ref.dtype)

    @pl.when(tv_ref[i] == 0)
    def _():
        o_ref[...] = jnp.zeros_like(o_ref)


def moe_up(xs, w_gate, w_up, tile_e, tile_v, tn):
    n_rows, d = xs.shape
    f = w_gate.shape[2]
    nt = n_rows // MOE_TM
    grid_spec = pltpu.PrefetchScalarGridSpec(
        num_scalar_prefetch=2,
        grid=(f // tn, nt),
        in_specs=[pl.BlockSpec((MOE_TM, d), lambda j, i, te, tv: (i, 0)),
                  pl.BlockSpec((None, d, tn), lambda j, i, te, tv: (te[i], 0, j)),
                  pl.BlockSpec((None, d, tn), lambda j, i, te, tv: (te[i], 0, j))],
        out_specs=pl.BlockSpec((MOE_TM, tn), lambda j, i, te, tv: (i, j)),
    )
    return pl.pallas_call(
        _moe_up_kernel,
        grid_spec=grid_spec,
        out_shape=jax.ShapeDtypeStruct((n_rows, f), BF),
        compiler_params=_cp(("parallel", "arbitrary")),
        name="moe_up",
    )(tile_e, tile_v, xs, w_gate, w_up)


def _moe_down_kernel(te_ref, tv_ref, h_ref, wd_ref, o_ref):
    i = pl.program_id(1)

    @pl.when(tv_ref[i] > 0)
    def _():
        o_ref[...] = _dot(h_ref[...], wd_ref[...])

    @pl.when(tv_ref[i] == 0)
    def _():
        o_ref[...] = jnp.zeros_like(o_ref)


def moe_down(h, w_down, tile_e, tile_v, tn):
    n_rows, f = h.shape
    d = w_down.shape[2]
    nt = n_rows // MOE_TM
    grid_spec = pltpu.PrefetchScalarGridSpec(
        num_scalar_prefetch=2,
        grid=(d // tn, nt),
        in_specs=[pl.BlockSpec((MOE_TM, f), lambda j, i, te, tv: (i, 0)),
                  pl.BlockSpec((None, f, tn), lambda j, i, te, tv: (te[i], 0, j))],
        out_specs=pl.BlockSpec((MOE_TM, tn), lambda j, i, te, tv: (i, j)),
    )
    return pl.pallas_call(
        _moe_down_kernel,
        grid_spec=grid_spec,
        out_shape=jax.ShapeDtypeStruct((n_rows, d), F32),
        compiler_params=_cp(("parallel", "arbitrary")),
        name="moe_down",
    )(tile_e, tile_v, h, w_down)


COMBINE_TM = 256


def _moe_combine_kernel(d1_ref, d2_ref, y_hbm, x_ref, rw_ref, g_ref, b_ref, o_ref, buf, sem):
    nslab = buf.shape[2]

    def start(r, _):
        pltpu.make_async_copy(y_hbm.at[d1_ref[0, 0, r]], buf.at[0, r], sem).start()
        pltpu.make_async_copy(y_hbm.at[d2_ref[0, 0, r]], buf.at[1, r], sem).start()
        return 0

    lax.fori_loop(0, COMBINE_TM, start, 0)
    for k in range(2):
        pltpu.make_async_copy(y_hbm.at[pl.ds(0, COMBINE_TM)], buf.at[k], sem).wait()
    rw = rw_ref[...]
    w1 = rw[:, 2:3]
    w2 = rw[:, 3:4]
    z = []
    tot = jnp.zeros((COMBINE_TM, 1), F32)
    for c in range(nslab):
        sl = slice(c * LANES, (c + 1) * LANES)
        zc = ALPHA * x_ref[:, sl] + (w1 * buf[0, :, c, :] + w2 * buf[1, :, c, :])
        z.append(zc)
        tot = tot + jnp.sum(zc, axis=-1, keepdims=True)
    d = nslab * LANES
    mu = tot * (1.0 / d)
    ss = jnp.zeros((COMBINE_TM, 1), F32)
    for c in range(nslab):
        z[c] = z[c] - mu
        ss = ss + jnp.sum(z[c] * z[c], axis=-1, keepdims=True)
    r = lax.rsqrt(ss * (1.0 / d) + LN_EPS)
    for c in range(nslab):
        sl = slice(c * LANES, (c + 1) * LANES)
        o_ref[:, sl] = z[c] * r * g_ref[:, sl] + b_ref[:, sl]


def moe_combine(y_rows, x, route, dest1, dest2, g, beta):
    n, d = x.shape
    nt = n // COMBINE_TM
    y3 = _row_slabs(y_rows)
    idx_spec = pl.BlockSpec((1, 1, COMBINE_TM), lambda i: (i, 0, 0), memory_space=pltpu.SMEM)
    return pl.pallas_call(
        _moe_combine_kernel,
        grid=(nt,),
        in_specs=[idx_spec, idx_spec,
                  pl.BlockSpec(memory_space=pl.ANY),
                  pl.BlockSpec((COMBINE_TM, d), lambda i: (i, 0)),
                  pl.BlockSpec((COMBINE_TM, LANES), lambda i: (i, 0)),
                  pl.BlockSpec((1, d), lambda i: (0, 0)),
                  pl.BlockSpec((1, d), lambda i: (0, 0))],
        out_specs=pl.BlockSpec((COMBINE_TM, d), lambda i: (i, 0)),
        out_shape=jax.ShapeDtypeStruct((n, d), F32),
        scratch_shapes=[pltpu.VMEM((2, COMBINE_TM) + y3.shape[1:], F32), pltpu.SemaphoreType.DMA],
        compiler_params=_cp(("arbitrary",)),
        name="moe_combine",
    )(dest1.reshape(nt, 1, COMBINE_TM), dest2.reshape(nt, 1, COMBINE_TM), y3, x, route,
      g.reshape(1, d).astype(F32), beta.reshape(1, d).astype(F32))


def moe_layer(x, xb, w_router, b_router, w_gate, w_up, w_down, g, beta):
    n, d = x.shape
    route, cnt = moe_router(x, w_router, b_router)
    e1 = route[:, 0].astype(I32)
    e2 = route[:, 1].astype(I32)
    counts = cnt[0, :N_EXPERTS].astype(I32)
    padded = (counts + MOE_TM - 1) // MOE_TM * MOE_TM
    pad_end = jnp.cumsum(padded)
    pad_start = pad_end - padded
    dest1 = pad_start[e1] + route[:, 4].astype(I32)
    dest2 = pad_start[e2] + route[:, 5].astype(I32)
    n_rows = 2 * n + N_EXPERTS * MOE_TM
    nt = n_rows // MOE_TM
    tok = jnp.arange(n, dtype=I32)
    row_tok = jnp.zeros((n_rows,), I32).at[dest1].set(tok).at[dest2].set(tok)
    tile_start = jnp.arange(nt, dtype=I32) * MOE_TM
    tile_e = jnp.minimum(jnp.searchsorted(pad_end, tile_start, side='right'), N_EXPERTS - 1).astype(I32)
    tile_v = (tile_start < pad_end[-1]).astype(I32)
    xs = row_gather(xb, row_tok)
    h = moe_up(xs, w_gate, w_up, tile_e, tile_v, tn=min(1024, w_gate.shape[2]))
    y_rows = moe_down(h, w_down, tile_e, tile_v, tn=min(512, d))
    return moe_combine(y_rows, x, route, dest1, dest2, g, beta)


def _cmp_sel_overlap(n_cmp_pad, n_sel):
    i = np.arange(n_cmp_pad)[:, None]
    j = np.arange(LANES)[None, :]
    lo = np.maximum(i * CMP_STRIDE, j * SEL_LEN)
    hi = np.minimum(i * CMP_STRIDE + CMP_LEN, (j + 1) * SEL_LEN)
    ov = np.maximum(hi - lo, 0) / CMP_LEN
    ov[:, n_sel:] = 0.0
    return ov.astype(np.float32)


def _gate_columns():
    src = -np.ones((KV_GROUPS * LANES,), np.int64)
    for g in range(KV_GROUPS):
        for j in range(HPG):
            for br in range(3):
                src[g * LANES + br * HPG + j] = g * HPG * 3 + j * 3 + br
    return src


def nsa_layer(xb, b, s, w_in, pe_k, w1_k, w2_k, pe_v, w1_v, w2_v, band, cmpb):
    n = xb.shape[0]
    hd = N_HEADS * HEAD_DIM
    gw = KV_GROUPS * HEAD_DIM
    w_main = w_in[:, :hd + 6 * gw].astype(BF)
    scale = jnp.concatenate([jnp.full((hd,), HEAD_DIM ** -0.5 * LOG2E, F32), jnp.ones((6 * gw,), F32)])
    ybf = matmul_scaled(xb, w_main, scale, BF, tm=512, tn=1280)
    src = _gate_columns()
    w_gl = jnp.where(jnp.asarray(src >= 0), w_in[:, hd + 6 * gw:][:, np.maximum(src, 0)], 0.0).astype(BF)
    gl = matmul_scaled(xb, w_gl, jnp.ones((w_gl.shape[1],), F32), F32, tm=512, tn=512)
    cb = hd // gw
    kcmp = nsa_compress(ybf, b, s, cb + 0, pe_k, w1_k, w2_k)
    vcmp = nsa_compress(ybf, b, s, cb + 1, pe_v, w1_v, w2_v)
    overlap_t = jnp.asarray(_cmp_sel_overlap(s // CMP_STRIDE, s // SEL_LEN).T).astype(BF)
    oc, nsel = nsa_cmp_select(ybf, kcmp, vcmp, cmpb, overlap_t, b, s)
    c128 = hd // HEAD_DIM
    o = nsa_main(ybf, nsel, band, gl, oc, b, s,
                 col_ksl=c128 + 2 * KV_GROUPS, col_vsl=c128 + 3 * KV_GROUPS,
                 col_kw=c128 + 4 * KV_GROUPS, col_vw=c128 + 5 * KV_GROUPS)
    return o.reshape(n, hd)


def dsa_layer(xb, b, s, w_in, kv_norm_g, w_uk, w_uv, band):
    n = xb.shape[0]
    hd = N_HEADS * HEAD_DIM
    o_c = hd
    o_qi = hd + KV_RANK
    o_ki = o_qi + IDX_HEADS * IDX_DIM
    o_wi = o_ki + IDX_DIM
    w_q = jnp.concatenate([w_in[:, :hd], w_in[:, o_qi:o_ki]], axis=1).astype(BF)
    scale_q = jnp.concatenate([jnp.ones((hd,), F32), jnp.full((IDX_HEADS * IDX_DIM,), IDX_DIM ** -0.5, F32)])
    qbf = matmul_scaled(xb, w_q, scale_q, BF, tm=512, tn=1280)
    cn = matmul_rmsnorm(xb, w_in[:, o_c:o_qi].astype(BF), kv_norm_g, tm=512)
    w_kw = jnp.pad(w_in[:, o_ki:], ((0, 0), (0, LANES - IDX_DIM - IDX_HEADS))).astype(BF)
    scale_kw = jnp.concatenate([jnp.ones((IDX_DIM,), F32), jnp.full((IDX_HEADS,), IDX_HEADS ** -0.5, F32),
                                jnp.zeros((LANES - IDX_DIM - IDX_HEADS,), F32)])
    kw = matmul_scaled(xb, w_kw, scale_kw, F32, tm=512, tn=LANES)
    nmask = dsa_index(qbf, kw, b, s, col_qi=hd // (IDX_HEADS * IDX_DIM))
    o = dsa_attention(qbf, cn, nmask, band[:, :2], w_uk, w_uv, b, s)
    return o.reshape(n, hd)


def kernel(x, rel_bias, nsa_w_in, nsa_cmp_pe_k, nsa_cmp_w1_k, nsa_cmp_w2_k, nsa_cmp_pe_v, nsa_cmp_w1_v, nsa_cmp_w2_v, nsa_w_out, dsa_w_in, dsa_kv_norm_g, dsa_w_uk, dsa_w_uv, dsa_w_out, ffn_w_gate, ffn_w_up, ffn_w_down, moe_w_router, moe_b_router, moe_w_gate, moe_w_up, moe_w_down, ln_mix_g, ln_mix_b, ln_ffn_g, ln_ffn_b):
    b, s, d = x.shape
    n = b * s
    assert s % TQ == 0 and s // SEL_LEN <= LANES
    x0 = x.reshape(n, d)
    band, cmpb = _bias_tables(rel_bias)
    o = nsa_layer(x0.astype(BF), b, s, nsa_w_in, nsa_cmp_pe_k, nsa_cmp_w1_k, nsa_cmp_w2_k,
                  nsa_cmp_pe_v, nsa_cmp_w1_v, nsa_cmp_w2_v, band, cmpb)
    x1, x1b = matmul_residual_ln(o, nsa_w_out.astype(BF), x0, ln_mix_g[0], ln_mix_b[0], tm=256, tk=d)
    hff = swiglu_up(x1b, ffn_w_gate.astype(BF), ffn_w_up.astype(BF), tm=512, tn=512)
    x2, x2b = matmul_residual_ln(hff, ffn_w_down.astype(BF), x1, ln_ffn_g[0], ln_ffn_b[0], tm=256,
                                 tk=hff.shape[1] // 2)
    o = dsa_layer(x2b, b, s, dsa_w_in, dsa_kv_norm_g, dsa_w_uk, dsa_w_uv, band)
    x3, x3b = matmul_residual_ln(o, dsa_w_out.astype(BF), x2, ln_mix_g[1], ln_mix_b[1], tm=256, tk=d)
    out = moe_layer(x3, x3b, moe_w_router, moe_b_router, moe_w_gate, moe_w_up, moe_w_down,
                    ln_ffn_g[1], ln_ffn_b[1])
    return out.reshape(b, s, d)
```

```python
import functools
import math

import numpy as np
import jax
import jax.numpy as jnp
from jax import lax
from jax.experimental import pallas as pl
from jax.experimental.pallas import tpu as pltpu

F32 = jnp.float32
BF = jnp.bfloat16
I32 = jnp.int32

N_HEADS = 16
HEAD_DIM = 128
KV_GROUPS = 4
HPG = N_HEADS // KV_GROUPS
CMP_LEN = 32
CMP_STRIDE = 16
SEL_LEN = 64
SEL_TOP = 16
WINDOW = 512
SEL_FORCE = 1e4
KV_RANK = 256
IDX_HEADS = 8
IDX_DIM = 64
IDX_TOPK = 256
REL_BUCKETS = 32
REL_MAX_DIST = 128
N_EXPERTS = 8
DEPTH = 2
ALPHA = (2 * DEPTH) ** 0.25
LN_EPS = 1e-5
RMS_EPS = 1e-6
NEG_INF = -1e30

LOG2E = math.log2(math.e)
BIG = float(2.0 ** 100)
M_INIT = -3.0e38
TQ = 256
LANES = 128
VMEM_LIMIT_BYTES = 60000 * 1024


def _cp(sem, vmem=None):
    return pltpu.CompilerParams(dimension_semantics=sem, vmem_limit_bytes=vmem or VMEM_LIMIT_BYTES)


def _dot(a, b):
    return jnp.dot(a.astype(BF), b.astype(BF), preferred_element_type=F32)


def _dot_nt(a, b):
    return lax.dot_general(a.astype(BF), b.astype(BF), (((1,), (1,)), ((), ())),
                           preferred_element_type=F32)


def _layer_norm_rows(z, g, b):
    mu = jnp.mean(z, axis=-1, keepdims=True)
    zc = z - mu
    var = jnp.mean(zc * zc, axis=-1, keepdims=True)
    return zc * lax.rsqrt(var + LN_EPS) * g + b


def _mm_scale_kernel(a_ref, b_ref, s_ref, o_ref):
    o_ref[...] = (_dot(a_ref[...], b_ref[...]) * s_ref[...]).astype(o_ref.dtype)


def matmul_scaled(a, b, scale, out_dtype, tm, tn):
    m, k = a.shape
    n = b.shape[1]
    return pl.pallas_call(
        _mm_scale_kernel,
        grid=(m // tm, n // tn),
        in_specs=[pl.BlockSpec((tm, k), lambda i, j: (i, 0)),
                  pl.BlockSpec((k, tn), lambda i, j: (0, j)),
                  pl.BlockSpec((1, tn), lambda i, j: (0, j))],
        out_specs=pl.BlockSpec((tm, tn), lambda i, j: (i, j)),
        out_shape=jax.ShapeDtypeStruct((m, n), out_dtype),
        compiler_params=_cp(("parallel", "parallel")),
        name="matmul_scaled",
    )(a, b, scale.reshape(1, n).astype(F32))


def _mm_rms_kernel(a_ref, b_ref, g_ref, o_ref):
    c = _dot(a_ref[...], b_ref[...])
    r = lax.rsqrt(jnp.mean(c * c, axis=-1, keepdims=True) + RMS_EPS)
    o_ref[...] = (c * r * g_ref[...]).astype(o_ref.dtype)


def matmul_rmsnorm(a, b, g, tm):
    m, k = a.shape
    n = b.shape[1]
    return pl.pallas_call(
        _mm_rms_kernel,
        grid=(m // tm,),
        in_specs=[pl.BlockSpec((tm, k), lambda i: (i, 0)),
                  pl.BlockSpec((k, n), lambda i: (0, 0)),
                  pl.BlockSpec((1, n), lambda i: (0, 0))],
        out_specs=pl.BlockSpec((tm, n), lambda i: (i, 0)),
        out_shape=jax.ShapeDtypeStruct((m, n), BF),
        compiler_params=_cp(("parallel",)),
        name="matmul_rmsnorm",
    )(a, b, g.reshape(1, n).astype(F32))


def _mm_ln_kernel(a_ref, b_ref, r_ref, g_ref, be_ref, of_ref, ob_ref, acc_ref, *, nk):
    kk = pl.program_id(1)

    @pl.when(kk == 0)
    def _():
        acc_ref[...] = jnp.zeros_like(acc_ref)

    acc_ref[...] += _dot(a_ref[...], b_ref[...])

    @pl.when(kk == nk - 1)
    def _():
        y = _layer_norm_rows(ALPHA * r_ref[...] + acc_ref[...], g_ref[...], be_ref[...])
        of_ref[...] = y
        ob_ref[...] = y.astype(BF)


def matmul_residual_ln(a, b, res, g, beta, tm, tk):
    m, k = a.shape
    n = b.shape[1]
    nk = k // tk
    return pl.pallas_call(
        functools.partial(_mm_ln_kernel, nk=nk),
        grid=(m // tm, nk),
        in_specs=[pl.BlockSpec((tm, tk), lambda i, kk: (i, kk)),
                  pl.BlockSpec((tk, n), lambda i, kk: (kk, 0)),
                  pl.BlockSpec((tm, n), lambda i, kk: (i, 0)),
                  pl.BlockSpec((1, n), lambda i, kk: (0, 0)),
                  pl.BlockSpec((1, n), lambda i, kk: (0, 0))],
        out_specs=[pl.BlockSpec((tm, n), lambda i, kk: (i, 0)),
                   pl.BlockSpec((tm, n), lambda i, kk: (i, 0))],
        out_shape=[jax.ShapeDtypeStruct((m, n), F32), jax.ShapeDtypeStruct((m, n), BF)],
        scratch_shapes=[pltpu.VMEM((tm, n), F32)],
        compiler_params=_cp(("parallel", "arbitrary")),
        name="matmul_residual_ln",
    )(a, b, res, g.reshape(1, n).astype(F32), beta.reshape(1, n).astype(F32))


def _silu(x):
    return x * (1.0 / (1.0 + jnp.exp(-x)))


def _swiglu_up_kernel(a_ref, wg_ref, wu_ref, o_ref):
    a = a_ref[...]
    o_ref[...] = (_silu(_dot(a, wg_ref[...])) * _dot(a, wu_ref[...])).astype(o_ref.dtype)


def swiglu_up(a, wg, wu, tm, tn):
    m, k = a.shape
    n = wg.shape[1]
    return pl.pallas_call(
        _swiglu_up_kernel,
        grid=(m // tm, n // tn),
        in_specs=[pl.BlockSpec((tm, k), lambda i, j: (i, 0)),
                  pl.BlockSpec((k, tn), lambda i, j: (0, j)),
                  pl.BlockSpec((k, tn), lambda i, j: (0, j))],
        out_specs=pl.BlockSpec((tm, tn), lambda i, j: (i, j)),
        out_shape=jax.ShapeDtypeStruct((m, n), BF),
        compiler_params=_cp(("parallel", "parallel")),
        name="swiglu_up",
    )(a, wg, wu)


def _bucket_of_distance():
    n = np.arange(REL_MAX_DIST + 1)
    exact = REL_BUCKETS // 2
    nf = np.maximum(n, exact).astype(np.float64)
    large = exact + (np.log(nf / exact) / math.log(REL_MAX_DIST / exact) * (REL_BUCKETS - exact)).astype(np.int64)
    return np.where(n < exact, n, np.minimum(large, REL_BUCKETS - 1)).astype(np.int32)


def _bucket_starts():
    bk = _bucket_of_distance()
    return [int(np.argmax(bk >= b)) for b in range(REL_BUCKETS)]


def _bias_kernel(tab_ref, band_ref, cmp_ref, *, starts):
    h = pl.program_id(0)
    far = tab_ref[REL_BUCKETS - 1, h]

    def lookup(d):
        val = jnp.zeros(d.shape, F32)
        for b in range(REL_BUCKETS - 2, -1, -1):
            val = jnp.where(d < starts[b + 1], (tab_ref[b, h] - far) * LOG2E, val)
        return val

    j = lax.broadcasted_iota(I32, (TQ, TQ), 0)
    i = lax.broadcasted_iota(I32, (TQ, TQ), 1)
    for r in range(3):
        d = TQ * r + i - j
        band_ref[r] = jnp.where((d < 0) | (d >= WINDOW), -BIG, lookup(d))
    c = lax.broadcasted_iota(I32, (CMP_NEAR, TQ), 0)
    i = lax.broadcasted_iota(I32, (CMP_NEAR, TQ), 1)
    d = i - CMP_STRIDE * (c - CMP_NEAR // 2) - (CMP_LEN - 1)
    cmp_ref[...] = jnp.where(d < 0, -BIG, lookup(d))


CMP_NEAR = 2 * (TQ // CMP_STRIDE)


def _bias_tables(rel_bias):
    return pl.pallas_call(
        functools.partial(_bias_kernel, starts=_bucket_starts()),
        grid=(N_HEADS,),
        in_specs=[pl.BlockSpec(memory_space=pltpu.SMEM)],
        out_specs=[pl.BlockSpec((None, 3, TQ, TQ), lambda h: (h // HPG, 0, 0, h % HPG)),
                   pl.BlockSpec((None, CMP_NEAR, TQ), lambda h: (h // HPG, 0, h % HPG))],
        out_shape=[jax.ShapeDtypeStruct((N_HEADS // HPG, 3, TQ, HPG * TQ), F32),
                   jax.ShapeDtypeStruct((N_HEADS // HPG, CMP_NEAR, HPG * TQ), F32)],
        compiler_params=_cp(("parallel",)),
        name="bias_tables",
    )(rel_bias.astype(F32))


def _heads_to_rows(q):
    return jnp.concatenate([q[:, j * HEAD_DIM:(j + 1) * HEAD_DIM] for j in range(HPG)], axis=0)


def _causal_chunk_attention(qt, score_fn, extra_fn, vt_fn, band_ref, acc_ref, sbuf_ref, pbuf_ref):
    rows = HPG * TQ
    sa_ref, sb_ref = sbuf_ref.at[0], sbuf_ref.at[1]
    pa_ref, pb_ref = pbuf_ref.at[0], pbuf_ref.at[1]

    def softmax_stage(m, s_ref, p_ref, kc, bias=None):
        st = s_ref[...]
        extra = extra_fn(kc)
        if extra is not None:
            st = st + extra
        if bias is not None:
            st = st + bias
        m_new = jnp.maximum(m, jnp.max(st, axis=0, keepdims=True))
        p_ref[...] = jnp.exp2(st - m_new).astype(BF)
        return m_new, jnp.exp2(m - m_new)

    def value_stage(alpha, p_ref, kc):
        acc_ref[...] = alpha * acc_ref[...] + jnp.dot(vt_fn(kc), p_ref[...], preferred_element_type=F32)

    def far_pair(i, carry):
        m, alpha_b = carry
        a = 2 * i
        sb_ref[...] = score_fn(a + 1)
        value_stage(alpha_b, pb_ref, jnp.maximum(a - 1, 0))
        m, alpha_a = softmax_stage(m, sa_ref, pa_ref, a)
        sa_ref[...] = score_fn(a + 2)
        value_stage(alpha_a, pa_ref, a)
        m, alpha_b = softmax_stage(m, sb_ref, pb_ref, a + 1)
        return m, alpha_b

    acc_ref[...] = jnp.zeros_like(acc_ref)
    pb_ref[...] = jnp.zeros_like(pb_ref)
    sa_ref[...] = score_fn(0)
    npair = lax.shift_right_logical(jnp.maximum(qt - 1, 0), 1)
    m, alpha_b = lax.fori_loop(0, npair, far_pair,
                               (jnp.full((1, rows), M_INIT, F32), jnp.ones((1, rows), F32)))
    a = 2 * npair
    odd = (qt & 1) == 1
    first = qt == 0
    sb_ref[...] = score_fn(a + 1)
    value_stage(alpha_b, pb_ref, jnp.maximum(a - 1, 0))
    m, alpha_a = softmax_stage(m, sa_ref, pa_ref, a,
                               jnp.where(odd, band_ref[1], jnp.where(first, band_ref[0], 0.0)))
    value_stage(alpha_a, pa_ref, a)
    m, alpha_b = softmax_stage(m, sb_ref, pb_ref, a + 1,
                               jnp.where(odd, band_ref[0], jnp.where(first, -BIG, band_ref[1])))
    value_stage(alpha_b, pb_ref, a + 1)

    @pl.when((qt >= 2) & jnp.logical_not(odd))
    def _():
        sa_ref[...] = score_fn(qt)
        _, alpha_a = softmax_stage(m, sa_ref, pa_ref, qt, band_ref[0])
        value_stage(alpha_a, pa_ref, qt)


ONES_ROWS = 16


def _store_transposed(src_ref, dst_ref, nchunk):
    dv = src_ref.shape[1]
    tail = jnp.where(lax.broadcasted_iota(I32, (ONES_ROWS, TQ), 0) == 0, 1.0, 0.0).astype(dst_ref.dtype)

    def body(kc, _):
        start = pl.multiple_of(kc * TQ, TQ)
        dst_ref[kc, 0:dv, :] = src_ref[pl.ds(start, TQ), :].astype(F32).T.astype(dst_ref.dtype)
        dst_ref[kc, dv:dv + ONES_ROWS, :] = tail
        return 0

    lax.fori_loop(0, nchunk, body, 0)


def _gelu_tanh(x):
    return 0.5 * x * (1.0 + jnp.tanh(math.sqrt(2.0 / math.pi) * (x + 0.044715 * (x * x * x))))


def _compress_kernel(*refs, nch):
    x_refs = refs[:CMP_STRIDE]
    pe_ref, w1_ref, w2_ref, o_ref = refs[CMP_STRIDE:]
    acc_a = [jnp.zeros((nch, HEAD_DIM), F32) for _ in range(KV_GROUPS)]
    acc_b = [jnp.zeros((nch, HEAD_DIM), F32) for _ in range(KV_GROUPS)]
    for l in range(CMP_STRIDE):
        x = x_refs[l][...].astype(F32)
        xa = (x + pe_ref[l:l + 1, :]).astype(BF)
        xb = (x + pe_ref[CMP_STRIDE + l:CMP_STRIDE + l + 1, :]).astype(BF)
        for g in range(KV_GROUPS):
            sl = slice(g * HEAD_DIM, (g + 1) * HEAD_DIM)
            acc_a[g] = acc_a[g] + jnp.dot(xa[:, sl], w1_ref[l], preferred_element_type=F32)
            acc_b[g] = acc_b[g] + jnp.dot(xb[:, sl], w1_ref[CMP_STRIDE + l], preferred_element_type=F32)
    for g in range(KV_GROUPS):
        pre = acc_a[g] + pltpu.roll(acc_b[g], nch - 1, axis=0)
        hid = _gelu_tanh(pre).astype(BF)
        o_ref[:, g * HEAD_DIM:(g + 1) * HEAD_DIM] = jnp.dot(hid, w2_ref[...], preferred_element_type=F32).astype(BF)


def nsa_compress(ybf, b, s, col_block, pe, w1, w2):
    ncols = ybf.shape[1]
    nch = s // CMP_STRIDE
    blk_w = KV_GROUPS * HEAD_DIM
    per_tok = ncols // blk_w
    y3 = ybf.reshape(b, nch, CMP_STRIDE * ncols)
    pe_t = jnp.tile(pe.astype(F32), (1, KV_GROUPS))
    in_specs = [pl.BlockSpec((None, nch, blk_w), (lambda bi, l=l: (bi, 0, l * per_tok + col_block)))
                for l in range(CMP_STRIDE)]
    in_specs += [pl.BlockSpec((CMP_LEN, blk_w), lambda bi: (0, 0)),
                 pl.BlockSpec((CMP_LEN, HEAD_DIM, HEAD_DIM), lambda bi: (0, 0, 0)),
                 pl.BlockSpec((HEAD_DIM, HEAD_DIM), lambda bi: (0, 0))]
    return pl.pallas_call(
        functools.partial(_compress_kernel, nch=nch),
        grid=(b,),
        in_specs=in_specs,
        out_specs=pl.BlockSpec((None, nch, blk_w), lambda bi: (bi, 0, 0)),
        out_shape=jax.ShapeDtypeStruct((b, nch, blk_w), BF),
        compiler_params=_cp(("parallel",)),
        name="nsa_compress",
    )(*([y3] * CMP_STRIDE), pe_t, w1.astype(BF), w2.astype(BF))


def _nsa_cmp_kernel(q_ref, kc_ref, vc_ref, cb_ref, ovt_ref, oc_ref, ns_ref, vct_ref, s_ref, *, n_sel, n_top):
    qt = pl.program_id(2)
    t0 = qt * TQ
    ncp = kc_ref.shape[0]
    pad = CMP_NEAR // 2

    @pl.when(qt == 0)
    def _():
        vct_ref[...] = vc_ref[...].astype(F32).T.astype(BF)
        s_ref[0:pad, :] = jnp.zeros((pad, HPG * TQ), F32)

    qs = _heads_to_rows(q_ref[...])
    s_ref[pad:pad + ncp, :] = _dot_nt(kc_ref[...], qs)
    near = pl.multiple_of(qt * (TQ // CMP_STRIDE), TQ // CMP_STRIDE)
    s_ref[pl.ds(near, CMP_NEAR), :] = s_ref[pl.ds(near, CMP_NEAR), :] + cb_ref[...]
    s = s_ref[pad:pad + ncp, :]
    key = lax.broadcasted_iota(I32, (ncp, HPG * TQ), 0)
    tq = t0 + (lax.broadcasted_iota(I32, (ncp, HPG * TQ), 1) & (TQ - 1))
    vis = (key * CMP_STRIDE + (CMP_LEN - 1)) <= tq
    s = jnp.where(vis, s, -BIG)
    m = jnp.max(s, axis=0, keepdims=True)
    p = jnp.where(vis, jnp.exp2(s - m), 0.0)
    l = jnp.sum(p, axis=0, keepdims=True)
    p = p * jnp.where(l > 0.0, 1.0 / l, 0.0)
    pb = p.astype(BF)
    oct = jnp.dot(vct_ref[...], pb, preferred_element_type=F32)
    score = jnp.zeros((LANES, TQ), F32)
    for j in range(HPG):
        cs = slice(j * TQ, (j + 1) * TQ)
        oc_ref[:, j * HEAD_DIM:(j + 1) * HEAD_DIM] = oct[:, cs].T.astype(oc_ref.dtype)
        score = score + jnp.dot(ovt_ref[...], pb[:, cs], preferred_element_type=F32)
    blk = lax.broadcasted_iota(I32, (LANES, TQ), 0)
    t = t0 + lax.broadcasted_iota(I32, (LANES, TQ), 1)
    cur = lax.shift_right_logical(t, int(math.log2(SEL_LEN)))
    forced = (blk == 0) | (blk == cur) | (blk == cur - 1)
    visible = blk * SEL_LEN <= t
    sc = jnp.where(forced, SEL_FORCE, jnp.where(visible, score, -1.0))
    sc = jnp.where(blk < n_sel, sc, -2.0)
    blkf = blk.astype(F32)

    def pick_one(_, carry):
        sc, sel = carry
        mx = jnp.max(sc, axis=0, keepdims=True)
        first = jnp.min(jnp.where(sc == mx, blkf, float(LANES)), axis=0, keepdims=True)
        hit = blkf == first
        return jnp.where(hit, -3.0, sc), jnp.where(hit, 1.0, sel)

    _, sel = lax.fori_loop(0, n_top, pick_one, (sc, jnp.zeros((LANES, TQ), F32)))
    ns_ref[...] = jnp.where(sel.T > 0.5, 0.0, BIG).astype(BF)


def nsa_cmp_select(ybf, kcmp, vcmp, cmpb, overlap_t, b, s):
    ncp = kcmp.shape[1]
    n_sel = s // SEL_LEN
    n_top = min(SEL_TOP, n_sel)
    y3 = ybf.reshape(b, s, ybf.shape[1])
    gw = HPG * HEAD_DIM
    return pl.pallas_call(
        functools.partial(_nsa_cmp_kernel, n_sel=n_sel, n_top=n_top),
        grid=(b, KV_GROUPS, s // TQ),
        in_specs=[pl.BlockSpec((None, TQ, gw), lambda bi, g, qt: (bi, qt, g)),
                  pl.BlockSpec((None, ncp, HEAD_DIM), lambda bi, g, qt: (bi, 0, g)),
                  pl.BlockSpec((None, ncp, HEAD_DIM), lambda bi, g, qt: (bi, 0, g)),
                  pl.BlockSpec((None, CMP_NEAR, HPG * TQ), lambda bi, g, qt: (g, 0, 0)),
                  pl.BlockSpec((LANES, ncp), lambda bi, g, qt: (0, 0))],
        out_specs=[pl.BlockSpec((None, TQ, gw), lambda bi, g, qt: (bi, qt, g)),
                   pl.BlockSpec((None, None, TQ, LANES), lambda bi, g, qt: (bi, g, qt, 0))],
        out_shape=[jax.ShapeDtypeStruct((b, s, KV_GROUPS * gw), BF),
                   jax.ShapeDtypeStruct((b, KV_GROUPS, s, LANES), BF)],
        scratch_shapes=[pltpu.VMEM((HEAD_DIM, ncp), BF),
                        pltpu.VMEM((ncp + CMP_NEAR, HPG * TQ), F32)],
        compiler_params=_cp(("parallel", "parallel", "arbitrary")),
        name="nsa_cmp_select",
    )(y3, kcmp, vcmp, cmpb, overlap_t)


def _nsa_main_kernel(q_ref, ks_ref, vs_ref, kw_ref, vw_ref, ns_ref, band_ref, gl_ref, oc_ref, o_ref,
                     vst_ref, vwt_ref, accs_ref, accw_ref, sbuf_ref, pbuf_ref, *, nchunk):
    qt = pl.program_id(2)

    @pl.when(qt == 0)
    def _():
        _store_transposed(vs_ref, vst_ref, nchunk)
        _store_transposed(vw_ref, vwt_ref, nchunk)

    q = q_ref[...]
    ns = ns_ref[...]
    qs = _heads_to_rows(q)
    qp = jnp.concatenate([qs, jnp.concatenate([ns] * HPG, axis=0)], axis=1)

    def sel_scores(kc):
        start = pl.multiple_of(kc * TQ, TQ)
        k = ks_ref[pl.ds(start, TQ), :]
        krow = lax.broadcasted_iota(I32, (TQ, LANES), 0)
        klane = lax.broadcasted_iota(I32, (TQ, LANES), 1)
        kblk = lax.shift_right_logical(krow, int(math.log2(SEL_LEN))) + kc * (TQ // SEL_LEN)
        oh = jnp.where(klane == kblk, -1.0, 0.0).astype(BF)
        return _dot_nt(jnp.concatenate([k, oh], axis=1), qp)

    _causal_chunk_attention(qt, sel_scores, lambda kc: None, lambda kc: vst_ref[kc],
                            band_ref, accs_ref, sbuf_ref, pbuf_ref)

    accw_ref[...] = jnp.zeros_like(accw_ref)
    m_w = [jnp.full((1, TQ), M_INIT, F32)] * HPG
    for r in (2, 1, 0):
        kc = jnp.maximum(qt - r, 0)
        start = pl.multiple_of(kc * TQ, TQ)
        st_all = _dot_nt(kw_ref[pl.ds(start, TQ), :], qs) + jnp.where(qt >= r, band_ref[r], -BIG)
        for j in range(HPG):
            cs = slice(j * TQ, (j + 1) * TQ)
            st = st_all[:, cs]
            m_new = jnp.maximum(m_w[j], jnp.max(st, axis=0, keepdims=True))
            alpha = jnp.exp2(m_w[j] - m_new)
            p = jnp.exp2(st - m_new).astype(BF)
            m_w[j] = m_new
            accw_ref[:, cs] = alpha * accw_ref[:, cs] + jnp.dot(vwt_ref[kc], p, preferred_element_type=F32)

    gates = 1.0 / (1.0 + jnp.exp(-gl_ref[...]))
    oc = oc_ref[...].astype(F32)
    for j in range(HPG):
        sl = slice(j * HEAD_DIM, (j + 1) * HEAD_DIM)
        cs = slice(j * TQ, (j + 1) * TQ)
        o_s = (accs_ref[0:HEAD_DIM, cs] * (1.0 / accs_ref[HEAD_DIM:HEAD_DIM + 1, cs])).T
        o_w = (accw_ref[0:HEAD_DIM, cs] * (1.0 / accw_ref[HEAD_DIM:HEAD_DIM + 1, cs])).T
        o = (gates[:, j:j + 1] * oc[:, sl] + gates[:, HPG + j:HPG + j + 1] * o_s
             + gates[:, 2 * HPG + j:2 * HPG + j + 1] * o_w)
        o_ref[:, sl] = o.astype(o_ref.dtype)


def nsa_main(ybf, nsel, band, gl, oc, b, s, col_ksl, col_vsl, col_kw, col_vw):
    y3 = ybf.reshape(b, s, ybf.shape[1])
    gw = HPG * HEAD_DIM
    kv_spec = lambda cb: pl.BlockSpec((None, s, HEAD_DIM), lambda bi, g, qt: (bi, 0, cb + g))
    nchunk = s // TQ
    return pl.pallas_call(
        functools.partial(_nsa_main_kernel, nchunk=nchunk),
        grid=(b, KV_GROUPS, nchunk),
        in_specs=[pl.BlockSpec((None, TQ, gw), lambda bi, g, qt: (bi, qt, g)),
                  kv_spec(col_ksl), kv_spec(col_vsl), kv_spec(col_kw), kv_spec(col_vw),
                  pl.BlockSpec((None, None, TQ, LANES), lambda bi, g, qt: (bi, g, qt, 0)),
                  pl.BlockSpec((None, 3, TQ, HPG * TQ), lambda bi, g, qt: (g, 0, 0, 0)),
                  pl.BlockSpec((None, TQ, LANES), lambda bi, g, qt: (bi, qt, g)),
                  pl.BlockSpec((None, TQ, gw), lambda bi, g, qt: (bi, qt, g))],
        out_specs=pl.BlockSpec((None, TQ, gw), lambda bi, g, qt: (bi, qt, g)),
        out_shape=jax.ShapeDtypeStruct((b, s, KV_GROUPS * gw), BF),
        scratch_shapes=[pltpu.VMEM((nchunk, HEAD_DIM + ONES_ROWS, TQ), BF),
                        pltpu.VMEM((nchunk, HEAD_DIM + ONES_ROWS, TQ), BF),
                        pltpu.VMEM((HEAD_DIM + ONES_ROWS, HPG * TQ), F32),
                        pltpu.VMEM((HEAD_DIM + ONES_ROWS, HPG * TQ), F32),
                        pltpu.VMEM((2, TQ, HPG * TQ), F32), pltpu.VMEM((2, TQ, HPG * TQ), BF)],
        compiler_params=_cp(("parallel", "parallel", "arbitrary")),
        name="nsa_main",
    )(y3, y3, y3, y3, y3, nsel, band, gl.reshape(b, s, gl.shape[1]), oc)


def _dsa_index_kernel(qi_ref, kw_ref, wq_ref, nm_ref, ka_ref, kb_ref, key_ref, d_ref, *, k_sel, nchunk,
                      idx_bits):
    qt = pl.program_id(1)
    t0 = qt * TQ
    half = LANES // 2

    @pl.when(qt == 0)
    def _():
        lane = lax.broadcasted_iota(I32, kw_ref.shape, 1)
        ka = jnp.where(lane < IDX_DIM, kw_ref[...], 0.0)
        ka_ref[...] = ka.astype(BF)
        kb_ref[...] = pltpu.roll(ka, half, axis=1).astype(BF)

    qi = qi_ref[...]
    npair = IDX_HEADS // 2
    lq = jnp.concatenate([qi[:, p * LANES:(p + 1) * LANES] for p in range(npair)], axis=0)
    wt = wq_ref[...].T
    w_r = [wt[IDX_DIM + h:IDX_DIM + h + 1, :] for h in range(IDX_HEADS)]
    kpos = lax.broadcasted_iota(I32, (TQ, TQ), 0)
    qpos = t0 + lax.broadcasted_iota(I32, (TQ, TQ), 1)

    def head_dots(kc, buf):
        start = pl.multiple_of(jnp.minimum(kc, nchunk - 1) * TQ, TQ)
        d_ref[buf, 0] = _dot_nt(ka_ref[pl.ds(start, TQ), :], lq)
        d_ref[buf, 1] = _dot_nt(kb_ref[pl.ds(start, TQ), :], lq)

    def to_keys(kc, buf):
        sc = jnp.zeros((TQ, TQ), F32)
        for p in range(npair):
            cs = slice(p * TQ, (p + 1) * TQ)
            sc = (sc + w_r[2 * p] * jnp.maximum(d_ref[buf, 0, :, cs], 0.0)
                  + w_r[2 * p + 1] * jnp.maximum(d_ref[buf, 1, :, cs], 0.0))
        sc = jnp.where(sc == 0.0, 0.0, sc)
        sc = jnp.where(kc * TQ + kpos <= qpos, sc, NEG_INF)
        bits = lax.bitcast_convert_type(sc, I32)
        key_ref[kc] = jnp.where(bits < 0, bits ^ 0x7FFFFFFF, bits)

    def score_pair(i, _):
        a = 2 * i
        head_dots(a + 1, 1)
        to_keys(a, 0)
        head_dots(a + 2, 0)
        to_keys(a + 1, 1)
        return 0

    nproc = qt + 1
    head_dots(0, 0)
    lax.fori_loop(0, lax.shift_right_logical(nproc, 1), score_pair, 0)

    @pl.when((nproc & 1) == 1)
    def _():
        to_keys(nproc - 1, 0)

    def count(pred):
        def one(kc):
            return jnp.sum(jnp.where(pred(key_ref[kc], kc), 1.0, 0.0), axis=0, keepdims=True)

        def body(i, acc):
            return acc + (one(2 * i) + one(2 * i + 1))

        acc = lax.fori_loop(0, lax.shift_right_logical(nproc, 1), body, jnp.zeros((1, TQ), F32))
        return acc + jnp.where((nproc & 1) == 1, one(nproc - 1), 0.0)

    def count_ge(cand):
        return count(lambda k, kc: k >= cand)

    kf = float(k_sel)
    thr0 = jnp.full((1, TQ), -2 ** 31, I32)
    done0 = jnp.where(count_ge(thr0) == kf, 1.0, 0.0)

    def search_cond(state):
        i, _, done = state
        return (i < 32) & (jnp.min(done) < 0.5)

    def search_body(state):
        i, thr, done = state
        cand = thr ^ lax.shift_left(jnp.int32(1), jnp.int32(31) - i)
        cnt = count_ge(cand)
        take = (cnt >= kf) & (done < 0.5)
        thr = jnp.where(take, cand, thr)
        done = jnp.where(take & (cnt == kf), 1.0, done)
        return i + 1, thr, done

    _, thr, done = lax.while_loop(search_cond, search_body, (jnp.int32(0), thr0, done0))

    def tie_break(_):
        need = kf - count_ge(thr + 1)

        def ties_below(bound):
            return count(lambda k, kc: (k == thr) & (kc * TQ + kpos < bound))

        def idx_bit(i, jm):
            cand = jm | lax.shift_left(jnp.int32(1), jnp.int32(idx_bits - 1) - i)
            return jnp.where(ties_below(cand) < need, cand, jm)

        jm = lax.fori_loop(0, idx_bits, idx_bit, jnp.zeros((1, TQ), I32))
        return jnp.where(done > 0.5, jnp.int32(2 ** 30), jm)

    jm = lax.cond(jnp.min(done) < 0.5, tie_break, lambda _: jnp.full((1, TQ), 2 ** 30, I32), 0)

    def write_chunk(kc, _):
        k = key_ref[kc]
        kidx = kc * TQ + kpos
        sel = (kidx <= qpos) & ((k > thr) | ((k == thr) & (kidx <= jm)))
        nm_ref[kc] = jnp.where(sel, 0.0, -BIG).astype(BF)
        return 0

    lax.fori_loop(0, nproc, write_chunk, 0)

    def fill_chunk(kc, _):
        nm_ref[kc] = jnp.full((TQ, TQ), -BIG, BF)
        return 0

    lax.fori_loop(nproc, nchunk, fill_chunk, 0)


def dsa_index(qbf, kw, b, s, col_qi):
    nchunk = s // TQ
    k_sel = min(IDX_TOPK, s // 4)
    q3 = qbf.reshape(b, s, qbf.shape[1])
    kw3 = kw.reshape(b, s, LANES)
    return pl.pallas_call(
        functools.partial(_dsa_index_kernel, k_sel=k_sel, nchunk=nchunk, idx_bits=int(math.log2(s))),
        grid=(b, nchunk),
        in_specs=[pl.BlockSpec((None, TQ, IDX_HEADS * IDX_DIM), lambda bi, qt: (bi, qt, col_qi)),
                  pl.BlockSpec((None, s, LANES), lambda bi, qt: (bi, 0, 0)),
                  pl.BlockSpec((None, TQ, LANES), lambda bi, qt: (bi, qt, 0))],
        out_specs=pl.BlockSpec((None, None, nchunk, TQ, TQ), lambda bi, qt: (bi, qt, 0, 0, 0)),
        out_shape=jax.ShapeDtypeStruct((b, nchunk, nchunk, TQ, TQ), BF),
        scratch_shapes=[pltpu.VMEM((s, LANES), BF), pltpu.VMEM((s, LANES), BF),
                        pltpu.VMEM((nchunk, TQ, TQ), I32),
                        pltpu.VMEM((2, 2, TQ, (IDX_HEADS // 2) * TQ), F32)],
        compiler_params=_cp(("parallel", "arbitrary")),
        name="dsa_index",
    )(q3, kw3, kw3)


def _dsa_attn_kernel(q_ref, c_ref, nm_ref, band_ref, wuk_ref, wuvt_ref, o_ref, ct_ref, acc_ref, sbuf_ref,
                     pbuf_ref, *, nchunk):
    qt = pl.program_id(1)
    hg = pl.program_id(2)

    @pl.when((qt == 0) & (hg == 0))
    def _():
        _store_transposed(c_ref, ct_ref, nchunk)

    q = q_ref[...]
    ql = jnp.concatenate(
        [(jnp.dot(q[:, j * HEAD_DIM:(j + 1) * HEAD_DIM], wuk_ref[j], preferred_element_type=F32)
          * (HEAD_DIM ** -0.5 * LOG2E)).astype(BF) for j in range(HPG)], axis=0)

    def scores(kc):
        start = pl.multiple_of(kc * TQ, TQ)
        return _dot_nt(c_ref[pl.ds(start, TQ), :], ql)

    def member_mask(kc):
        nm = nm_ref[kc].astype(F32)
        return jnp.concatenate([nm] * HPG, axis=1)

    _causal_chunk_attention(qt, scores, member_mask, lambda kc: ct_ref[kc], band_ref, acc_ref,
                            sbuf_ref, pbuf_ref)
    o_lat = (acc_ref[0:KV_RANK, :] * (1.0 / acc_ref[KV_RANK:KV_RANK + 1, :])).astype(BF)
    for j in range(HPG):
        ot = jnp.dot(wuvt_ref[j], o_lat[:, j * TQ:(j + 1) * TQ], preferred_element_type=F32)
        o_ref[:, j * HEAD_DIM:(j + 1) * HEAD_DIM] = ot.T.astype(o_ref.dtype)


def dsa_attention(qbf, cn, nmask, band, w_uk, w_uv, b, s):
    nchunk = s // TQ
    q3 = qbf.reshape(b, s, qbf.shape[1])
    c3 = cn.reshape(b, s, KV_RANK)
    gw = HPG * HEAD_DIM
    return pl.pallas_call(
        functools.partial(_dsa_attn_kernel, nchunk=nchunk),
        grid=(b, nchunk, N_HEADS // HPG),
        in_specs=[pl.BlockSpec((None, TQ, gw), lambda bi, qt, hg: (bi, qt, hg)),
                  pl.BlockSpec((None, s, KV_RANK), lambda bi, qt, hg: (bi, 0, 0)),
                  pl.BlockSpec((None, None, nchunk, TQ, TQ), lambda bi, qt, hg: (bi, qt, 0, 0, 0)),
                  pl.BlockSpec((None, 2, TQ, HPG * TQ), lambda bi, qt, hg: (hg, 0, 0, 0)),
                  pl.BlockSpec((HPG, HEAD_DIM, KV_RANK), lambda bi, qt, hg: (hg, 0, 0)),
                  pl.BlockSpec((HPG, HEAD_DIM, KV_RANK), lambda bi, qt, hg: (hg, 0, 0))],
        out_specs=pl.BlockSpec((None, TQ, gw), lambda bi, qt, hg: (bi, qt, hg)),
        out_shape=jax.ShapeDtypeStruct((b, s, N_HEADS * HEAD_DIM), BF),
        scratch_shapes=[pltpu.VMEM((nchunk, KV_RANK + ONES_ROWS, TQ), BF),
                        pltpu.VMEM((KV_RANK + ONES_ROWS, HPG * TQ), F32),
                        pltpu.VMEM((2, TQ, HPG * TQ), F32), pltpu.VMEM((2, TQ, HPG * TQ), BF)],
        compiler_params=_cp(("parallel", "arbitrary", "arbitrary")),
        name="dsa_attention",
    )(q3, c3, nmask, band, w_uk.astype(BF), jnp.swapaxes(w_uv, 1, 2).astype(BF))


ROUTER_TM = 256
MOE_TM = 512


def _router_kernel(x_ref, w_ref, b_ref, o_ref, cnt_ref, carry_ref):
    i = pl.program_id(0)

    @pl.when(i == 0)
    def _():
        carry_ref[...] = jnp.zeros_like(carry_ref)

    tm = x_ref.shape[0]
    logits = jnp.dot(x_ref[...], w_ref[...], preferred_element_type=F32,
                     precision=lax.Precision.HIGHEST) + b_ref[...]
    lane = lax.broadcasted_iota(I32, (tm, LANES), 1)
    lanef = lane.astype(F32)
    lg = jnp.where(lane < N_EXPERTS, logits, -BIG)
    m1 = jnp.max(lg, axis=-1, keepdims=True)
    i1 = jnp.min(jnp.where(lg == m1, lanef, float(LANES)), axis=-1, keepdims=True)
    lg2 = jnp.where(lanef == i1, -BIG, lg)
    m2 = jnp.max(lg2, axis=-1, keepdims=True)
    i2 = jnp.min(jnp.where(lg2 == m2, lanef, float(LANES)), axis=-1, keepdims=True)
    e2 = jnp.exp(m2 - m1)
    den = 1.0 + e2
    w1 = 1.0 / den
    w2 = e2 / den
    hit1 = lanef == i1
    hit2 = lanef == i2
    onehot = jnp.where(hit1 | hit2, 1.0, 0.0)
    r = lax.broadcasted_iota(I32, (tm, tm), 0)
    c = lax.broadcasted_iota(I32, (tm, tm), 1)
    tri = jnp.where(c < r, 1.0, 0.0).astype(BF)
    before = jnp.dot(tri, onehot.astype(BF), preferred_element_type=F32) + carry_ref[...]
    rank1 = jnp.sum(jnp.where(hit1, before, 0.0), axis=-1, keepdims=True)
    rank2 = jnp.sum(jnp.where(hit2, before, 0.0), axis=-1, keepdims=True)
    carry_ref[...] = carry_ref[...] + jnp.sum(onehot, axis=0, keepdims=True)
    vals = (i1, i2, w1, w2, rank1, rank2)
    out = jnp.zeros((tm, LANES), F32)
    for k, v in enumerate(vals):
        out = jnp.where(lane == k, v, out)
    o_ref[...] = out
    cnt_ref[...] = jnp.broadcast_to(carry_ref[...], cnt_ref.shape)


def moe_router(x, w_router, b_router):
    n, d = x.shape
    wp = jnp.pad(w_router.astype(F32), ((0, 0), (0, LANES - N_EXPERTS)))
    bp = jnp.pad(b_router.astype(F32), (0, LANES - N_EXPERTS)).reshape(1, LANES)
    return pl.pallas_call(
        _router_kernel,
        grid=(n // ROUTER_TM,),
        in_specs=[pl.BlockSpec((ROUTER_TM, d), lambda i: (i, 0)),
                  pl.BlockSpec((d, LANES), lambda i: (0, 0)),
                  pl.BlockSpec((1, LANES), lambda i: (0, 0))],
        out_specs=[pl.BlockSpec((ROUTER_TM, LANES), lambda i: (i, 0)),
                   pl.BlockSpec((8, LANES), lambda i: (0, 0))],
        out_shape=[jax.ShapeDtypeStruct((n, LANES), F32), jax.ShapeDtypeStruct((8, LANES), F32)],
        scratch_shapes=[pltpu.VMEM((1, LANES), F32)],
        compiler_params=_cp(("arbitrary",)),
        name="moe_router",
    )(x, wp, bp)


GATHER_ROWS = 1024


def _row_slabs(x):
    return x.reshape(x.shape[0], x.shape[1] // LANES, LANES)


def _row_gather_kernel(idx_ref, x_hbm, o_ref, sem):
    def start(r, _):
        pltpu.make_async_copy(x_hbm.at[idx_ref[0, 0, r]], o_ref.at[r], sem).start()
        return 0

    lax.fori_loop(0, GATHER_ROWS, start, 0)
    pltpu.make_async_copy(x_hbm.at[pl.ds(0, GATHER_ROWS)], o_ref, sem).wait()


def row_gather(x, row_idx):
    n_rows = row_idx.shape[0]
    nt = n_rows // GATHER_ROWS
    x3 = _row_slabs(x)
    out = pl.pallas_call(
        _row_gather_kernel,
        grid=(nt,),
        in_specs=[pl.BlockSpec((1, 1, GATHER_ROWS), lambda i: (i, 0, 0), memory_space=pltpu.SMEM),
                  pl.BlockSpec(memory_space=pl.ANY)],
        out_specs=pl.BlockSpec((GATHER_ROWS,) + x3.shape[1:], lambda i: (i, 0, 0)),
        out_shape=jax.ShapeDtypeStruct((n_rows,) + x3.shape[1:], x.dtype),
        scratch_shapes=[pltpu.SemaphoreType.DMA],
        compiler_params=_cp(("arbitrary",)),
        name="moe_row_gather",
    )(row_idx.reshape(nt, 1, GATHER_ROWS), x3)
    return out.reshape(n_rows, x.shape[1])


def _moe_up_kernel(te_ref, tv_ref, x_ref, wg_ref, wu_ref, o_ref):
    i = pl.program_id(1)

    @pl.when(tv_ref[i] > 0)
    def _():
        x = x_ref[...]
        o_ref[...] = (_silu(_dot(x, wg_ref[...])) * _dot(x, wu_ref[...])).astype(o_ref.dtype)

    @pl.when(tv_ref[i] == 0)
    def _():
        o_ref[...] = jnp.zeros_like(o_ref)


def moe_up(xs, w_gate, w_up, tile_e, tile_v, tn):
    n_rows, d = xs.shape
    f = w_gate.shape[2]
    nt = n_rows // MOE_TM
    grid_spec = pltpu.PrefetchScalarGridSpec(
        num_scalar_prefetch=2,
        grid=(f // tn, nt),
        in_specs=[pl.BlockSpec((MOE_TM, d), lambda j, i, te, tv: (i, 0)),
                  pl.BlockSpec((None, d, tn), lambda j, i, te, tv: (te[i], 0, j)),
                  pl.BlockSpec((None, d, tn), lambda j, i, te, tv: (te[i], 0, j))],
        out_specs=pl.BlockSpec((MOE_TM, tn), lambda j, i, te, tv: (i, j)),
    )
    return pl.pallas_call(
        _moe_up_kernel,
        grid_spec=grid_spec,
        out_shape=jax.ShapeDtypeStruct((n_rows, f), BF),
        compiler_params=_cp(("parallel", "arbitrary")),
        name="moe_up",
    )(tile_e, tile_v, xs, w_gate, w_up)


def _moe_down_kernel(te_ref, tv_ref, h_ref, wd_ref, o_ref):
    i = pl.program_id(1)

    @pl.when(tv_ref[i] > 0)
    def _():
        o_ref[...] = _dot(h_ref[...], wd_ref[...])

    @pl.when(tv_ref[i] == 0)
    def _():
        o_ref[...] = jnp.zeros_like(o_ref)


def moe_down(h, w_down, tile_e, tile_v, tn):
    n_rows, f = h.shape
    d = w_down.shape[2]
    nt = n_rows // MOE_TM
    grid_spec = pltpu.PrefetchScalarGridSpec(
        num_scalar_prefetch=2,
        grid=(d // tn, nt),
        in_specs=[pl.BlockSpec((MOE_TM, f), lambda j, i, te, tv: (i, 0)),
                  pl.BlockSpec((None, f, tn), lambda j, i, te, tv: (te[i], 0, j))],
        out_specs=pl.BlockSpec((MOE_TM, tn), lambda j, i, te, tv: (i, j)),
    )
    return pl.pallas_call(
        _moe_down_kernel,
        grid_spec=grid_spec,
        out_shape=jax.ShapeDtypeStruct((n_rows, d), F32),
        compiler_params=_cp(("parallel", "arbitrary")),
        name="moe_down",
    )(tile_e, tile_v, h, w_down)


COMBINE_TM = 256


def _moe_combine_kernel(d1_ref, d2_ref, n1_ref, n2_ref, y_hbm, x_ref, rw_ref, g_ref, b_ref, o_ref, buf, sem,
                        *, nt):
    i = pl.program_id(0)
    slot = i & 1
    nslab = buf.shape[3]

    def fetch(i1_ref, i2_ref, s):
        def start(r, _):
            pltpu.make_async_copy(y_hbm.at[i1_ref[0, 0, r]], buf.at[s, 0, r], sem.at[s]).start()
            pltpu.make_async_copy(y_hbm.at[i2_ref[0, 0, r]], buf.at[s, 1, r], sem.at[s]).start()
            return 0

        lax.fori_loop(0, COMBINE_TM, start, 0)

    @pl.when(i == 0)
    def _():
        fetch(d1_ref, d2_ref, 0)

    @pl.when(i + 1 < nt)
    def _():
        fetch(n1_ref, n2_ref, 1 - slot)

    for k in range(2):
        pltpu.make_async_copy(y_hbm.at[pl.ds(0, COMBINE_TM)], buf.at[slot, k], sem.at[slot]).wait()
    rw = rw_ref[...]
    w1 = rw[:, 2:3]
    w2 = rw[:, 3:4]
    z = []
    tot = jnp.zeros((COMBINE_TM, 1), F32)
    for c in range(nslab):
        sl = slice(c * LANES, (c + 1) * LANES)
        zc = ALPHA * x_ref[:, sl] + (w1 * buf[slot, 0, :, c, :] + w2 * buf[slot, 1, :, c, :])
        z.append(zc)
        tot = tot + jnp.sum(zc, axis=-1, keepdims=True)
    d = nslab * LANES
    mu = tot * (1.0 / d)
    ss = jnp.zeros((COMBINE_TM, 1), F32)
    for c in range(nslab):
        z[c] = z[c] - mu
        ss = ss + jnp.sum(z[c] * z[c], axis=-1, keepdims=True)
    r = lax.rsqrt(ss * (1.0 / d) + LN_EPS)
    for c in range(nslab):
        sl = slice(c * LANES, (c + 1) * LANES)
        o_ref[:, sl] = z[c] * r * g_ref[:, sl] + b_ref[:, sl]


def moe_combine(y_rows, x, route, dest1, dest2, g, beta):
    n, d = x.shape
    nt = n // COMBINE_TM
    y3 = _row_slabs(y_rows)
    idx_spec = pl.BlockSpec((1, 1, COMBINE_TM), lambda i: (i, 0, 0), memory_space=pltpu.SMEM)
    nxt_spec = pl.BlockSpec((1, 1, COMBINE_TM), lambda i: (jnp.minimum(i + 1, nt - 1), 0, 0),
                            memory_space=pltpu.SMEM)
    d1 = dest1.reshape(nt, 1, COMBINE_TM)
    d2 = dest2.reshape(nt, 1, COMBINE_TM)
    return pl.pallas_call(
        functools.partial(_moe_combine_kernel, nt=nt),
        grid=(nt,),
        in_specs=[idx_spec, idx_spec, nxt_spec, nxt_spec,
                  pl.BlockSpec(memory_space=pl.ANY),
                  pl.BlockSpec((COMBINE_TM, d), lambda i: (i, 0)),
                  pl.BlockSpec((COMBINE_TM, LANES), lambda i: (i, 0)),
                  pl.BlockSpec((1, d), lambda i: (0, 0)),
                  pl.BlockSpec((1, d), lambda i: (0, 0))],
        out_specs=pl.BlockSpec((COMBINE_TM, d), lambda i: (i, 0)),
        out_shape=jax.ShapeDtypeStruct((n, d), F32),
        scratch_shapes=[pltpu.VMEM((2, 2, COMBINE_TM) + y3.shape[1:], F32), pltpu.SemaphoreType.DMA((2,))],
        compiler_params=_cp(("arbitrary",)),
        name="moe_combine",
    )(d1, d2, d1, d2, y3, x, route, g.reshape(1, d).astype(F32), beta.reshape(1, d).astype(F32))


def moe_layer(x, xb, w_router, b_router, w_gate, w_up, w_down, g, beta):
    n, d = x.shape
    route, cnt = moe_router(x, w_router, b_router)
    e1 = route[:, 0].astype(I32)
    e2 = route[:, 1].astype(I32)
    counts = cnt[0, :N_EXPERTS].astype(I32)
    padded = (counts + MOE_TM - 1) // MOE_TM * MOE_TM
    pad_end = jnp.cumsum(padded)
    pad_start = pad_end - padded
    dest1 = pad_start[e1] + route[:, 4].astype(I32)
    dest2 = pad_start[e2] + route[:, 5].astype(I32)
    n_rows = 2 * n + N_EXPERTS * MOE_TM
    nt = n_rows // MOE_TM
    tok = jnp.arange(n, dtype=I32)
    row_tok = jnp.zeros((n_rows,), I32).at[dest1].set(tok).at[dest2].set(tok)
    tile_start = jnp.arange(nt, dtype=I32) * MOE_TM
    tile_e = jnp.minimum(jnp.searchsorted(pad_end, tile_start, side='right'), N_EXPERTS - 1).astype(I32)
    tile_v = (tile_start < pad_end[-1]).astype(I32)
    xs = row_gather(xb, row_tok)
    h = moe_up(xs, w_gate, w_up, tile_e, tile_v, tn=min(1024, w_gate.shape[2]))
    y_rows = moe_down(h, w_down, tile_e, tile_v, tn=min(512, d))
    return moe_combine(y_rows, x, route, dest1, dest2, g, beta)


def _cmp_sel_overlap(n_cmp_pad, n_sel):
    i = np.arange(n_cmp_pad)[:, None]
    j = np.arange(LANES)[None, :]
    lo = np.maximum(i * CMP_STRIDE, j * SEL_LEN)
    hi = np.minimum(i * CMP_STRIDE + CMP_LEN, (j + 1) * SEL_LEN)
    ov = np.maximum(hi - lo, 0) / CMP_LEN
    ov[:, n_sel:] = 0.0
    return ov.astype(np.float32)


def _gate_columns():
    src = -np.ones((KV_GROUPS * LANES,), np.int64)
    for g in range(KV_GROUPS):
        for j in range(HPG):
            for br in range(3):
                src[g * LANES + br * HPG + j] = g * HPG * 3 + j * 3 + br
    return src


def nsa_layer(xb, b, s, w_in, pe_k, w1_k, w2_k, pe_v, w1_v, w2_v, band, cmpb):
    n = xb.shape[0]
    hd = N_HEADS * HEAD_DIM
    gw = KV_GROUPS * HEAD_DIM
    w_main = w_in[:, :hd + 6 * gw].astype(BF)
    scale = jnp.concatenate([jnp.full((hd,), HEAD_DIM ** -0.5 * LOG2E, F32), jnp.ones((6 * gw,), F32)])
    ybf = matmul_scaled(xb, w_main, scale, BF, tm=512, tn=1280)
    src = _gate_columns()
    w_gl = jnp.where(jnp.asarray(src >= 0), w_in[:, hd + 6 * gw:][:, np.maximum(src, 0)], 0.0).astype(BF)
    gl = matmul_scaled(xb, w_gl, jnp.ones((w_gl.shape[1],), F32), F32, tm=512, tn=512)
    cb = hd // gw
    kcmp = nsa_compress(ybf, b, s, cb + 0, pe_k, w1_k, w2_k)
    vcmp = nsa_compress(ybf, b, s, cb + 1, pe_v, w1_v, w2_v)
    overlap_t = jnp.asarray(_cmp_sel_overlap(s // CMP_STRIDE, s // SEL_LEN).T).astype(BF)
    oc, nsel = nsa_cmp_select(ybf, kcmp, vcmp, cmpb, overlap_t, b, s)
    c128 = hd // HEAD_DIM
    o = nsa_main(ybf, nsel, band, gl, oc, b, s,
                 col_ksl=c128 + 2 * KV_GROUPS, col_vsl=c128 + 3 * KV_GROUPS,
                 col_kw=c128 + 4 * KV_GROUPS, col_vw=c128 + 5 * KV_GROUPS)
    return o.reshape(n, hd)


def dsa_layer(xb, b, s, w_in, kv_norm_g, w_uk, w_uv, band):
    n = xb.shape[0]
    hd = N_HEADS * HEAD_DIM
    o_c = hd
    o_qi = hd + KV_RANK
    o_ki = o_qi + IDX_HEADS * IDX_DIM
    o_wi = o_ki + IDX_DIM
    w_q = jnp.concatenate([w_in[:, :hd], w_in[:, o_qi:o_ki]], axis=1).astype(BF)
    scale_q = jnp.concatenate([jnp.ones((hd,), F32), jnp.full((IDX_HEADS * IDX_DIM,), IDX_DIM ** -0.5, F32)])
    qbf = matmul_scaled(xb, w_q, scale_q, BF, tm=512, tn=1280)
    cn = matmul_rmsnorm(xb, w_in[:, o_c:o_qi].astype(BF), kv_norm_g, tm=512)
    w_kw = jnp.pad(w_in[:, o_ki:], ((0, 0), (0, LANES - IDX_DIM - IDX_HEADS))).astype(BF)
    scale_kw = jnp.concatenate([jnp.ones((IDX_DIM,), F32), jnp.full((IDX_HEADS,), IDX_HEADS ** -0.5, F32),
                                jnp.zeros((LANES - IDX_DIM - IDX_HEADS,), F32)])
    kw = matmul_scaled(xb, w_kw, scale_kw, F32, tm=512, tn=LANES)
    nmask = dsa_index(qbf, kw, b, s, col_qi=hd // (IDX_HEADS * IDX_DIM))
    o = dsa_attention(qbf, cn, nmask, band[:, :2], w_uk, w_uv, b, s)
    return o.reshape(n, hd)


def kernel(x, rel_bias, nsa_w_in, nsa_cmp_pe_k, nsa_cmp_w1_k, nsa_cmp_w2_k, nsa_cmp_pe_v, nsa_cmp_w1_v, nsa_cmp_w2_v, nsa_w_out, dsa_w_in, dsa_kv_norm_g, dsa_w_uk, dsa_w_uv, dsa_w_out, ffn_w_gate, ffn_w_up, ffn_w_down, moe_w_router, moe_b_router, moe_w_gate, moe_w_up, moe_w_down, ln_mix_g, ln_mix_b, ln_ffn_g, ln_ffn_b):
    b, s, d = x.shape
    n = b * s
    assert s % TQ == 0 and s // SEL_LEN <= LANES
    x0 = x.reshape(n, d)
    band, cmpb = _bias_tables(rel_bias)
    o = nsa_layer(x0.astype(BF), b, s, nsa_w_in, nsa_cmp_pe_k, nsa_cmp_w1_k, nsa_cmp_w2_k,
                  nsa_cmp_pe_v, nsa_cmp_w1_v, nsa_cmp_w2_v, band, cmpb)
    x1, x1b = matmul_residual_ln(o, nsa_w_out.astype(BF), x0, ln_mix_g[0], ln_mix_b[0], tm=256, tk=d)
    d_ff = ffn_w_gate.shape[1]
    hff = swiglu_up(x1b, ffn_w_gate.astype(BF), ffn_w_up.astype(BF), tm=512,
                    tn=d_ff // 4 if d_ff % (4 * LANES) == 0 else 512)
    x2, x2b = matmul_residual_ln(hff, ffn_w_down.astype(BF), x1, ln_ffn_g[0], ln_ffn_b[0], tm=256,
                                 tk=hff.shape[1] // 2)
    o = dsa_layer(x2b, b, s, dsa_w_in, dsa_kv_norm_g, dsa_w_uk, dsa_w_uv, band)
    x3, x3b = matmul_residual_ln(o, dsa_w_out.astype(BF), x2, ln_mix_g[1], ln_mix_b[1], tm=256, tk=d)
    out = moe_layer(x3, x3b, moe_w_router, moe_b_router, moe_w_gate, moe_w_up, moe_w_down,
                    ln_ffn_g[1], ln_ffn_b[1])
    return out.reshape(b, s, d)
```

```python
import functools
import math

import numpy as np
import jax
import jax.numpy as jnp
from jax import lax
from jax.experimental import pallas as pl
from jax.experimental.pallas import tpu as pltpu

F32 = jnp.float32
BF = jnp.bfloat16
I32 = jnp.int32

N_HEADS = 16
HEAD_DIM = 128
KV_GROUPS = 4
HPG = N_HEADS // KV_GROUPS
CMP_LEN = 32
CMP_STRIDE = 16
SEL_LEN = 64
SEL_TOP = 16
WINDOW = 512
SEL_FORCE = 1e4
KV_RANK = 256
IDX_HEADS = 8
IDX_DIM = 64
IDX_TOPK = 256
REL_BUCKETS = 32
REL_MAX_DIST = 128
N_EXPERTS = 8
DEPTH = 2
ALPHA = (2 * DEPTH) ** 0.25
LN_EPS = 1e-5
RMS_EPS = 1e-6
NEG_INF = -1e30

LOG2E = math.log2(math.e)
BIG = float(2.0 ** 100)
M_INIT = -3.0e38
TQ = 256
LANES = 128
VMEM_LIMIT_BYTES = 60000 * 1024


def _cp(sem, vmem=None):
    return pltpu.CompilerParams(dimension_semantics=sem, vmem_limit_bytes=vmem or VMEM_LIMIT_BYTES)


def _dot(a, b):
    return jnp.dot(a.astype(BF), b.astype(BF), preferred_element_type=F32)


def _dot_nt(a, b):
    return lax.dot_general(a.astype(BF), b.astype(BF), (((1,), (1,)), ((), ())),
                           preferred_element_type=F32)


def _layer_norm_rows(z, g, b):
    mu = jnp.mean(z, axis=-1, keepdims=True)
    zc = z - mu
    var = jnp.mean(zc * zc, axis=-1, keepdims=True)
    return zc * lax.rsqrt(var + LN_EPS) * g + b


def _mm_scale_kernel(a_ref, b_ref, s_ref, o_ref):
    o_ref[...] = (_dot(a_ref[...], b_ref[...]) * s_ref[...]).astype(o_ref.dtype)


def matmul_scaled(a, b, scale, out_dtype, tm, tn):
    m, k = a.shape
    n = b.shape[1]
    return pl.pallas_call(
        _mm_scale_kernel,
        grid=(m // tm, n // tn),
        in_specs=[pl.BlockSpec((tm, k), lambda i, j: (i, 0)),
                  pl.BlockSpec((k, tn), lambda i, j: (0, j)),
                  pl.BlockSpec((1, tn), lambda i, j: (0, j))],
        out_specs=pl.BlockSpec((tm, tn), lambda i, j: (i, j)),
        out_shape=jax.ShapeDtypeStruct((m, n), out_dtype),
        compiler_params=_cp(("parallel", "parallel")),
        name="matmul_scaled",
    )(a, b, scale.reshape(1, n).astype(F32))


def _mm_rms_kernel(a_ref, b_ref, g_ref, o_ref):
    c = _dot(a_ref[...], b_ref[...])
    r = lax.rsqrt(jnp.mean(c * c, axis=-1, keepdims=True) + RMS_EPS)
    o_ref[...] = (c * r * g_ref[...]).astype(o_ref.dtype)


def matmul_rmsnorm(a, b, g, tm):
    m, k = a.shape
    n = b.shape[1]
    return pl.pallas_call(
        _mm_rms_kernel,
        grid=(m // tm,),
        in_specs=[pl.BlockSpec((tm, k), lambda i: (i, 0)),
                  pl.BlockSpec((k, n), lambda i: (0, 0)),
                  pl.BlockSpec((1, n), lambda i: (0, 0))],
        out_specs=pl.BlockSpec((tm, n), lambda i: (i, 0)),
        out_shape=jax.ShapeDtypeStruct((m, n), BF),
        compiler_params=_cp(("parallel",)),
        name="matmul_rmsnorm",
    )(a, b, g.reshape(1, n).astype(F32))


def _mm_ln_kernel(a_ref, b_ref, r_ref, g_ref, be_ref, of_ref, ob_ref, acc_ref, *, nk):
    kk = pl.program_id(1)

    @pl.when(kk == 0)
    def _():
        acc_ref[...] = jnp.zeros_like(acc_ref)

    acc_ref[...] += _dot(a_ref[...], b_ref[...])

    @pl.when(kk == nk - 1)
    def _():
        y = _layer_norm_rows(ALPHA * r_ref[...] + acc_ref[...], g_ref[...], be_ref[...])
        of_ref[...] = y
        ob_ref[...] = y.astype(BF)


def matmul_residual_ln(a, b, res, g, beta, tm, tk):
    m, k = a.shape
    n = b.shape[1]
    nk = k // tk
    return pl.pallas_call(
        functools.partial(_mm_ln_kernel, nk=nk),
        grid=(m // tm, nk),
        in_specs=[pl.BlockSpec((tm, tk), lambda i, kk: (i, kk)),
                  pl.BlockSpec((tk, n), lambda i, kk: (kk, 0)),
                  pl.BlockSpec((tm, n), lambda i, kk: (i, 0)),
                  pl.BlockSpec((1, n), lambda i, kk: (0, 0)),
                  pl.BlockSpec((1, n), lambda i, kk: (0, 0))],
        out_specs=[pl.BlockSpec((tm, n), lambda i, kk: (i, 0)),
                   pl.BlockSpec((tm, n), lambda i, kk: (i, 0))],
        out_shape=[jax.ShapeDtypeStruct((m, n), F32), jax.ShapeDtypeStruct((m, n), BF)],
        scratch_shapes=[pltpu.VMEM((tm, n), F32)],
        compiler_params=_cp(("parallel", "arbitrary")),
        name="matmul_residual_ln",
    )(a, b, res, g.reshape(1, n).astype(F32), beta.reshape(1, n).astype(F32))


def _silu(x):
    return x * (1.0 / (1.0 + jnp.exp(-x)))


def _swiglu_up_kernel(a_ref, wg_ref, wu_ref, o_ref):
    a = a_ref[...]
    o_ref[...] = (_silu(_dot(a, wg_ref[...])) * _dot(a, wu_ref[...])).astype(o_ref.dtype)


def swiglu_up(a, wg, wu, tm, tn):
    m, k = a.shape
    n = wg.shape[1]
    return pl.pallas_call(
        _swiglu_up_kernel,
        grid=(m // tm, n // tn),
        in_specs=[pl.BlockSpec((tm, k), lambda i, j: (i, 0)),
                  pl.BlockSpec((k, tn), lambda i, j: (0, j)),
                  pl.BlockSpec((k, tn), lambda i, j: (0, j))],
        out_specs=pl.BlockSpec((tm, tn), lambda i, j: (i, j)),
        out_shape=jax.ShapeDtypeStruct((m, n), BF),
        compiler_params=_cp(("parallel", "parallel")),
        name="swiglu_up",
    )(a, wg, wu)


def _bucket_of_distance():
    n = np.arange(REL_MAX_DIST + 1)
    exact = REL_BUCKETS // 2
    nf = np.maximum(n, exact).astype(np.float64)
    large = exact + (np.log(nf / exact) / math.log(REL_MAX_DIST / exact) * (REL_BUCKETS - exact)).astype(np.int64)
    return np.where(n < exact, n, np.minimum(large, REL_BUCKETS - 1)).astype(np.int32)


def _bucket_starts():
    bk = _bucket_of_distance()
    return [int(np.argmax(bk >= b)) for b in range(REL_BUCKETS)]


def _bias_kernel(tab_ref, band_ref, cmp_ref, *, starts):
    h = pl.program_id(0)
    far = tab_ref[REL_BUCKETS - 1, h]

    def lookup(d):
        val = jnp.zeros(d.shape, F32)
        for b in range(REL_BUCKETS - 2, -1, -1):
            val = jnp.where(d < starts[b + 1], (tab_ref[b, h] - far) * LOG2E, val)
        return val

    j = lax.broadcasted_iota(I32, (TQ, TQ), 0)
    i = lax.broadcasted_iota(I32, (TQ, TQ), 1)
    for r in range(3):
        d = TQ * r + i - j
        band_ref[r] = jnp.where((d < 0) | (d >= WINDOW), -BIG, lookup(d))
    c = lax.broadcasted_iota(I32, (CMP_NEAR, TQ), 0)
    i = lax.broadcasted_iota(I32, (CMP_NEAR, TQ), 1)
    d = i - CMP_STRIDE * (c - CMP_NEAR // 2) - (CMP_LEN - 1)
    cmp_ref[...] = jnp.where(d < 0, -BIG, lookup(d))


CMP_NEAR = 2 * (TQ // CMP_STRIDE)


def _bias_tables(rel_bias):
    return pl.pallas_call(
        functools.partial(_bias_kernel, starts=_bucket_starts()),
        grid=(N_HEADS,),
        in_specs=[pl.BlockSpec(memory_space=pltpu.SMEM)],
        out_specs=[pl.BlockSpec((None, 3, TQ, TQ), lambda h: (h // HPG, 0, 0, h % HPG)),
                   pl.BlockSpec((None, CMP_NEAR, TQ), lambda h: (h // HPG, 0, h % HPG))],
        out_shape=[jax.ShapeDtypeStruct((N_HEADS // HPG, 3, TQ, HPG * TQ), F32),
                   jax.ShapeDtypeStruct((N_HEADS // HPG, CMP_NEAR, HPG * TQ), F32)],
        compiler_params=_cp(("parallel",)),
        name="bias_tables",
    )(rel_bias.astype(F32))


def _heads_to_rows(q):
    return jnp.concatenate([q[:, j * HEAD_DIM:(j + 1) * HEAD_DIM] for j in range(HPG)], axis=0)


def _causal_chunk_attention(qt, score_fn, extra_fn, vt_fn, band_ref, acc_ref, sbuf_ref, pbuf_ref):
    rows = HPG * TQ
    sa_ref, sb_ref = sbuf_ref.at[0], sbuf_ref.at[1]
    pa_ref, pb_ref = pbuf_ref.at[0], pbuf_ref.at[1]

    def softmax_stage(m, s_ref, p_ref, kc, bias=None):
        st = s_ref[...]
        extra = extra_fn(kc)
        if extra is not None:
            st = st + extra
        if bias is not None:
            st = st + bias
        m_new = jnp.maximum(m, jnp.max(st, axis=0, keepdims=True))
        p_ref[...] = jnp.exp2(st - m_new).astype(BF)
        return m_new, jnp.exp2(m - m_new)

    def value_stage(alpha, p_ref, kc):
        acc_ref[...] = alpha * acc_ref[...] + jnp.dot(vt_fn(kc), p_ref[...], preferred_element_type=F32)

    def far_pair(i, carry):
        m, alpha_b = carry
        a = 2 * i
        sb_ref[...] = score_fn(a + 1)
        value_stage(alpha_b, pb_ref, jnp.maximum(a - 1, 0))
        m, alpha_a = softmax_stage(m, sa_ref, pa_ref, a)
        sa_ref[...] = score_fn(a + 2)
        value_stage(alpha_a, pa_ref, a)
        m, alpha_b = softmax_stage(m, sb_ref, pb_ref, a + 1)
        return m, alpha_b

    acc_ref[...] = jnp.zeros_like(acc_ref)
    pb_ref[...] = jnp.zeros_like(pb_ref)
    sa_ref[...] = score_fn(0)
    npair = lax.shift_right_logical(jnp.maximum(qt - 1, 0), 1)
    m, alpha_b = lax.fori_loop(0, npair, far_pair,
                               (jnp.full((1, rows), M_INIT, F32), jnp.ones((1, rows), F32)))
    a = 2 * npair
    odd = (qt & 1) == 1
    first = qt == 0
    sb_ref[...] = score_fn(a + 1)
    value_stage(alpha_b, pb_ref, jnp.maximum(a - 1, 0))
    m, alpha_a = softmax_stage(m, sa_ref, pa_ref, a,
                               jnp.where(odd, band_ref[1], jnp.where(first, band_ref[0], 0.0)))
    value_stage(alpha_a, pa_ref, a)
    m, alpha_b = softmax_stage(m, sb_ref, pb_ref, a + 1,
                               jnp.where(odd, band_ref[0], jnp.where(first, -BIG, band_ref[1])))
    value_stage(alpha_b, pb_ref, a + 1)

    @pl.when((qt >= 2) & jnp.logical_not(odd))
    def _():
        sa_ref[...] = score_fn(qt)
        _, alpha_a = softmax_stage(m, sa_ref, pa_ref, qt, band_ref[0])
        value_stage(alpha_a, pa_ref, qt)


ONES_ROWS = 16


def _store_transposed(src_ref, dst_ref, nchunk):
    dv = src_ref.shape[1]
    tail = jnp.where(lax.broadcasted_iota(I32, (ONES_ROWS, TQ), 0) == 0, 1.0, 0.0).astype(dst_ref.dtype)

    def body(kc, _):
        start = pl.multiple_of(kc * TQ, TQ)
        dst_ref[kc, 0:dv, :] = src_ref[pl.ds(start, TQ), :].astype(F32).T.astype(dst_ref.dtype)
        dst_ref[kc, dv:dv + ONES_ROWS, :] = tail
        return 0

    lax.fori_loop(0, nchunk, body, 0)


def _gelu_tanh(x):
    return 0.5 * x * (1.0 + jnp.tanh(math.sqrt(2.0 / math.pi) * (x + 0.044715 * (x * x * x))))


def _compress_kernel(*refs, nch):
    x_refs = refs[:CMP_STRIDE]
    pe_ref, w1_ref, w2_ref, o_ref = refs[CMP_STRIDE:]
    acc_a = [jnp.zeros((nch, HEAD_DIM), F32) for _ in range(KV_GROUPS)]
    acc_b = [jnp.zeros((nch, HEAD_DIM), F32) for _ in range(KV_GROUPS)]
    for l in range(CMP_STRIDE):
        x = x_refs[l][...].astype(F32)
        xa = (x + pe_ref[l:l + 1, :]).astype(BF)
        xb = (x + pe_ref[CMP_STRIDE + l:CMP_STRIDE + l + 1, :]).astype(BF)
        for g in range(KV_GROUPS):
            sl = slice(g * HEAD_DIM, (g + 1) * HEAD_DIM)
            acc_a[g] = acc_a[g] + jnp.dot(xa[:, sl], w1_ref[l], preferred_element_type=F32)
            acc_b[g] = acc_b[g] + jnp.dot(xb[:, sl], w1_ref[CMP_STRIDE + l], preferred_element_type=F32)
    for g in range(KV_GROUPS):
        pre = acc_a[g] + pltpu.roll(acc_b[g], nch - 1, axis=0)
        hid = _gelu_tanh(pre).astype(BF)
        o_ref[:, g * HEAD_DIM:(g + 1) * HEAD_DIM] = jnp.dot(hid, w2_ref[...], preferred_element_type=F32).astype(BF)


def nsa_compress(ybf, b, s, col_block, pe, w1, w2):
    ncols = ybf.shape[1]
    nch = s // CMP_STRIDE
    blk_w = KV_GROUPS * HEAD_DIM
    per_tok = ncols // blk_w
    y3 = ybf.reshape(b, nch, CMP_STRIDE * ncols)
    pe_t = jnp.tile(pe.astype(F32), (1, KV_GROUPS))
    in_specs = [pl.BlockSpec((None, nch, blk_w), (lambda bi, l=l: (bi, 0, l * per_tok + col_block)))
                for l in range(CMP_STRIDE)]
    in_specs += [pl.BlockSpec((CMP_LEN, blk_w), lambda bi: (0, 0)),
                 pl.BlockSpec((CMP_LEN, HEAD_DIM, HEAD_DIM), lambda bi: (0, 0, 0)),
                 pl.BlockSpec((HEAD_DIM, HEAD_DIM), lambda bi: (0, 0))]
    return pl.pallas_call(
        functools.partial(_compress_kernel, nch=nch),
        grid=(b,),
        in_specs=in_specs,
        out_specs=pl.BlockSpec((None, nch, blk_w), lambda bi: (bi, 0, 0)),
        out_shape=jax.ShapeDtypeStruct((b, nch, blk_w), BF),
        compiler_params=_cp(("parallel",)),
        name="nsa_compress",
    )(*([y3] * CMP_STRIDE), pe_t, w1.astype(BF), w2.astype(BF))


def _nsa_cmp_kernel(q_ref, kc_ref, vc_ref, cb_ref, ovt_ref, oc_ref, ns_ref, vct_ref, s_ref, *, n_sel, n_top):
    qt = pl.program_id(2)
    t0 = qt * TQ
    ncp = kc_ref.shape[0]
    pad = CMP_NEAR // 2

    @pl.when(qt == 0)
    def _():
        vct_ref[...] = vc_ref[...].astype(F32).T.astype(BF)
        s_ref[0:pad, :] = jnp.zeros((pad, HPG * TQ), F32)

    qs = _heads_to_rows(q_ref[...])
    s_ref[pad:pad + ncp, :] = _dot_nt(kc_ref[...], qs)
    near = pl.multiple_of(qt * (TQ // CMP_STRIDE), TQ // CMP_STRIDE)
    s_ref[pl.ds(near, CMP_NEAR), :] = s_ref[pl.ds(near, CMP_NEAR), :] + cb_ref[...]
    s = s_ref[pad:pad + ncp, :]
    key = lax.broadcasted_iota(I32, (ncp, HPG * TQ), 0)
    tq = t0 + (lax.broadcasted_iota(I32, (ncp, HPG * TQ), 1) & (TQ - 1))
    vis = (key * CMP_STRIDE + (CMP_LEN - 1)) <= tq
    s = jnp.where(vis, s, -BIG)
    m = jnp.max(s, axis=0, keepdims=True)
    p = jnp.where(vis, jnp.exp2(s - m), 0.0)
    l = jnp.sum(p, axis=0, keepdims=True)
    p = p * jnp.where(l > 0.0, 1.0 / l, 0.0)
    pb = p.astype(BF)
    oct = jnp.dot(vct_ref[...], pb, preferred_element_type=F32)
    score = jnp.zeros((LANES, TQ), F32)
    for j in range(HPG):
        cs = slice(j * TQ, (j + 1) * TQ)
        oc_ref[:, j * HEAD_DIM:(j + 1) * HEAD_DIM] = oct[:, cs].T.astype(oc_ref.dtype)
        score = score + jnp.dot(ovt_ref[...], pb[:, cs], preferred_element_type=F32)
    blk = lax.broadcasted_iota(I32, (LANES, TQ), 0)
    t = t0 + lax.broadcasted_iota(I32, (LANES, TQ), 1)
    cur = lax.shift_right_logical(t, int(math.log2(SEL_LEN)))
    forced = (blk == 0) | (blk == cur) | (blk == cur - 1)
    visible = blk * SEL_LEN <= t
    sc = jnp.where(forced, SEL_FORCE, jnp.where(visible, score, -1.0))
    sc = jnp.where(blk < n_sel, sc, -2.0)
    blkf = blk.astype(F32)

    def pick_one(_, carry):
        sc, sel = carry
        mx = jnp.max(sc, axis=0, keepdims=True)
        first = jnp.min(jnp.where(sc == mx, blkf, float(LANES)), axis=0, keepdims=True)
        hit = blkf == first
        return jnp.where(hit, -3.0, sc), jnp.where(hit, 1.0, sel)

    _, sel = lax.fori_loop(0, n_top, pick_one, (sc, jnp.zeros((LANES, TQ), F32)))
    ns_ref[...] = jnp.where(sel.T > 0.5, 0.0, BIG).astype(BF)


def nsa_cmp_select(ybf, kcmp, vcmp, cmpb, overlap_t, b, s):
    ncp = kcmp.shape[1]
    n_sel = s // SEL_LEN
    n_top = min(SEL_TOP, n_sel)
    y3 = ybf.reshape(b, s, ybf.shape[1])
    gw = HPG * HEAD_DIM
    return pl.pallas_call(
        functools.partial(_nsa_cmp_kernel, n_sel=n_sel, n_top=n_top),
        grid=(b, KV_GROUPS, s // TQ),
        in_specs=[pl.BlockSpec((None, TQ, gw), lambda bi, g, qt: (bi, qt, g)),
                  pl.BlockSpec((None, ncp, HEAD_DIM), lambda bi, g, qt: (bi, 0, g)),
                  pl.BlockSpec((None, ncp, HEAD_DIM), lambda bi, g, qt: (bi, 0, g)),
                  pl.BlockSpec((None, CMP_NEAR, HPG * TQ), lambda bi, g, qt: (g, 0, 0)),
                  pl.BlockSpec((LANES, ncp), lambda bi, g, qt: (0, 0))],
        out_specs=[pl.BlockSpec((None, TQ, gw), lambda bi, g, qt: (bi, qt, g)),
                   pl.BlockSpec((None, None, TQ, LANES), lambda bi, g, qt: (bi, g, qt, 0))],
        out_shape=[jax.ShapeDtypeStruct((b, s, KV_GROUPS * gw), BF),
                   jax.ShapeDtypeStruct((b, KV_GROUPS, s, LANES), BF)],
        scratch_shapes=[pltpu.VMEM((HEAD_DIM, ncp), BF),
                        pltpu.VMEM((ncp + CMP_NEAR, HPG * TQ), F32)],
        compiler_params=_cp(("parallel", "parallel", "arbitrary")),
        name="nsa_cmp_select",
    )(y3, kcmp, vcmp, cmpb, overlap_t)


def _nsa_main_kernel(q_ref, ks_ref, vs_ref, kw_ref, vw_ref, ns_ref, band_ref, gl_ref, oc_ref, o_ref,
                     vst_ref, vwt_ref, accs_ref, accw_ref, sbuf_ref, pbuf_ref, *, nchunk):
    qt = pl.program_id(2)

    @pl.when(qt == 0)
    def _():
        _store_transposed(vs_ref, vst_ref, nchunk)
        _store_transposed(vw_ref, vwt_ref, nchunk)

    q = q_ref[...]
    ns = ns_ref[...]
    qs = _heads_to_rows(q)
    qp = jnp.concatenate([qs, jnp.concatenate([ns] * HPG, axis=0)], axis=1)

    def sel_scores(kc):
        start = pl.multiple_of(kc * TQ, TQ)
        k = ks_ref[pl.ds(start, TQ), :]
        krow = lax.broadcasted_iota(I32, (TQ, LANES), 0)
        klane = lax.broadcasted_iota(I32, (TQ, LANES), 1)
        kblk = lax.shift_right_logical(krow, int(math.log2(SEL_LEN))) + kc * (TQ // SEL_LEN)
        oh = jnp.where(klane == kblk, -1.0, 0.0).astype(BF)
        return _dot_nt(jnp.concatenate([k, oh], axis=1), qp)

    _causal_chunk_attention(qt, sel_scores, lambda kc: None, lambda kc: vst_ref[kc],
                            band_ref, accs_ref, sbuf_ref, pbuf_ref)

    accw_ref[...] = jnp.zeros_like(accw_ref)
    m_w = [jnp.full((1, TQ), M_INIT, F32)] * HPG
    for r in (2, 1, 0):
        kc = jnp.maximum(qt - r, 0)
        start = pl.multiple_of(kc * TQ, TQ)
        st_all = _dot_nt(kw_ref[pl.ds(start, TQ), :], qs) + jnp.where(qt >= r, band_ref[r], -BIG)
        for j in range(HPG):
            cs = slice(j * TQ, (j + 1) * TQ)
            st = st_all[:, cs]
            m_new = jnp.maximum(m_w[j], jnp.max(st, axis=0, keepdims=True))
            alpha = jnp.exp2(m_w[j] - m_new)
            p = jnp.exp2(st - m_new).astype(BF)
            m_w[j] = m_new
            accw_ref[:, cs] = alpha * accw_ref[:, cs] + jnp.dot(vwt_ref[kc], p, preferred_element_type=F32)

    gates = 1.0 / (1.0 + jnp.exp(-gl_ref[...]))
    oc = oc_ref[...].astype(F32)
    for j in range(HPG):
        sl = slice(j * HEAD_DIM, (j + 1) * HEAD_DIM)
        cs = slice(j * TQ, (j + 1) * TQ)
        o_s = (accs_ref[0:HEAD_DIM, cs] * (1.0 / accs_ref[HEAD_DIM:HEAD_DIM + 1, cs])).T
        o_w = (accw_ref[0:HEAD_DIM, cs] * (1.0 / accw_ref[HEAD_DIM:HEAD_DIM + 1, cs])).T
        o = (gates[:, j:j + 1] * oc[:, sl] + gates[:, HPG + j:HPG + j + 1] * o_s
             + gates[:, 2 * HPG + j:2 * HPG + j + 1] * o_w)
        o_ref[:, sl] = o.astype(o_ref.dtype)


def nsa_main(ybf, nsel, band, gl, oc, b, s, col_ksl, col_vsl, col_kw, col_vw):
    y3 = ybf.reshape(b, s, ybf.shape[1])
    gw = HPG * HEAD_DIM
    kv_spec = lambda cb: pl.BlockSpec((None, s, HEAD_DIM), lambda bi, g, qt: (bi, 0, cb + g))
    nchunk = s // TQ
    return pl.pallas_call(
        functools.partial(_nsa_main_kernel, nchunk=nchunk),
        grid=(b, KV_GROUPS, nchunk),
        in_specs=[pl.BlockSpec((None, TQ, gw), lambda bi, g, qt: (bi, qt, g)),
                  kv_spec(col_ksl), kv_spec(col_vsl), kv_spec(col_kw), kv_spec(col_vw),
                  pl.BlockSpec((None, None, TQ, LANES), lambda bi, g, qt: (bi, g, qt, 0)),
                  pl.BlockSpec((None, 3, TQ, HPG * TQ), lambda bi, g, qt: (g, 0, 0, 0)),
                  pl.BlockSpec((None, TQ, LANES), lambda bi, g, qt: (bi, qt, g)),
                  pl.BlockSpec((None, TQ, gw), lambda bi, g, qt: (bi, qt, g))],
        out_specs=pl.BlockSpec((None, TQ, gw), lambda bi, g, qt: (bi, qt, g)),
        out_shape=jax.ShapeDtypeStruct((b, s, KV_GROUPS * gw), BF),
        scratch_shapes=[pltpu.VMEM((nchunk, HEAD_DIM + ONES_ROWS, TQ), BF),
                        pltpu.VMEM((nchunk, HEAD_DIM + ONES_ROWS, TQ), BF),
                        pltpu.VMEM((HEAD_DIM + ONES_ROWS, HPG * TQ), F32),
                        pltpu.VMEM((HEAD_DIM + ONES_ROWS, HPG * TQ), F32),
                        pltpu.VMEM((2, TQ, HPG * TQ), F32), pltpu.VMEM((2, TQ, HPG * TQ), BF)],
        compiler_params=_cp(("parallel", "parallel", "arbitrary")),
        name="nsa_main",
    )(y3, y3, y3, y3, y3, nsel, band, gl.reshape(b, s, gl.shape[1]), oc)


def _dsa_index_kernel(qi_ref, kw_ref, wq_ref, nm_ref, ka_ref, kb_ref, key_ref, d_ref, *, k_sel, nchunk,
                      idx_bits):
    qt = pl.program_id(1)
    t0 = qt * TQ
    half = LANES // 2

    @pl.when(qt == 0)
    def _():
        lane = lax.broadcasted_iota(I32, kw_ref.shape, 1)
        ka = jnp.where(lane < IDX_DIM, kw_ref[...], 0.0)
        ka_ref[...] = ka.astype(BF)
        kb_ref[...] = pltpu.roll(ka, half, axis=1).astype(BF)

    qi = qi_ref[...]
    npair = IDX_HEADS // 2
    lq = jnp.concatenate([qi[:, p * LANES:(p + 1) * LANES] for p in range(npair)], axis=0)
    wt = wq_ref[...].T
    w_r = [wt[IDX_DIM + h:IDX_DIM + h + 1, :] for h in range(IDX_HEADS)]
    kpos = lax.broadcasted_iota(I32, (TQ, TQ), 0)
    qpos = t0 + lax.broadcasted_iota(I32, (TQ, TQ), 1)

    def head_dots(kc, buf):
        start = pl.multiple_of(jnp.minimum(kc, nchunk - 1) * TQ, TQ)
        d_ref[buf, 0] = _dot_nt(ka_ref[pl.ds(start, TQ), :], lq)
        d_ref[buf, 1] = _dot_nt(kb_ref[pl.ds(start, TQ), :], lq)

    def to_keys(kc, buf):
        sc = jnp.zeros((TQ, TQ), F32)
        for p in range(npair):
            cs = slice(p * TQ, (p + 1) * TQ)
            sc = (sc + w_r[2 * p] * jnp.maximum(d_ref[buf, 0, :, cs], 0.0)
                  + w_r[2 * p + 1] * jnp.maximum(d_ref[buf, 1, :, cs], 0.0))
        sc = jnp.where(sc == 0.0, 0.0, sc)
        sc = jnp.where(kc * TQ + kpos <= qpos, sc, NEG_INF)
        bits = lax.bitcast_convert_type(sc, I32)
        key_ref[kc] = jnp.where(bits < 0, bits ^ 0x7FFFFFFF, bits)

    def score_pair(i, _):
        a = 2 * i
        head_dots(a + 1, 1)
        to_keys(a, 0)
        head_dots(a + 2, 0)
        to_keys(a + 1, 1)
        return 0

    nproc = qt + 1
    head_dots(0, 0)
    lax.fori_loop(0, lax.shift_right_logical(nproc, 1), score_pair, 0)

    @pl.when((nproc & 1) == 1)
    def _():
        to_keys(nproc - 1, 0)

    def count(pred):
        def one(kc):
            return jnp.sum(jnp.where(pred(key_ref[kc], kc), 1.0, 0.0), axis=0, keepdims=True)

        def body(i, acc):
            return acc + (one(2 * i) + one(2 * i + 1))

        acc = lax.fori_loop(0, lax.shift_right_logical(nproc, 1), body, jnp.zeros((1, TQ), F32))
        return acc + jnp.where((nproc & 1) == 1, one(nproc - 1), 0.0)

    def count_ge(cand):
        return count(lambda k, kc: k >= cand)

    kf = float(k_sel)
    thr0 = jnp.full((1, TQ), -2 ** 31, I32)
    done0 = jnp.where(count_ge(thr0) == kf, 1.0, 0.0)

    def search_cond(state):
        i, _, done = state
        return (i < 32) & (jnp.min(done) < 0.5)

    def search_body(state):
        i, thr, done = state
        cand = thr ^ lax.shift_left(jnp.int32(1), jnp.int32(31) - i)
        cnt = count_ge(cand)
        take = (cnt >= kf) & (done < 0.5)
        thr = jnp.where(take, cand, thr)
        done = jnp.where(take & (cnt == kf), 1.0, done)
        return i + 1, thr, done

    _, thr, done = lax.while_loop(search_cond, search_body, (jnp.int32(0), thr0, done0))

    def tie_break(_):
        need = kf - count_ge(thr + 1)

        def ties_below(bound):
            return count(lambda k, kc: (k == thr) & (kc * TQ + kpos < bound))

        def idx_bit(i, jm):
            cand = jm | lax.shift_left(jnp.int32(1), jnp.int32(idx_bits - 1) - i)
            return jnp.where(ties_below(cand) < need, cand, jm)

        jm = lax.fori_loop(0, idx_bits, idx_bit, jnp.zeros((1, TQ), I32))
        return jnp.where(done > 0.5, jnp.int32(2 ** 30), jm)

    jm = lax.cond(jnp.min(done) < 0.5, tie_break, lambda _: jnp.full((1, TQ), 2 ** 30, I32), 0)

    def write_chunk(kc, _):
        k = key_ref[kc]
        kidx = kc * TQ + kpos
        sel = (kidx <= qpos) & ((k > thr) | ((k == thr) & (kidx <= jm)))
        nm_ref[kc] = jnp.where(sel, 0.0, -BIG).astype(BF)
        return 0

    lax.fori_loop(0, nproc, write_chunk, 0)

    def fill_chunk(kc, _):
        nm_ref[kc] = jnp.full((TQ, TQ), -BIG, BF)
        return 0

    lax.fori_loop(nproc, nchunk, fill_chunk, 0)


def dsa_index(qbf, kw, b, s, col_qi):
    nchunk = s // TQ
    k_sel = min(IDX_TOPK, s // 4)
    q3 = qbf.reshape(b, s, qbf.shape[1])
    kw3 = kw.reshape(b, s, LANES)
    return pl.pallas_call(
        functools.partial(_dsa_index_kernel, k_sel=k_sel, nchunk=nchunk, idx_bits=int(math.log2(s))),
        grid=(b, nchunk),
        in_specs=[pl.BlockSpec((None, TQ, IDX_HEADS * IDX_DIM), lambda bi, qt: (bi, qt, col_qi)),
                  pl.BlockSpec((None, s, LANES), lambda bi, qt: (bi, 0, 0)),
                  pl.BlockSpec((None, TQ, LANES), lambda bi, qt: (bi, qt, 0))],
        out_specs=pl.BlockSpec((None, None, nchunk, TQ, TQ), lambda bi, qt: (bi, qt, 0, 0, 0)),
        out_shape=jax.ShapeDtypeStruct((b, nchunk, nchunk, TQ, TQ), BF),
        scratch_shapes=[pltpu.VMEM((s, LANES), BF), pltpu.VMEM((s, LANES), BF),
                        pltpu.VMEM((nchunk, TQ, TQ), I32),
                        pltpu.VMEM((2, 2, TQ, (IDX_HEADS // 2) * TQ), F32)],
        compiler_params=_cp(("parallel", "arbitrary")),
        name="dsa_index",
    )(q3, kw3, kw3)


def _dsa_attn_kernel(q_ref, c_ref, nm_ref, band_ref, wuk_ref, wuvt_ref, o_ref, ct_ref, acc_ref, sbuf_ref,
                     pbuf_ref, *, nchunk):
    qt = pl.program_id(1)
    hg = pl.program_id(2)

    @pl.when((qt == 0) & (hg == 0))
    def _():
        _store_transposed(c_ref, ct_ref, nchunk)

    q = q_ref[...]
    ql = jnp.concatenate(
        [(jnp.dot(q[:, j * HEAD_DIM:(j + 1) * HEAD_DIM], wuk_ref[j], preferred_element_type=F32)
          * (HEAD_DIM ** -0.5 * LOG2E)).astype(BF) for j in range(HPG)], axis=0)

    def scores(kc):
        start = pl.multiple_of(kc * TQ, TQ)
        return _dot_nt(c_ref[pl.ds(start, TQ), :], ql)

    def member_mask(kc):
        nm = nm_ref[kc].astype(F32)
        return jnp.concatenate([nm] * HPG, axis=1)

    _causal_chunk_attention(qt, scores, member_mask, lambda kc: ct_ref[kc], band_ref, acc_ref,
                            sbuf_ref, pbuf_ref)
    o_lat = (acc_ref[0:KV_RANK, :] * (1.0 / acc_ref[KV_RANK:KV_RANK + 1, :])).astype(BF)
    for j in range(HPG):
        ot = jnp.dot(wuvt_ref[j], o_lat[:, j * TQ:(j + 1) * TQ], preferred_element_type=F32)
        o_ref[:, j * HEAD_DIM:(j + 1) * HEAD_DIM] = ot.T.astype(o_ref.dtype)


def dsa_attention(qbf, cn, nmask, band, w_uk, w_uv, b, s):
    nchunk = s // TQ
    q3 = qbf.reshape(b, s, qbf.shape[1])
    c3 = cn.reshape(b, s, KV_RANK)
    gw = HPG * HEAD_DIM
    return pl.pallas_call(
        functools.partial(_dsa_attn_kernel, nchunk=nchunk),
        grid=(b, nchunk, N_HEADS // HPG),
        in_specs=[pl.BlockSpec((None, TQ, gw), lambda bi, qt, hg: (bi, qt, hg)),
                  pl.BlockSpec((None, s, KV_RANK), lambda bi, qt, hg: (bi, 0, 0)),
                  pl.BlockSpec((None, None, nchunk, TQ, TQ), lambda bi, qt, hg: (bi, qt, 0, 0, 0)),
                  pl.BlockSpec((None, 2, TQ, HPG * TQ), lambda bi, qt, hg: (hg, 0, 0, 0)),
                  pl.BlockSpec((HPG, HEAD_DIM, KV_RANK), lambda bi, qt, hg: (hg, 0, 0)),
                  pl.BlockSpec((HPG, HEAD_DIM, KV_RANK), lambda bi, qt, hg: (hg, 0, 0))],
        out_specs=pl.BlockSpec((None, TQ, gw), lambda bi, qt, hg: (bi, qt, hg)),
        out_shape=jax.ShapeDtypeStruct((b, s, N_HEADS * HEAD_DIM), BF),
        scratch_shapes=[pltpu.VMEM((nchunk, KV_RANK + ONES_ROWS, TQ), BF),
                        pltpu.VMEM((KV_RANK + ONES_ROWS, HPG * TQ), F32),
                        pltpu.VMEM((2, TQ, HPG * TQ), F32), pltpu.VMEM((2, TQ, HPG * TQ), BF)],
        compiler_params=_cp(("parallel", "arbitrary", "arbitrary")),
        name="dsa_attention",
    )(q3, c3, nmask, band, w_uk.astype(BF), jnp.swapaxes(w_uv, 1, 2).astype(BF))


ROUTER_TM = 256
MOE_TM = 512


def _router_kernel(x_ref, w_ref, b_ref, o_ref, cnt_ref, carry_ref):
    i = pl.program_id(0)

    @pl.when(i == 0)
    def _():
        carry_ref[...] = jnp.zeros_like(carry_ref)

    tm = x_ref.shape[0]
    logits = jnp.dot(x_ref[...], w_ref[...], preferred_element_type=F32,
                     precision=lax.Precision.HIGHEST) + b_ref[...]
    lane = lax.broadcasted_iota(I32, (tm, LANES), 1)
    lanef = lane.astype(F32)
    lg = jnp.where(lane < N_EXPERTS, logits, -BIG)
    m1 = jnp.max(lg, axis=-1, keepdims=True)
    i1 = jnp.min(jnp.where(lg == m1, lanef, float(LANES)), axis=-1, keepdims=True)
    lg2 = jnp.where(lanef == i1, -BIG, lg)
    m2 = jnp.max(lg2, axis=-1, keepdims=True)
    i2 = jnp.min(jnp.where(lg2 == m2, lanef, float(LANES)), axis=-1, keepdims=True)
    e2 = jnp.exp(m2 - m1)
    den = 1.0 + e2
    w1 = 1.0 / den
    w2 = e2 / den
    hit1 = lanef == i1
    hit2 = lanef == i2
    onehot = jnp.where(hit1 | hit2, 1.0, 0.0)
    r = lax.broadcasted_iota(I32, (tm, tm), 0)
    c = lax.broadcasted_iota(I32, (tm, tm), 1)
    tri = jnp.where(c < r, 1.0, 0.0).astype(BF)
    before = jnp.dot(tri, onehot.astype(BF), preferred_element_type=F32) + carry_ref[...]
    rank1 = jnp.sum(jnp.where(hit1, before, 0.0), axis=-1, keepdims=True)
    rank2 = jnp.sum(jnp.where(hit2, before, 0.0), axis=-1, keepdims=True)
    carry_ref[...] = carry_ref[...] + jnp.sum(onehot, axis=0, keepdims=True)
    vals = (i1, i2, w1, w2, rank1, rank2)
    out = jnp.zeros((tm, LANES), F32)
    for k, v in enumerate(vals):
        out = jnp.where(lane == k, v, out)
    o_ref[...] = out
    cnt_ref[...] = jnp.broadcast_to(carry_ref[...], cnt_ref.shape)


def moe_router(x, w_router, b_router):
    n, d = x.shape
    wp = jnp.pad(w_router.astype(F32), ((0, 0), (0, LANES - N_EXPERTS)))
    bp = jnp.pad(b_router.astype(F32), (0, LANES - N_EXPERTS)).reshape(1, LANES)
    return pl.pallas_call(
        _router_kernel,
        grid=(n // ROUTER_TM,),
        in_specs=[pl.BlockSpec((ROUTER_TM, d), lambda i: (i, 0)),
                  pl.BlockSpec((d, LANES), lambda i: (0, 0)),
                  pl.BlockSpec((1, LANES), lambda i: (0, 0))],
        out_specs=[pl.BlockSpec((ROUTER_TM, LANES), lambda i: (i, 0)),
                   pl.BlockSpec((8, LANES), lambda i: (0, 0))],
        out_shape=[jax.ShapeDtypeStruct((n, LANES), F32), jax.ShapeDtypeStruct((8, LANES), F32)],
        scratch_shapes=[pltpu.VMEM((1, LANES), F32)],
        compiler_params=_cp(("arbitrary",)),
        name="moe_router",
    )(x, wp, bp)


GATHER_ROWS = 1024


def _row_slabs(x):
    return x.reshape(x.shape[0], x.shape[1] // LANES, LANES)


def _row_gather_kernel(idx_ref, x_hbm, o_ref, sem):
    def start(h, _):
        for pr in range(2):
            r = 2 * h + pr
            pltpu.make_async_copy(x_hbm.at[idx_ref[0, 0, r]], o_ref.at[r], sem).start(priority=pr)
        return 0

    lax.fori_loop(0, GATHER_ROWS // 2, start, 0)
    pltpu.make_async_copy(x_hbm.at[pl.ds(0, GATHER_ROWS)], o_ref, sem).wait()


def row_gather(x, row_idx):
    n_rows = row_idx.shape[0]
    nt = n_rows // GATHER_ROWS
    x3 = _row_slabs(x)
    out = pl.pallas_call(
        _row_gather_kernel,
        grid=(nt,),
        in_specs=[pl.BlockSpec((1, 1, GATHER_ROWS), lambda i: (i, 0, 0), memory_space=pltpu.SMEM),
                  pl.BlockSpec(memory_space=pl.ANY)],
        out_specs=pl.BlockSpec((GATHER_ROWS,) + x3.shape[1:], lambda i: (i, 0, 0)),
        out_shape=jax.ShapeDtypeStruct((n_rows,) + x3.shape[1:], x.dtype),
        scratch_shapes=[pltpu.SemaphoreType.DMA],
        compiler_params=_cp(("arbitrary",)),
        name="moe_row_gather",
    )(row_idx.reshape(nt, 1, GATHER_ROWS), x3)
    return out.reshape(n_rows, x.shape[1])


def _moe_up_kernel(te_ref, tv_ref, x_ref, wg_ref, wu_ref, o_ref):
    i = pl.program_id(1)

    @pl.when(tv_ref[i] > 0)
    def _():
        x = x_ref[...]
        o_ref[...] = (_silu(_dot(x, wg_ref[...])) * _dot(x, wu_ref[...])).astype(o_ref.dtype)

    @pl.when(tv_ref[i] == 0)
    def _():
        o_ref[...] = jnp.zeros_like(o_ref)


def moe_up(xs, w_gate, w_up, tile_e, tile_v, tn):
    n_rows, d = xs.shape
    f = w_gate.shape[2]
    nt = n_rows // MOE_TM
    grid_spec = pltpu.PrefetchScalarGridSpec(
        num_scalar_prefetch=2,
        grid=(f // tn, nt),
        in_specs=[pl.BlockSpec((MOE_TM, d), lambda j, i, te, tv: (i, 0)),
                  pl.BlockSpec((None, d, tn), lambda j, i, te, tv: (te[i], 0, j)),
                  pl.BlockSpec((None, d, tn), lambda j, i, te, tv: (te[i], 0, j))],
        out_specs=pl.BlockSpec((MOE_TM, tn), lambda j, i, te, tv: (i, j)),
    )
    return pl.pallas_call(
        _moe_up_kernel,
        grid_spec=grid_spec,
        out_shape=jax.ShapeDtypeStruct((n_rows, f), BF),
        compiler_params=_cp(("parallel", "arbitrary")),
        name="moe_up",
    )(tile_e, tile_v, xs, w_gate, w_up)


def _moe_down_kernel(te_ref, tv_ref, h_ref, wd_ref, o_ref):
    i = pl.program_id(1)

    @pl.when(tv_ref[i] > 0)
    def _():
        o_ref[...] = _dot(h_ref[...], wd_ref[...])

    @pl.when(tv_ref[i] == 0)
    def _():
        o_ref[...] = jnp.zeros_like(o_ref)


def moe_down(h, w_down, tile_e, tile_v, tn):
    n_rows, f = h.shape
    d = w_down.shape[2]
    nt = n_rows // MOE_TM
    grid_spec = pltpu.PrefetchScalarGridSpec(
        num_scalar_prefetch=2,
        grid=(d // tn, nt),
        in_specs=[pl.BlockSpec((MOE_TM, f), lambda j, i, te, tv: (i, 0)),
                  pl.BlockSpec((None, f, tn), lambda j, i, te, tv: (te[i], 0, j))],
        out_specs=pl.BlockSpec((MOE_TM, tn), lambda j, i, te, tv: (i, j)),
    )
    return pl.pallas_call(
        _moe_down_kernel,
        grid_spec=grid_spec,
        out_shape=jax.ShapeDtypeStruct((n_rows, d), F32),
        compiler_params=_cp(("parallel", "arbitrary")),
        name="moe_down",
    )(tile_e, tile_v, h, w_down)


COMBINE_TM = 256


def _moe_combine_kernel(d1_ref, d2_ref, n1_ref, n2_ref, y_hbm, x_ref, rw_ref, g_ref, b_ref, o_ref, buf, sem,
                        *, nt):
    i = pl.program_id(0)
    slot = i & 1
    nslab = buf.shape[3]

    def fetch(i1_ref, i2_ref, s):
        def start(r, _):
            pltpu.make_async_copy(y_hbm.at[i1_ref[0, 0, r]], buf.at[s, 0, r], sem.at[s]).start(priority=0)
            pltpu.make_async_copy(y_hbm.at[i2_ref[0, 0, r]], buf.at[s, 1, r], sem.at[s]).start(priority=1)
            return 0

        lax.fori_loop(0, COMBINE_TM, start, 0)

    @pl.when(i == 0)
    def _():
        fetch(d1_ref, d2_ref, 0)

    @pl.when(i + 1 < nt)
    def _():
        fetch(n1_ref, n2_ref, 1 - slot)

    for k in range(2):
        pltpu.make_async_copy(y_hbm.at[pl.ds(0, COMBINE_TM)], buf.at[slot, k], sem.at[slot]).wait()
    rw = rw_ref[...]
    w1 = rw[:, 2:3]
    w2 = rw[:, 3:4]
    z = []
    tot = jnp.zeros((COMBINE_TM, 1), F32)
    for c in range(nslab):
        sl = slice(c * LANES, (c + 1) * LANES)
        zc = ALPHA * x_ref[:, sl] + (w1 * buf[slot, 0, :, c, :] + w2 * buf[slot, 1, :, c, :])
        z.append(zc)
        tot = tot + jnp.sum(zc, axis=-1, keepdims=True)
    d = nslab * LANES
    mu = tot * (1.0 / d)
    ss = jnp.zeros((COMBINE_TM, 1), F32)
    for c in range(nslab):
        z[c] = z[c] - mu
        ss = ss + jnp.sum(z[c] * z[c], axis=-1, keepdims=True)
    r = lax.rsqrt(ss * (1.0 / d) + LN_EPS)
    for c in range(nslab):
        sl = slice(c * LANES, (c + 1) * LANES)
        o_ref[:, sl] = z[c] * r * g_ref[:, sl] + b_ref[:, sl]


def moe_combine(y_rows, x, route, dest1, dest2, g, beta):
    n, d = x.shape
    nt = n // COMBINE_TM
    y3 = _row_slabs(y_rows)
    idx_spec = pl.BlockSpec((1, 1, COMBINE_TM), lambda i: (i, 0, 0), memory_space=pltpu.SMEM)
    nxt_spec = pl.BlockSpec((1, 1, COMBINE_TM), lambda i: (jnp.minimum(i + 1, nt - 1), 0, 0),
                            memory_space=pltpu.SMEM)
    d1 = dest1.reshape(nt, 1, COMBINE_TM)
    d2 = dest2.reshape(nt, 1, COMBINE_TM)
    return pl.pallas_call(
        functools.partial(_moe_combine_kernel, nt=nt),
        grid=(nt,),
        in_specs=[idx_spec, idx_spec, nxt_spec, nxt_spec,
                  pl.BlockSpec(memory_space=pl.ANY),
                  pl.BlockSpec((COMBINE_TM, d), lambda i: (i, 0)),
                  pl.BlockSpec((COMBINE_TM, LANES), lambda i: (i, 0)),
                  pl.BlockSpec((1, d), lambda i: (0, 0)),
                  pl.BlockSpec((1, d), lambda i: (0, 0))],
        out_specs=pl.BlockSpec((COMBINE_TM, d), lambda i: (i, 0)),
        out_shape=jax.ShapeDtypeStruct((n, d), F32),
        scratch_shapes=[pltpu.VMEM((2, 2, COMBINE_TM) + y3.shape[1:], F32), pltpu.SemaphoreType.DMA((2,))],
        compiler_params=_cp(("arbitrary",)),
        name="moe_combine",
    )(d1, d2, d1, d2, y3, x, route, g.reshape(1, d).astype(F32), beta.reshape(1, d).astype(F32))


def moe_layer(x, xb, w_router, b_router, w_gate, w_up, w_down, g, beta):
    n, d = x.shape
    route, cnt = moe_router(x, w_router, b_router)
    e1 = route[:, 0].astype(I32)
    e2 = route[:, 1].astype(I32)
    counts = cnt[0, :N_EXPERTS].astype(I32)
    padded = (counts + MOE_TM - 1) // MOE_TM * MOE_TM
    pad_end = jnp.cumsum(padded)
    pad_start = pad_end - padded
    dest1 = pad_start[e1] + route[:, 4].astype(I32)
    dest2 = pad_start[e2] + route[:, 5].astype(I32)
    n_rows = 2 * n + N_EXPERTS * MOE_TM
    nt = n_rows // MOE_TM
    tok = jnp.arange(n, dtype=I32)
    row_tok = jnp.zeros((n_rows,), I32).at[dest1].set(tok).at[dest2].set(tok)
    tile_start = jnp.arange(nt, dtype=I32) * MOE_TM
    tile_e = jnp.minimum(jnp.searchsorted(pad_end, tile_start, side='right'), N_EXPERTS - 1).astype(I32)
    tile_v = (tile_start < pad_end[-1]).astype(I32)
    xs = row_gather(xb, row_tok)
    h = moe_up(xs, w_gate, w_up, tile_e, tile_v, tn=min(1024, w_gate.shape[2]))
    y_rows = moe_down(h, w_down, tile_e, tile_v, tn=min(512, d))
    return moe_combine(y_rows, x, route, dest1, dest2, g, beta)


def _cmp_sel_overlap(n_cmp_pad, n_sel):
    i = np.arange(n_cmp_pad)[:, None]
    j = np.arange(LANES)[None, :]
    lo = np.maximum(i * CMP_STRIDE, j * SEL_LEN)
    hi = np.minimum(i * CMP_STRIDE + CMP_LEN, (j + 1) * SEL_LEN)
    ov = np.maximum(hi - lo, 0) / CMP_LEN
    ov[:, n_sel:] = 0.0
    return ov.astype(np.float32)


def _gate_columns():
    src = -np.ones((KV_GROUPS * LANES,), np.int64)
    for g in range(KV_GROUPS):
        for j in range(HPG):
            for br in range(3):
                src[g * LANES + br * HPG + j] = g * HPG * 3 + j * 3 + br
    return src


def nsa_layer(xb, b, s, w_in, pe_k, w1_k, w2_k, pe_v, w1_v, w2_v, band, cmpb):
    n = xb.shape[0]
    hd = N_HEADS * HEAD_DIM
    gw = KV_GROUPS * HEAD_DIM
    w_main = w_in[:, :hd + 6 * gw].astype(BF)
    scale = jnp.concatenate([jnp.full((hd,), HEAD_DIM ** -0.5 * LOG2E, F32), jnp.ones((6 * gw,), F32)])
    ybf = matmul_scaled(xb, w_main, scale, BF, tm=512, tn=1280)
    src = _gate_columns()
    w_gl = jnp.where(jnp.asarray(src >= 0), w_in[:, hd + 6 * gw:][:, np.maximum(src, 0)], 0.0).astype(BF)
    gl = matmul_scaled(xb, w_gl, jnp.ones((w_gl.shape[1],), F32), F32, tm=512, tn=512)
    cb = hd // gw
    kcmp = nsa_compress(ybf, b, s, cb + 0, pe_k, w1_k, w2_k)
    vcmp = nsa_compress(ybf, b, s, cb + 1, pe_v, w1_v, w2_v)
    overlap_t = jnp.asarray(_cmp_sel_overlap(s // CMP_STRIDE, s // SEL_LEN).T).astype(BF)
    oc, nsel = nsa_cmp_select(ybf, kcmp, vcmp, cmpb, overlap_t, b, s)
    c128 = hd // HEAD_DIM
    o = nsa_main(ybf, nsel, band, gl, oc, b, s,
                 col_ksl=c128 + 2 * KV_GROUPS, col_vsl=c128 + 3 * KV_GROUPS,
                 col_kw=c128 + 4 * KV_GROUPS, col_vw=c128 + 5 * KV_GROUPS)
    return o.reshape(n, hd)


def dsa_layer(xb, b, s, w_in, kv_norm_g, w_uk, w_uv, band):
    n = xb.shape[0]
    hd = N_HEADS * HEAD_DIM
    o_c = hd
    o_qi = hd + KV_RANK
    o_ki = o_qi + IDX_HEADS * IDX_DIM
    o_wi = o_ki + IDX_DIM
    w_q = jnp.concatenate([w_in[:, :hd], w_in[:, o_qi:o_ki]], axis=1).astype(BF)
    scale_q = jnp.concatenate([jnp.ones((hd,), F32), jnp.full((IDX_HEADS * IDX_DIM,), IDX_DIM ** -0.5, F32)])
    qbf = matmul_scaled(xb, w_q, scale_q, BF, tm=512, tn=1280)
    cn = matmul_rmsnorm(xb, w_in[:, o_c:o_qi].astype(BF), kv_norm_g, tm=512)
    w_kw = jnp.pad(w_in[:, o_ki:], ((0, 0), (0, LANES - IDX_DIM - IDX_HEADS))).astype(BF)
    scale_kw = jnp.concatenate([jnp.ones((IDX_DIM,), F32), jnp.full((IDX_HEADS,), IDX_HEADS ** -0.5, F32),
                                jnp.zeros((LANES - IDX_DIM - IDX_HEADS,), F32)])
    kw = matmul_scaled(xb, w_kw, scale_kw, F32, tm=512, tn=LANES)
    nmask = dsa_index(qbf, kw, b, s, col_qi=hd // (IDX_HEADS * IDX_DIM))
    o = dsa_attention(qbf, cn, nmask, band[:, :2], w_uk, w_uv, b, s)
    return o.reshape(n, hd)


def kernel(x, rel_bias, nsa_w_in, nsa_cmp_pe_k, nsa_cmp_w1_k, nsa_cmp_w2_k, nsa_cmp_pe_v, nsa_cmp_w1_v, nsa_cmp_w2_v, nsa_w_out, dsa_w_in, dsa_kv_norm_g, dsa_w_uk, dsa_w_uv, dsa_w_out, ffn_w_gate, ffn_w_up, ffn_w_down, moe_w_router, moe_b_router, moe_w_gate, moe_w_up, moe_w_down, ln_mix_g, ln_mix_b, ln_ffn_g, ln_ffn_b):
    b, s, d = x.shape
    n = b * s
    assert s % TQ == 0 and s // SEL_LEN <= LANES
    x0 = x.reshape(n, d)
    band, cmpb = _bias_tables(rel_bias)
    o = nsa_layer(x0.astype(BF), b, s, nsa_w_in, nsa_cmp_pe_k, nsa_cmp_w1_k, nsa_cmp_w2_k,
                  nsa_cmp_pe_v, nsa_cmp_w1_v, nsa_cmp_w2_v, band, cmpb)
    x1, x1b = matmul_residual_ln(o, nsa_w_out.astype(BF), x0, ln_mix_g[0], ln_mix_b[0], tm=256, tk=d)
    d_ff = ffn_w_gate.shape[1]
    hff = swiglu_up(x1b, ffn_w_gate.astype(BF), ffn_w_up.astype(BF), tm=512,
                    tn=d_ff // 4 if d_ff % (4 * LANES) == 0 else 512)
    x2, x2b = matmul_residual_ln(hff, ffn_w_down.astype(BF), x1, ln_ffn_g[0], ln_ffn_b[0], tm=256,
                                 tk=hff.shape[1] // 2)
    o = dsa_layer(x2b, b, s, dsa_w_in, dsa_kv_norm_g, dsa_w_uk, dsa_w_uv, band)
    x3, x3b = matmul_residual_ln(o, dsa_w_out.astype(BF), x2, ln_mix_g[1], ln_mix_b[1], tm=256, tk=d)
    out = moe_layer(x3, x3b, moe_w_router, moe_b_router, moe_w_gate, moe_w_up, moe_w_down,
                    ln_ffn_g[1], ln_ffn_b[1])
    return out.reshape(b, s, d)
```
